```python
import math
import jax, jax.numpy as jnp
from jax import lax
import numpy as np

D_MODEL = 1024
BATCH = 4
SEQ = 8192
DEPTH = 4

GRID_W = 64
CTX_LEN = 256
EPS = 1e-6
N_MIXERS = 3
N_SSD_LAYERS = (DEPTH + 2) // 3
N_FNET_LAYERS = (DEPTH + 1) // 3
N_DIFF_LAYERS = DEPTH // 3

SSM_EXPAND = 2
D_INNER = SSM_EXPAND * D_MODEL
SSM_HEAD_DIM = 64
SSM_HEADS = D_INNER // SSM_HEAD_DIM
SSM_GROUPS = 8
SSM_STATE = 128
SSM_CONV = 5
SSM_CHUNK = 128
CONV_DIM = D_INNER + 2 * SSM_GROUPS * SSM_STATE
IN_PROJ_DIM = D_INNER + CONV_DIM + 2 * SSM_HEADS

FNET_GROUPS = 4
FNET_GROUP_DIM = D_MODEL // FNET_GROUPS

DIFF_HEADS = 8
DIFF_HEAD_DIM = 64
DIFF_V_DIM = 2 * DIFF_HEAD_DIM
DIFF_SCALE = DIFF_HEAD_DIM ** -0.5
ROPE_THETA = 10000.0
ROPE_PAIRS = DIFF_HEAD_DIM // 4
Q_BLOCK = 128

N_EXPERTS = 16
EXPERT_FF = 2048
EC_CAPACITY_FACTOR = 2

kernel_name = "hybrid_ssd_fnet_diffattn_ecmoe_dit"


def rmsnorm(x, g):
    xf = x.astype(jnp.float32)
    y = xf * lax.rsqrt(jnp.mean(xf * xf, axis=-1, keepdims=True) + EPS)
    return (y * g.astype(jnp.float32)).astype(x.dtype)


def modulate(h, shift, scale):
    return h * (1 + scale) + shift


def centred_dwconv(u, w, b):
    k = w.shape[0]
    out = lax.conv_general_dilated(u, w[:, None, :].astype(u.dtype), window_strides=(1,),
                                   padding=[(k // 2, k // 2)],
                                   dimension_numbers=("NWC", "WIO", "NWC"),
                                   feature_group_count=u.shape[-1])
    return out + b


def ssd_chunked_scan(xh, dt, a, bm, cm, h0):
    f32 = jnp.float32
    b, l, _, p = xh.shape
    g, n = bm.shape[2], bm.shape[3]
    r = SSM_HEADS // g
    q = SSM_CHUNK
    nc = l // q
    x = xh.astype(f32).reshape(b, nc, q, g, r, p)
    dtc = dt.astype(f32).reshape(b, nc, q, g, r)
    bc = bm.astype(f32).reshape(b, nc, q, g, n)
    cc = cm.astype(f32).reshape(b, nc, q, g, n)
    a_cs = jnp.cumsum(dtc * a.astype(f32).reshape(g, r), axis=2)
    causal = jnp.tril(jnp.ones((q, q), dtype=bool))[None, None, :, :, None, None]
    seg = a_cs[:, :, :, None] - a_cs[:, :, None, :]
    decay = jnp.exp(jnp.where(causal, seg, -jnp.inf))
    cb = jnp.einsum("bcign,bcjgn->bcijg", cc, bc)
    xdt = x * dtc[..., None]
    y_diag = jnp.einsum("bcijgr,bcjgrp->bcigrp", cb[..., None] * decay, xdt)
    to_end = jnp.exp(a_cs[:, :, -1:] - a_cs)
    states = jnp.einsum("bcjgn,bcjgrp->bcgrpn", bc, xdt * to_end[..., None])
    chunk_decay = jnp.exp(a_cs[:, :, -1])

    def step(h, inp):
        s, d = inp
        return d[..., None, None] * h + s, h

    h_final, h_in = lax.scan(step, h0.astype(f32),
                             (jnp.moveaxis(states, 1, 0), jnp.moveaxis(chunk_decay, 1, 0)))
    h_in = jnp.moveaxis(h_in, 0, 1)
    y_off = jnp.einsum("bcign,bcgrpn->bcigrp", cc, h_in) * jnp.exp(a_cs)[..., None]
    y = (y_diag + y_off).reshape(b, l, SSM_HEADS, p)
    return y.astype(xh.dtype), h_final


def ssd_project(h, in_w, conv_w, conv_b, dt_bias):
    b, l, _ = h.shape
    z, xbc, dt = jnp.split(h @ in_w, [D_INNER, D_INNER + CONV_DIM], axis=-1)
    xbc = jax.nn.silu(centred_dwconv(xbc, conv_w, conv_b))
    xs, bm, cm = jnp.split(xbc, [D_INNER, D_INNER + SSM_GROUPS * SSM_STATE], axis=-1)
    xs = xs.reshape(b, l, SSM_HEADS, SSM_HEAD_DIM)
    bm = bm.reshape(b, l, SSM_GROUPS, SSM_STATE)
    cm = cm.reshape(b, l, SSM_GROUPS, SSM_STATE)
    dt = jax.nn.softplus(dt.reshape(b, l, 2, SSM_HEADS) + dt_bias)
    return z, xs, bm, cm, dt


def ssd_bidirectional(xs, bm, cm, dt, a, d_skip, h0_fwd, h0_bwd):
    flip = lambda t: jnp.flip(t, axis=1)
    y_f, h_f = ssd_chunked_scan(xs, dt[:, :, 0], a[0], bm, cm, h0_fwd)
    y_b, h_b = ssd_chunked_scan(flip(xs), flip(dt[:, :, 1]), a[1], flip(bm), flip(cm), h0_bwd)
    y = y_f + flip(y_b) + d_skip[:, None] * xs
    return y, h_f, h_b


def ssd_output(y, z, norm_g, out_w):
    b, l = y.shape[:2]
    y = y.reshape(b, l, D_INNER) * jax.nn.silu(z)
    return rmsnorm(y, norm_g) @ out_w


def ssd_mixer(h_ctx, h_lat, in_w, conv_w, conv_b, dt_bias, a_log, d_skip, norm_g, out_w, need_ctx):
    a = -jnp.exp(a_log.astype(jnp.float32))
    b = h_ctx.shape[0]
    h0 = jnp.zeros((b, SSM_GROUPS, SSM_HEADS // SSM_GROUPS, SSM_HEAD_DIM, SSM_STATE), jnp.float32)
    z_c, xs_c, b_c, c_c, dt_c = ssd_project(h_ctx, in_w, conv_w, conv_b, dt_bias)
    y_c, h_fwd, h_bwd = ssd_bidirectional(xs_c, b_c, c_c, dt_c, a, d_skip, h0, h0)
    z_l, xs_l, b_l, c_l, dt_l = ssd_project(h_lat, in_w, conv_w, conv_b, dt_bias)
    y_l, _, _ = ssd_bidirectional(xs_l, b_l, c_l, dt_l, a, d_skip, h_fwd, h_bwd)
    out_lat = ssd_output(y_l, z_l, norm_g, out_w)
    out_ctx = ssd_output(y_c, z_c, norm_g, out_w) if need_ctx else None
    return out_ctx, out_lat


def fourier_mixer(h, out_w, out_b):
    b, l, _ = h.shape
    hg = h.astype(jnp.float32).reshape(b, l, FNET_GROUPS, FNET_GROUP_DIM)
    mixed = jnp.fft.fft2(hg, axes=(1, 3), norm="ortho").real
    return mixed.reshape(b, l, D_MODEL).astype(h.dtype) @ out_w + out_b


def axial_rope(l):
    rows = l // GRID_W
    row = jnp.repeat(jnp.arange(rows), GRID_W).astype(jnp.float32)
    col = jnp.tile(jnp.arange(GRID_W), rows).astype(jnp.float32)
    inv = ROPE_THETA ** (-jnp.arange(ROPE_PAIRS, dtype=jnp.float32) / ROPE_PAIRS)
    ang = jnp.stack([row[:, None] * inv, col[:, None] * inv], axis=1)
    return jnp.cos(ang), jnp.sin(ang)


def apply_axial_rope(t, cos, sin):
    shp = t.shape
    t = t.reshape(*shp[:-1], 2, 2, ROPE_PAIRS)
    t1, t2 = t[..., 0, :], t[..., 1, :]
    cs = cos[None, :, None, None].astype(t.dtype)
    sn = sin[None, :, None, None].astype(t.dtype)
    out = jnp.stack([t1 * cs - t2 * sn, t2 * cs + t1 * sn], axis=-2)
    return out.reshape(shp)


def diff_attend(q, k, v, lam):
    s = jnp.einsum("bqhcd,bkhcd->bhcqk", q, k).astype(jnp.float32) * DIFF_SCALE
    p = jax.nn.softmax(s, axis=-1)
    a = p[:, :, 0] - lam * p[:, :, 1]
    return jnp.einsum("bhqk,bkhe->bqhe", a.astype(v.dtype), v)


def diff_attention_mixer(h_ctx, h_lat, qkv_w, out_w, lam_params, subln_g, layer_idx, need_ctx):
    lambda_init = 0.8 - 0.6 * math.exp(-0.3 * layer_idx)
    lp = lam_params.astype(jnp.float32)
    lam = jnp.exp(jnp.sum(lp[0] * lp[1])) - jnp.exp(jnp.sum(lp[2] * lp[3])) + lambda_init

    def qkv(h):
        b, l, _ = h.shape
        q, k, v = jnp.split(h @ qkv_w, 3, axis=-1)
        return (q.reshape(b, l, DIFF_HEADS, 2, DIFF_HEAD_DIM),
                k.reshape(b, l, DIFF_HEADS, 2, DIFF_HEAD_DIM),
                v.reshape(b, l, DIFF_HEADS, DIFF_V_DIM))

    def finish(o):
        b, l = o.shape[:2]
        o = rmsnorm(o, subln_g) * (1 - lambda_init)
        return o.reshape(b, l, D_MODEL) @ out_w

    q_c, k_c, v_c = qkv(h_ctx)
    q_l, k_l, v_l = qkv(h_lat)
    b, l = h_lat.shape[:2]
    cos, sin = axial_rope(l)
    q_l = apply_axial_rope(q_l, cos, sin)
    k_l = apply_axial_rope(k_l, cos, sin)
    k_all = jnp.concatenate([k_l, k_c], axis=1)
    v_all = jnp.concatenate([v_l, v_c], axis=1)
    nb = l // Q_BLOCK
    q_blocks = jnp.moveaxis(q_l.reshape(b, nb, Q_BLOCK, DIFF_HEADS, 2, DIFF_HEAD_DIM), 1, 0)
    o_l = lax.map(lambda qb: diff_attend(qb, k_all, v_all, lam), q_blocks)
    o_l = jnp.moveaxis(o_l, 0, 1).reshape(b, l, DIFF_HEADS, DIFF_V_DIM)
    out_lat = finish(o_l)
    out_ctx = finish(diff_attend(q_c, k_c, v_c, lam)) if need_ctx else None
    return out_ctx, out_lat


def expert_choice_moe(h, router_w, w1, w3, w2):
    b, n, _ = h.shape
    cap = EC_CAPACITY_FACTOR * n // N_EXPERTS
    aff = jax.nn.softmax((h @ router_w).astype(jnp.float32), axis=-1)
    gate, idx = lax.top_k(jnp.swapaxes(aff, 1, 2), cap)
    bi = jnp.arange(b)[:, None, None]
    xs = h[bi, idx]
    hid = jax.nn.silu(jnp.einsum("becd,edf->becf", xs, w1)) * jnp.einsum("becd,edf->becf", xs, w3)
    ys = jnp.einsum("becf,efd->becd", hid, w2) * gate[..., None].astype(h.dtype)
    return jnp.zeros_like(h).at[bi, idx].add(ys)


def setup_inputs(seed: int = 0) -> dict:
    key = jax.random.key(seed)
    ks = jax.random.split(key, 32)
    f32 = jnp.float32
    nrm = lambda k, shp, s: jax.random.normal(k, shp, f32) * s
    dt0 = jnp.exp(jax.random.uniform(ks[10], (N_SSD_LAYERS, 2, SSM_HEADS), f32,
                                     math.log(1e-3), math.log(1e-1)))
    return {
        "x": nrm(ks[0], (BATCH, SEQ, D_MODEL), 1.0),
        "c": nrm(ks[1], (BATCH, D_MODEL), 1.0),
        "ctx": nrm(ks[2], (BATCH, CTX_LEN, D_MODEL), 1.0),
        "c_ctx": nrm(ks[3], (D_MODEL,), 1.0),
        "mod_w": nrm(ks[4], (DEPTH, D_MODEL, 6 * D_MODEL), 0.5 * D_MODEL ** -0.5),
        "mod_b": nrm(ks[5], (DEPTH, 6 * D_MODEL), 0.01),
        "norm1_g": 1.0 + nrm(ks[6], (DEPTH, D_MODEL), 0.05),
        "norm2_g": 1.0 + nrm(ks[7], (DEPTH, D_MODEL), 0.05),
        "ssd_in_w": nrm(ks[8], (N_SSD_LAYERS, D_MODEL, IN_PROJ_DIM), D_MODEL ** -0.5),
        "ssd_conv_w": nrm(ks[9], (N_SSD_LAYERS, SSM_CONV, CONV_DIM), SSM_CONV ** -0.5),
        "ssd_conv_b": nrm(ks[11], (N_SSD_LAYERS, CONV_DIM), 0.01),
        "ssd_dt_bias": dt0 + jnp.log(-jnp.expm1(-dt0)),
        "ssd_a_log": jnp.log(jax.random.uniform(ks[12], (N_SSD_LAYERS, 2, SSM_HEADS), f32, 1.0, 16.0)),
        "ssd_d": 1.0 + nrm(ks[13], (N_SSD_LAYERS, SSM_HEADS), 0.05),
        "ssd_norm_g": 1.0 + nrm(ks[14], (N_SSD_LAYERS, D_INNER), 0.05),
        "ssd_out_w": nrm(ks[15], (N_SSD_LAYERS, D_INNER, D_MODEL), D_INNER ** -0.5),
        "fnet_out_w": nrm(ks[16], (N_FNET_LAYERS, D_MODEL, D_MODEL), D_MODEL ** -0.5),
        "fnet_out_b": nrm(ks[17], (N_FNET_LAYERS, D_MODEL), 0.01),
        "diff_qkv_w": nrm(ks[18], (N_DIFF_LAYERS, D_MODEL, 3 * D_MODEL), D_MODEL ** -0.5),
        "diff_out_w": nrm(ks[19], (N_DIFF_LAYERS, D_MODEL, D_MODEL), D_MODEL ** -0.5),
        "diff_lambda": nrm(ks[20], (N_DIFF_LAYERS, 4, DIFF_HEAD_DIM), 0.1),
        "diff_subln_g": 1.0 + nrm(ks[21], (N_DIFF_LAYERS, DIFF_V_DIM), 0.05),
        "router_w": nrm(ks[22], (DEPTH, D_MODEL, N_EXPERTS), D_MODEL ** -0.5),
        "moe_w1": nrm(ks[23], (DEPTH, N_EXPERTS, D_MODEL, EXPERT_FF), D_MODEL ** -0.5),
        "moe_w3": nrm(ks[24], (DEPTH, N_EXPERTS, D_MODEL, EXPERT_FF), D_MODEL ** -0.5),
        "moe_w2": nrm(ks[25], (DEPTH, N_EXPERTS, EXPERT_FF, D_MODEL), EXPERT_FF ** -0.5),
        "final_g": 1.0 + nrm(ks[26], (D_MODEL,), 0.05),
    }


def reference(x, c, ctx, c_ctx, mod_w, mod_b, norm1_g, norm2_g,
              ssd_in_w, ssd_conv_w, ssd_conv_b, ssd_dt_bias, ssd_a_log, ssd_d, ssd_norm_g, ssd_out_w,
              fnet_out_w, fnet_out_b, diff_qkv_w, diff_out_w, diff_lambda, diff_subln_g,
              router_w, moe_w1, moe_w3, moe_w2, final_g):
    cond_lat = jax.nn.silu(c)
    cond_ctx = jax.nn.silu(c_ctx)[None]
    for i in range(DEPTH):
        need_ctx = i < DEPTH - 1
        kind, j = i % N_MIXERS, i // N_MIXERS
        m_l = jnp.split((cond_lat @ mod_w[i] + mod_b[i])[:, None, :], 6, axis=-1)
        m_c = jnp.split((cond_ctx @ mod_w[i] + mod_b[i])[:, None, :], 6, axis=-1)
        h_lat = modulate(rmsnorm(x, norm1_g[i]), m_l[0], m_l[1])
        h_ctx = modulate(rmsnorm(ctx, norm1_g[i]), m_c[0], m_c[1])
        if kind == 0:
            y_ctx, y_lat = ssd_mixer(h_ctx, h_lat, ssd_in_w[j], ssd_conv_w[j], ssd_conv_b[j],
                                     ssd_dt_bias[j], ssd_a_log[j], ssd_d[j], ssd_norm_g[j],
                                     ssd_out_w[j], need_ctx)
        elif kind == 1:
            y_lat = fourier_mixer(h_lat, fnet_out_w[j], fnet_out_b[j])
            y_ctx = fourier_mixer(h_ctx, fnet_out_w[j], fnet_out_b[j]) if need_ctx else None
        else:
            y_ctx, y_lat = diff_attention_mixer(h_ctx, h_lat, diff_qkv_w[j], diff_out_w[j],
                                                diff_lambda[j], diff_subln_g[j], i, need_ctx)
        x = x + m_l[2] * y_lat
        x = x + m_l[5] * expert_choice_moe(modulate(rmsnorm(x, norm2_g[i]), m_l[3], m_l[4]),
                                           router_w[i], moe_w1[i], moe_w3[i], moe_w2[i])
        if need_ctx:
            ctx = ctx + m_c[2] * y_ctx
            ctx = ctx + m_c[5] * expert_choice_moe(modulate(rmsnorm(ctx, norm2_g[i]), m_c[3], m_c[4]),
                                                   router_w[i], moe_w1[i], moe_w3[i], moe_w2[i])
    return rmsnorm(x, final_g)
```

```python
import functools
import math

import jax
import jax.numpy as jnp
from jax import lax
from jax.experimental import pallas as pl
from jax.experimental.pallas import tpu as pltpu

F32 = jnp.float32
BF16 = jnp.bfloat16

EPS = 1e-6
DEPTH = 4
N_MIXERS = 3
GRID_W = 64
ROPE_THETA = 10000.0

SSM_HEAD_DIM = 64
SSM_HEADS = 32
SSM_GROUPS = 8
SSM_STATE = 128
SSM_CHUNK = 128
SSM_CONV = 5
HEADS_PER_GROUP = SSM_HEADS // SSM_GROUPS
GROUP_WIDTH = HEADS_PER_GROUP * SSM_HEAD_DIM

FNET_GROUPS = 4

DIFF_HEADS = 8
DIFF_HEAD_DIM = 64
DIFF_V_DIM = 2 * DIFF_HEAD_DIM
DIFF_SCALE = DIFF_HEAD_DIM ** -0.5
ROPE_PAIRS = DIFF_HEAD_DIM // 4

N_EXPERTS = 16
EC_CAPACITY_FACTOR = 2

LANES = 128
VMEM_LIMIT_BYTES = 56 * 1024 * 1024


def _compiler_params(semantics):
    return pltpu.CompilerParams(dimension_semantics=semantics, vmem_limit_bytes=VMEM_LIMIT_BYTES)


def _split_bf16(x, parts):
    out = []
    r = x
    for _ in range(parts):
        p = r.astype(BF16)
        out.append(p)
        r = r - p.astype(F32)
    return out


def _dot(a, b):
    return jnp.dot(a, b, preferred_element_type=F32)


def _dot_nt(a, b):
    return lax.dot_general(a, b, (((1,), (1,)), ((), ())), preferred_element_type=F32)


def _dot_exact_rhs(x, m_bf16, parts=3, nt=False):
    f = _dot_nt if nt else _dot
    acc = None
    for p in _split_bf16(x, parts):
        t = f(p, m_bf16)
        acc = t if acc is None else acc + t
    return acc


def _dot_exact_lhs(m_bf16, x, parts=3):
    acc = None
    for p in _split_bf16(x, parts):
        t = _dot(m_bf16, p)
        acc = t if acc is None else acc + t
    return acc


def _mm_kernel(*refs, prologue, precise, has_bias, has_gate_res, emit_h, nk):
    it = iter(refs)
    x_ref = next(it)
    z_ref = g_ref = sh_ref = sc_ref = None
    if prologue == "norm_mod":
        g_ref, sh_ref, sc_ref = next(it), next(it), next(it)
    elif prologue == "gated_norm":
        z_ref, g_ref = next(it), next(it)
    w_ref = next(it)
    b_ref = next(it) if has_bias else None
    gate_ref = res_ref = None
    if has_gate_res:
        gate_ref, res_ref = next(it), next(it)
    o_ref = next(it)
    hout_ref = next(it) if emit_h else None
    h_ref = next(it) if prologue else None
    acc_ref = next(it) if nk > 1 else None

    j = pl.program_id(1)
    k = pl.program_id(2)

    if prologue:
        @pl.when(j == 0)
        def _():
            x = x_ref[...].astype(F32)
            if prologue == "gated_norm":
                z = z_ref[...].astype(F32)
                x = x * (z * jax.nn.sigmoid(z))
            ms = jnp.mean(x * x, axis=-1, keepdims=True)
            y = x * lax.rsqrt(ms + EPS) * g_ref[...]
            if prologue == "norm_mod":
                y = y * (1.0 + sc_ref[...]) + sh_ref[...]
            h_ref[...] = y.astype(h_ref.dtype)
            if emit_h:
                hout_ref[...] = y.astype(hout_ref.dtype)
        lhs = h_ref[...]
    else:
        lhs = x_ref[...]

    w = w_ref[...]
    if precise:
        l_hi, l_lo = _split_bf16(lhs.astype(F32), 2)
        w_hi, w_lo = _split_bf16(w.astype(F32), 2)
        part = _dot(l_hi, w_hi) + (_dot(l_lo, w_hi) + _dot(l_hi, w_lo))
    else:
        part = _dot(lhs.astype(BF16), w.astype(BF16))

    def finish(acc):
        if has_bias:
            acc = acc + b_ref[...]
        if has_gate_res:
            acc = res_ref[...].astype(F32) + gate_ref[...] * acc
        o_ref[...] = acc.astype(o_ref.dtype)

    if nk == 1:
        finish(part)
    else:
        @pl.when(k == 0)
        def _():
            acc_ref[...] = part

        @pl.when(k > 0)
        def _():
            acc_ref[...] += part

        @pl.when(k == nk - 1)
        def _():
            finish(acc_ref[...])


def fused_matmul(x, w, *, prologue=None, g=None, shift=None, scale=None, z=None, bias=None,
                 gate=None, res=None, rows_per_batch=None, precise=False, out_dtype=F32,
                 emit_h=None, tm=512, tn=1024, tk=None, name="fused_matmul"):
    m, kdim = x.shape
    n = w.shape[1]
    rows_per_batch = m if rows_per_batch is None else rows_per_batch
    tm = min(tm, m, rows_per_batch)
    tn = min(tn, n)
    tk = kdim if tk is None else min(tk, kdim)
    assert m % tm == 0 and n % tn == 0 and kdim % tk == 0, (x.shape, w.shape, tm, tn, tk)
    nk = kdim // tk
    assert not (prologue and nk > 1)
    assert rows_per_batch % tm == 0
    tiles_per_batch = rows_per_batch // tm

    def bidx(i):
        return i // tiles_per_batch

    args = [x]
    in_specs = [pl.BlockSpec((tm, tk), lambda i, j, k: (i, k))]
    if prologue == "norm_mod":
        args += [g.reshape(1, kdim), shift, scale]
        in_specs += [pl.BlockSpec((1, kdim), lambda i, j, k: (0, 0)),
                     pl.BlockSpec((None, 1, kdim), lambda i, j, k: (bidx(i), 0, 0)),
                     pl.BlockSpec((None, 1, kdim), lambda i, j, k: (bidx(i), 0, 0))]
    elif prologue == "gated_norm":
        args += [z, g.reshape(1, kdim)]
        in_specs += [pl.BlockSpec((tm, kdim), lambda i, j, k: (i, 0)),
                     pl.BlockSpec((1, kdim), lambda i, j, k: (0, 0))]
    args.append(w)
    in_specs.append(pl.BlockSpec((tk, tn), lambda i, j, k: (k, j)))
    if bias is not None:
        args.append(bias.reshape(1, n))
        in_specs.append(pl.BlockSpec((1, tn), lambda i, j, k: (0, j)))
    if gate is not None:
        args += [gate, res]
        in_specs += [pl.BlockSpec((None, 1, tn), lambda i, j, k: (bidx(i), 0, j)),
                     pl.BlockSpec((tm, tn), lambda i, j, k: (i, j))]

    out_shape = [jax.ShapeDtypeStruct((m, n), out_dtype)]
    out_specs = [pl.BlockSpec((tm, tn), lambda i, j, k: (i, j))]
    if emit_h is not None:
        out_shape.append(jax.ShapeDtypeStruct((m, kdim), emit_h))
        out_specs.append(pl.BlockSpec((tm, kdim), lambda i, j, k: (i, 0)))

    scratch = []
    if prologue:
        scratch.append(pltpu.VMEM((tm, kdim), F32 if precise else BF16))
    if nk > 1:
        scratch.append(pltpu.VMEM((tm, tn), F32))

    kern = functools.partial(_mm_kernel, prologue=prologue, precise=precise, has_bias=bias is not None,
                             has_gate_res=gate is not None, emit_h=emit_h is not None, nk=nk)
    outs = pl.pallas_call(
        kern, grid=(m // tm, n // tn, nk), in_specs=in_specs, out_specs=out_specs, out_shape=out_shape,
        scratch_shapes=scratch, name=name,
        compiler_params=_compiler_params(("parallel", "arbitrary", "arbitrary")))(*args)
    return outs if emit_h is not None else outs[0]


def _ssd_kernel(*refs, reverse, combine, q):
    it = iter(refs)
    xs_ref, bm_ref, cm_ref, dt_ref, dtt_ref, a_ref, at_ref, h0_ref = (next(it) for _ in range(8))
    yprev_ref = dskip_ref = None
    if combine:
        yprev_ref, dskip_ref = next(it), next(it)
    y_ref, hfin_ref, state_ref = next(it), next(it), next(it)

    c = pl.program_id(1)

    @pl.when(c == 0)
    def _():
        state_ref[...] = h0_ref[...]

    ii = lax.broadcasted_iota(jnp.int32, (q, q), 0)
    jj = lax.broadcasted_iota(jnp.int32, (q, q), 1)
    mask = (ii <= jj) if reverse else (ii >= jj)
    tri = jnp.where(mask, 1.0, 0.0).astype(BF16)

    dt = dt_ref[...]
    dta = dt * a_ref[...]
    dta_t = dtt_ref[...] * at_ref[...]
    a_cs = _dot_exact_lhs(tri, dta)
    a_cs_t = _dot_exact_rhs(dta_t, tri, nt=True)
    edge = 0 if reverse else q - 1
    tot = a_cs[edge:edge + 1, :]
    to_end = jnp.exp(tot - a_cs)
    into = jnp.exp(a_cs)
    chunk_decay = jnp.exp(tot)

    hh = lax.broadcasted_iota(jnp.int32, (SSM_HEADS, SSM_HEADS * SSM_HEAD_DIM), 0)
    cc = lax.broadcasted_iota(jnp.int32, (SSM_HEADS, SSM_HEADS * SSM_HEAD_DIM), 1)
    expand = jnp.where(cc // SSM_HEAD_DIM == hh, 1.0, 0.0).astype(BF16)
    dt_x = _dot_exact_rhs(dt, expand, parts=2)
    to_end_x = _dot_exact_rhs(to_end, expand, parts=2)
    into_x = _dot_exact_rhs(into, expand, parts=2)
    decay_x = _dot_exact_rhs(jnp.broadcast_to(chunk_decay, (8, SSM_HEADS)), expand, parts=3)[0:1, :]

    for g in range(SSM_GROUPS):
        c0 = g * GROUP_WIDTH
        xs_g = xs_ref[:, c0:c0 + GROUP_WIDTH]
        bm_g = bm_ref[:, g * SSM_STATE:(g + 1) * SSM_STATE]
        cm_g = cm_ref[:, g * SSM_STATE:(g + 1) * SSM_STATE].astype(BF16)
        xdt_g = xs_g * dt_x[:, c0:c0 + GROUP_WIDTH]
        xdt_b = xdt_g.astype(BF16)
        cb = _dot_nt(cm_g, bm_g.astype(BF16))
        st = state_ref[g]
        y_g = _dot(cm_g, st.astype(BF16)) * into_x[:, c0:c0 + GROUP_WIDTH]
        if combine:
            y_g = y_g + yprev_ref[:, c0:c0 + GROUP_WIDTH] + dskip_ref[:, c0:c0 + GROUP_WIDTH] * xs_g
        y_ref[:, c0:c0 + GROUP_WIDTH] = y_g
        for r in range(HEADS_PER_GROUP):
            h = g * HEADS_PER_GROUP + r
            seg = a_cs[:, h:h + 1] - a_cs_t[h:h + 1, :]
            lmat = (jnp.exp(jnp.where(mask, seg, -1e30)) * cb).astype(BF16)
            lo = c0 + r * SSM_HEAD_DIM
            y_ref[:, lo:lo + SSM_HEAD_DIM] += _dot(lmat, xdt_b[:, r * SSM_HEAD_DIM:(r + 1) * SSM_HEAD_DIM])
        xdtw = (xdt_g * to_end_x[:, c0:c0 + GROUP_WIDTH]).astype(BF16)
        new = _dot(bm_g.T.astype(BF16), xdtw)
        state_ref[g] = st * decay_x[:, c0:c0 + GROUP_WIDTH] + new

    @pl.when(c == pl.num_programs(1) - 1)
    def _():
        hfin_ref[...] = state_ref[...]


def ssd_scan(xbc, dt, dt_t, a, h0, *, reverse, y_prev=None, d_skip=None, name="ssd_scan"):
    b, l, _ = xbc.shape
    q = SSM_CHUNK
    nc = l // q
    d_inner = SSM_HEADS * SSM_HEAD_DIM
    nbc = SSM_GROUPS * SSM_STATE
    combine = y_prev is not None

    def cidx(c):
        return (nc - 1 - c) if reverse else c

    xs_blocks = d_inner // d_inner
    in_specs = [
        pl.BlockSpec((None, q, d_inner), lambda i, c: (i, cidx(c), 0)),
        pl.BlockSpec((None, q, nbc), lambda i, c: (i, cidx(c), d_inner // nbc)),
        pl.BlockSpec((None, q, nbc), lambda i, c: (i, cidx(c), d_inner // nbc + 1)),
        pl.BlockSpec((None, q, SSM_HEADS), lambda i, c: (i, cidx(c), 0)),
        pl.BlockSpec((None, SSM_HEADS, q), lambda i, c: (i, 0, cidx(c))),
        pl.BlockSpec((1, SSM_HEADS), lambda i, c: (0, 0)),
        pl.BlockSpec((SSM_HEADS, 1), lambda i, c: (0, 0)),
        pl.BlockSpec((None, SSM_GROUPS, SSM_STATE, GROUP_WIDTH), lambda i, c: (i, 0, 0, 0)),
    ]
    del xs_blocks
    args = [xbc, xbc, xbc, dt, dt_t, a.reshape(1, SSM_HEADS), a.reshape(SSM_HEADS, 1), h0]
    if combine:
        in_specs += [pl.BlockSpec((None, q, d_inner), lambda i, c: (i, cidx(c), 0)),
                     pl.BlockSpec((1, d_inner), lambda i, c: (0, 0))]
        args += [y_prev, jnp.repeat(d_skip, SSM_HEAD_DIM).reshape(1, d_inner)]
    out_shape = [jax.ShapeDtypeStruct((b, l, d_inner), F32),
                 jax.ShapeDtypeStruct((b, SSM_GROUPS, SSM_STATE, GROUP_WIDTH), F32)]
    out_specs = [pl.BlockSpec((None, q, d_inner), lambda i, c: (i, cidx(c), 0)),
                 pl.BlockSpec((None, SSM_GROUPS, SSM_STATE, GROUP_WIDTH), lambda i, c: (i, 0, 0, 0))]
    kern = functools.partial(_ssd_kernel, reverse=reverse, combine=combine, q=q)
    return pl.pallas_call(
        kern, grid=(b, nc), in_specs=in_specs, out_specs=out_specs, out_shape=out_shape,
        scratch_shapes=[pltpu.VMEM((SSM_GROUPS, SSM_STATE, GROUP_WIDTH), F32)], name=name,
        compiler_params=_compiler_params(("parallel", "arbitrary")))(*args)


def _rope(x, cos, sin_up, sin_dn):
    return (x * cos + pltpu.roll(x, ROPE_PAIRS, 1) * sin_up
            + pltpu.roll(x, LANES - ROPE_PAIRS, 1) * sin_dn)


def _attn_kernel(*refs, n_lat, n_ctx, tk, rope, out_scale):
    it = iter(refs)
    q_ref = next(it)
    kl_ref = vl_ref = None
    if n_lat:
        kl_ref, vl_ref = next(it), next(it)
    kc_ref, vc_ref = next(it), next(it)
    if rope:
        cq_ref, suq_ref, sdq_ref, ck_ref, suk_ref, sdk_ref = (next(it) for _ in range(6))
    lam_ref, g_ref, o_ref, k_s, v_s, m_s, l_s, acc_s = (next(it) for _ in range(8))

    qi = pl.program_id(2)

    @pl.when(qi == 0)
    def _():
        if n_lat:
            k = kl_ref[...]
            if rope:
                k = _rope(k, ck_ref[...], suk_ref[...], sdk_ref[...])
            k_s[0:n_lat, :] = k.astype(BF16)
            v_s[0:n_lat, :] = vl_ref[...].astype(BF16)
        k_s[n_lat:n_lat + n_ctx, :] = kc_ref[...].astype(BF16)
        v_s[n_lat:n_lat + n_ctx, :] = vc_ref[...].astype(BF16)

    q = q_ref[...] * DIFF_SCALE
    if rope:
        q = _rope(q, cq_ref[...], suq_ref[...], sdq_ref[...])
    lane = lax.broadcasted_iota(jnp.int32, q.shape, 1)
    qs = (jnp.where(lane < DIFF_HEAD_DIM, q, 0.0).astype(BF16),
          jnp.where(lane >= DIFF_HEAD_DIM, q, 0.0).astype(BF16))

    m_s[...] = jnp.full(m_s.shape, -1e30, F32)
    l_s[...] = jnp.zeros(l_s.shape, F32)
    acc_s[...] = jnp.zeros(acc_s.shape, F32)

    def body(t, carry):
        off = pl.multiple_of(t * tk, tk)
        kt = k_s[pl.ds(off, tk), :]
        vt = v_s[pl.ds(off, tk), :]
        for comp in range(2):
            s = _dot_nt(qs[comp], kt)
            m_old = m_s[comp]
            m_new = jnp.maximum(m_old, jnp.max(s, axis=-1, keepdims=True))
            alpha = jnp.exp(m_old - m_new)
            p = jnp.exp(s - m_new)
            l_s[comp] = alpha * l_s[comp] + jnp.sum(p, axis=-1, keepdims=True)
            acc_s[comp] = alpha * acc_s[comp] + _dot(p.astype(BF16), vt)
            m_s[comp] = m_new
        return carry

    lax.fori_loop(0, (n_lat + n_ctx) // tk, body, 0)

    o = acc_s[0] / l_s[0] - lam_ref[0:1, 0:1] * (acc_s[1] / l_s[1])
    ms = jnp.mean(o * o, axis=-1, keepdims=True)
    o_ref[...] = o * lax.rsqrt(ms + EPS) * (g_ref[...] * out_scale)


def diff_attention(qkv_q, qkv_lat, qkv_ctx, lam, subln_g, out_scale, rope_tabs, *, tq, tk, name):
    b, lq, d3 = qkv_q.shape
    d = d3 // 3
    nh = d // DIFF_V_DIM
    n_lat = 0 if qkv_lat is None else qkv_lat.shape[1]
    n_ctx = qkv_ctx.shape[1]
    rope = rope_tabs is not None
    tq = min(tq, lq)
    tk = min(tk, n_lat + n_ctx)
    assert lq % tq == 0 and (n_lat + n_ctx) % tk == 0

    args = [qkv_q]
    in_specs = [pl.BlockSpec((None, tq, DIFF_V_DIM), lambda i, h, t: (i, t, h))]
    if n_lat:
        args += [qkv_lat, qkv_lat]
        in_specs += [pl.BlockSpec((None, n_lat, DIFF_V_DIM), lambda i, h, t: (i, 0, nh + h)),
                     pl.BlockSpec((None, n_lat, DIFF_V_DIM), lambda i, h, t: (i, 0, 2 * nh + h))]
    args += [qkv_ctx, qkv_ctx]
    in_specs += [pl.BlockSpec((None, n_ctx, DIFF_V_DIM), lambda i, h, t: (i, 0, nh + h)),
                 pl.BlockSpec((None, n_ctx, DIFF_V_DIM), lambda i, h, t: (i, 0, 2 * nh + h))]
    if rope:
        args += list(rope_tabs) + list(rope_tabs)
        in_specs += [pl.BlockSpec((tq, LANES), lambda i, h, t: (t, 0))] * 3
        in_specs += [pl.BlockSpec((n_lat, LANES), lambda i, h, t: (0, 0))] * 3
    args += [jnp.broadcast_to(lam.astype(F32).reshape(1, 1), (8, LANES)), subln_g.reshape(1, DIFF_V_DIM)]
    in_specs += [pl.BlockSpec((8, LANES), lambda i, h, t: (0, 0)),
                 pl.BlockSpec((1, DIFF_V_DIM), lambda i, h, t: (0, 0))]

    kern = functools.partial(_attn_kernel, n_lat=n_lat, n_ctx=n_ctx, tk=tk, rope=rope, out_scale=out_scale)
    return pl.pallas_call(
        kern, grid=(b, nh, lq // tq), in_specs=in_specs,
        out_specs=pl.BlockSpec((None, tq, DIFF_V_DIM), lambda i, h, t: (i, t, h)),
        out_shape=jax.ShapeDtypeStruct((b, lq, d), F32),
        scratch_shapes=[pltpu.VMEM((n_lat + n_ctx, DIFF_V_DIM), BF16),
                        pltpu.VMEM((n_lat + n_ctx, DIFF_V_DIM), BF16),
                        pltpu.VMEM((2, tq, 1), F32), pltpu.VMEM((2, tq, 1), F32),
                        pltpu.VMEM((2, tq, DIFF_V_DIM), F32)],
        name=name,
        compiler_params=_compiler_params(("parallel", "parallel", "arbitrary")))(*args)


def _rope_tables(l):
    rows = l // GRID_W
    row = jnp.repeat(jnp.arange(rows), GRID_W).astype(F32)
    col = jnp.tile(jnp.arange(GRID_W), rows).astype(F32)
    inv = ROPE_THETA ** (-jnp.arange(ROPE_PAIRS, dtype=F32) / ROPE_PAIRS)
    lane = jnp.arange(LANES)
    within = lane % DIFF_HEAD_DIM
    axis = within // (2 * ROPE_PAIRS)
    second = (within // ROPE_PAIRS) % 2
    pos = jnp.where(axis[None, :] == 0, row[:, None], col[:, None])
    ang = pos * inv[within % ROPE_PAIRS][None, :]
    cos, sin = jnp.cos(ang), jnp.sin(ang)
    sin_up = jnp.where(second[None, :] == 1, sin, 0.0)
    sin_dn = jnp.where(second[None, :] == 0, -sin, 0.0)
    return cos, sin_up, sin_dn


def _ffn_kernel(x_ref, w1_ref, w3_ref, w2_ref, gate_ref, o_ref):
    x = x_ref[...]
    h1 = _dot(x, w1_ref[...])
    h3 = _dot(x, w3_ref[...])
    hid = (h1 * jax.nn.sigmoid(h1) * h3).astype(BF16)
    o_ref[...] = _dot(hid, w2_ref[...]) * gate_ref[...]


def expert_ffn(xs, w1, w3, w2, gate, *, tm=512, name="expert_ffn"):
    e, r, d = xs.shape
    f = w1.shape[2]
    tm = min(tm, r)
    assert r % tm == 0
    return pl.pallas_call(
        _ffn_kernel, grid=(e, r // tm),
        in_specs=[pl.BlockSpec((None, tm, d), lambda i, t: (i, t, 0)),
                  pl.BlockSpec((None, d, f), lambda i, t: (i, 0, 0)),
                  pl.BlockSpec((None, d, f), lambda i, t: (i, 0, 0)),
                  pl.BlockSpec((None, f, d), lambda i, t: (i, 0, 0)),
                  pl.BlockSpec((None, tm, 1), lambda i, t: (i, t, 0))],
        out_specs=pl.BlockSpec((None, tm, d), lambda i, t: (i, t, 0)),
        out_shape=jax.ShapeDtypeStruct((e, r, d), F32), name=name,
        compiler_params=_compiler_params(("parallel", "arbitrary")))(xs, w1, w3, w2, gate)


def _pad_cols(w, n):
    return jnp.pad(w, ((0, 0), (0, n - w.shape[1])))


def _ssd_project(x2d, g, shift, scale, rows_per_batch, w_main, w_dt, conv_w, conv_b, dt_bias, b, l):
    d_inner = SSM_HEADS * SSM_HEAD_DIM
    main = fused_matmul(x2d, w_main, prologue="norm_mod", g=g, shift=shift, scale=scale,
                        rows_per_batch=rows_per_batch, name="ssd_in_proj")
    dt_raw = fused_matmul(x2d, w_dt, prologue="norm_mod", g=g, shift=shift, scale=scale,
                          rows_per_batch=rows_per_batch, precise=True, name="ssd_dt_proj")
    main = main.reshape(b, l, -1)
    z = main[..., :d_inner]
    xbc = main[..., d_inner:]
    k = conv_w.shape[0]
    conv = lax.conv_general_dilated(xbc, conv_w[:, None, :], window_strides=(1,), padding=[(k // 2, k // 2)],
                                    dimension_numbers=("NWC", "WIO", "NWC"), feature_group_count=xbc.shape[-1])
    xbc = jax.nn.silu(conv + conv_b)
    dt = jax.nn.softplus(dt_raw[:, :2 * SSM_HEADS].reshape(b, l, 2, SSM_HEADS) + dt_bias)
    return z, xbc, dt


def _ssd_bidirectional(xbc, dt, a, d_skip, h0_f, h0_b):
    dt_f, dt_b = dt[:, :, 0], dt[:, :, 1]
    y_f, hf = ssd_scan(xbc, dt_f, jnp.swapaxes(dt_f, 1, 2), a[0], h0_f, reverse=False, name="ssd_scan_fwd")
    y, hb = ssd_scan(xbc, dt_b, jnp.swapaxes(dt_b, 1, 2), a[1], h0_b, reverse=True, y_prev=y_f, d_skip=d_skip,
                     name="ssd_scan_bwd")
    return y, hf, hb


def _moe(x2d, g, shift, scale, gate_mod, rows_per_batch, b, n, router_w, w1, w3, w2):
    d = x2d.shape[1]
    cap = EC_CAPACITY_FACTOR * n // N_EXPERTS
    logits, h = fused_matmul(x2d, _pad_cols(router_w, LANES), prologue="norm_mod", g=g, shift=shift, scale=scale,
                             rows_per_batch=rows_per_batch, precise=True, emit_h=BF16, name="moe_router")
    aff = jax.nn.softmax(logits[:, :N_EXPERTS].reshape(b, n, N_EXPERTS), axis=-1)
    gate, idx = lax.top_k(jnp.swapaxes(aff, 1, 2), cap)
    bi = jnp.arange(b)[:, None, None]
    h3 = h.reshape(b, n, d)
    xs = jnp.swapaxes(h3[bi, idx], 0, 1).reshape(N_EXPERTS, b * cap, d)
    gate_e = jnp.swapaxes(gate, 0, 1).reshape(N_EXPERTS, b * cap, 1)
    ys = expert_ffn(xs, w1, w3, w2, gate_e)
    ys = jnp.swapaxes(ys.reshape(N_EXPERTS, b, cap, d), 0, 1)
    moe = jnp.zeros((b, n, d), F32).at[bi, idx].add(ys)
    x3 = x2d.reshape(b, n, d)
    gm = gate_mod if gate_mod.shape[0] == b else jnp.broadcast_to(gate_mod, (b, 1, d))
    return (x3 + gm * moe).reshape(b * n, d)


def _dft_tables(l, ch):
    def cs(nn):
        idx = jnp.arange(nn)
        ang = ((idx[:, None] * idx[None, :]) % nn).astype(F32) * (2.0 * math.pi / nn)
        s = 1.0 / math.sqrt(nn)
        return jnp.cos(ang) * s, jnp.sin(ang) * s
    cl, sl = cs(l)
    cc, sc = cs(ch)
    return cl, sl, cc, sc


def _fourier(h2d, b, l, out_w, out_b, gate, res, rows_per_batch):
    d = h2d.shape[1]
    ch = d // FNET_GROUPS
    cl, sl, cc, sc = _dft_tables(l, ch)
    eye = jnp.eye(FNET_GROUPS, dtype=F32)
    w_ch = jnp.concatenate([jnp.kron(eye, cc), jnp.kron(eye, sc)], axis=1).astype(BF16)
    pq = fused_matmul(h2d, w_ch, out_dtype=BF16, name="fnet_channel_dft")
    pq = pq.reshape(b, l, 2, d)
    rhs = jnp.transpose(pq, (2, 1, 0, 3)).reshape(2 * l, b * d)
    lhs = jnp.concatenate([cl, -sl], axis=1).astype(BF16)
    mixed = fused_matmul(lhs, rhs, out_dtype=BF16, tk=2048, name="fnet_position_dft")
    mixed = jnp.transpose(mixed.reshape(l, b, d), (1, 0, 2)).reshape(b * l, d)
    return fused_matmul(mixed, out_w.astype(BF16), bias=out_b, gate=gate, res=res,
                        rows_per_batch=rows_per_batch, name="fnet_out_proj")


def kernel(x, c, ctx, c_ctx, mod_w, mod_b, norm1_g, norm2_g, ssd_in_w, ssd_conv_w, ssd_conv_b, ssd_dt_bias,
           ssd_a_log, ssd_d, ssd_norm_g, ssd_out_w, fnet_out_w, fnet_out_b, diff_qkv_w, diff_out_w, diff_lambda,
           diff_subln_g, router_w, moe_w1, moe_w3, moe_w2, final_g):
    b, l, d = x.shape
    lc = ctx.shape[1]
    d_inner = SSM_HEADS * SSM_HEAD_DIM

    cond = jnp.concatenate([c, c_ctx[None], jnp.zeros(((8 - (b + 1) % 8) % 8, d), F32)], axis=0)
    cond = jax.nn.silu(cond)
    xl = x.reshape(b * l, d)
    xc = ctx.reshape(b * lc, d)

    for i in range(DEPTH):
        need_ctx = i < DEPTH - 1
        kind, j = i % N_MIXERS, i // N_MIXERS
        mod = fused_matmul(cond, mod_w[i], bias=mod_b[i], tn=1024, name="modulation")
        m_l = [mod[:b, t * d:(t + 1) * d].reshape(b, 1, d) for t in range(6)]
        m_c = [mod[b:b + 1, t * d:(t + 1) * d].reshape(1, 1, d) for t in range(6)]

        if kind == 0:
            w_main = ssd_in_w[j][:, :d_inner + d_inner + 2 * SSM_GROUPS * SSM_STATE].astype(BF16)
            w_dt = _pad_cols(ssd_in_w[j][:, w_main.shape[1]:], LANES)
            a = -jnp.exp(ssd_a_log[j].astype(F32))
            h0 = jnp.zeros((b, SSM_GROUPS, SSM_STATE, GROUP_WIDTH), F32)
            z_c, xbc_c, dt_c = _ssd_project(xc, norm1_g[i], m_c[0], m_c[1], None, w_main, w_dt,
                                            ssd_conv_w[j], ssd_conv_b[j], ssd_dt_bias[j], b, lc)
            y_c, hf, hb = _ssd_bidirectional(xbc_c, dt_c, a, ssd_d[j], h0, h0)
            z_l, xbc_l, dt_l = _ssd_project(xl, norm1_g[i], m_l[0], m_l[1], l, w_main, w_dt,
                                            ssd_conv_w[j], ssd_conv_b[j], ssd_dt_bias[j], b, l)
            y_l, _, _ = _ssd_bidirectional(xbc_l, dt_l, a, ssd_d[j], hf, hb)
            w_out = ssd_out_w[j].astype(BF16)
            xl = fused_matmul(y_l.reshape(b * l, d_inner), w_out, prologue="gated_norm",
                              z=z_l.reshape(b * l, d_inner), g=ssd_norm_g[j], gate=m_l[2], res=xl,
                              rows_per_batch=l, name="ssd_out_proj")
            if need_ctx:
                xc = fused_matmul(y_c.reshape(b * lc, d_inner), w_out, prologue="gated_norm",
                                  z=z_c.reshape(b * lc, d_inner), g=ssd_norm_g[j], gate=m_c[2], res=xc,
                                  name="ssd_out_proj_ctx")
        elif kind == 1:
            ident = jnp.eye(d, dtype=BF16)
            _, h_l = fused_matmul(xl, ident[:, :LANES], prologue="norm_mod", g=norm1_g[i], shift=m_l[0],
                                  scale=m_l[1], rows_per_batch=l, emit_h=BF16, name="fnet_norm")
            xl = _fourier(h_l, b, l, fnet_out_w[j], fnet_out_b[j], m_l[2], xl, l)
            if need_ctx:
                _, h_c = fused_matmul(xc, ident[:, :LANES], prologue="norm_mod", g=norm1_g[i], shift=m_c[0],
                                      scale=m_c[1], emit_h=BF16, name="fnet_norm_ctx")
                xc = _fourier(h_c, b, lc, fnet_out_w[j], fnet_out_b[j], m_c[2], xc, None)
        else:
            lambda_init = 0.8 - 0.6 * math.exp(-0.3 * i)
            lp = diff_lambda[j].astype(F32)
            lam = jnp.exp(jnp.sum(lp[0] * lp[1])) - jnp.exp(jnp.sum(lp[2] * lp[3])) + lambda_init
            w_qkv = diff_qkv_w[j].astype(BF16)
            w_out = diff_out_w[j].astype(BF16)
            qkv_c = fused_matmul(xc, w_qkv, prologue="norm_mod", g=norm1_g[i], shift=m_c[0], scale=m_c[1],
                                 name="diff_qkv_ctx").reshape(b, lc, 3 * d)
            qkv_l = fused_matmul(xl, w_qkv, prologue="norm_mod", g=norm1_g[i], shift=m_l[0], scale=m_l[1],
                                 rows_per_batch=l, name="diff_qkv").reshape(b, l, 3 * d)
            o_l = diff_attention(qkv_l, qkv_l, qkv_c, lam, diff_subln_g[j], 1.0 - lambda_init, _rope_tables(l),
                                 tq=256, tk=768, name="diff_attention")
            xl = fused_matmul(o_l.reshape(b * l, d), w_out, gate=m_l[2], res=xl, rows_per_batch=l,
                              name="diff_out_proj")
            if need_ctx:
                o_c = diff_attention(qkv_c, None, qkv_c, lam, diff_subln_g[j], 1.0 - lambda_init, None,
                                     tq=256, tk=256, name="diff_attention_ctx")
                xc = fused_matmul(o_c.reshape(b * lc, d), w_out, gate=m_c[2], res=xc, name="diff_out_proj_ctx")

        w1, w3, w2 = moe_w1[i].astype(BF16), moe_w3[i].astype(BF16), moe_w2[i].astype(BF16)
        xl = _moe(xl, norm2_g[i], m_l[3], m_l[4], m_l[5], l, b, l, router_w[i], w1, w3, w2)
        if need_ctx:
            xc = _moe(xc, norm2_g[i], m_c[3], m_c[4], m_c[5], None, b, lc, router_w[i], w1, w3, w2)

    zero = jnp.zeros((1, 1, d), F32)
    _, out = fused_matmul(xl, jnp.eye(d, dtype=BF16)[:, :LANES], prologue="norm_mod", g=final_g, shift=zero,
                          scale=zero, emit_h=F32, name="final_norm")
    return out.reshape(b, l, d)
```

```python
import functools
import math

import jax
import jax.numpy as jnp
from jax import lax
from jax.experimental import pallas as pl
from jax.experimental.pallas import tpu as pltpu

F32 = jnp.float32
BF16 = jnp.bfloat16

EPS = 1e-6
DEPTH = 4
N_MIXERS = 3
GRID_W = 64
ROPE_THETA = 10000.0

SSM_HEAD_DIM = 64
SSM_HEADS = 32
SSM_GROUPS = 8
SSM_STATE = 128
SSM_CHUNK = 128
SSM_CONV = 5
HEADS_PER_GROUP = SSM_HEADS // SSM_GROUPS
GROUP_WIDTH = HEADS_PER_GROUP * SSM_HEAD_DIM

FNET_GROUPS = 4

DIFF_HEADS = 8
DIFF_HEAD_DIM = 64
DIFF_V_DIM = 2 * DIFF_HEAD_DIM
DIFF_SCALE = DIFF_HEAD_DIM ** -0.5
ROPE_PAIRS = DIFF_HEAD_DIM // 4

N_EXPERTS = 16
EC_CAPACITY_FACTOR = 2

LANES = 128
VMEM_LIMIT_BYTES = 56 * 1024 * 1024


def _compiler_params(semantics):
    return pltpu.CompilerParams(dimension_semantics=semantics, vmem_limit_bytes=VMEM_LIMIT_BYTES)


def _split_bf16(x, parts):
    out = []
    r = x
    for _ in range(parts):
        p = r.astype(BF16)
        out.append(p)
        r = r - p.astype(F32)
    return out


def _dot(a, b):
    return jnp.dot(a, b, preferred_element_type=F32)


def _dot_nt(a, b):
    return lax.dot_general(a, b, (((1,), (1,)), ((), ())), preferred_element_type=F32)


def _dot_exact_rhs(x, m_bf16, parts=3, nt=False):
    f = _dot_nt if nt else _dot
    acc = None
    for p in _split_bf16(x, parts):
        t = f(p, m_bf16)
        acc = t if acc is None else acc + t
    return acc


def _dot_exact_lhs(m_bf16, x, parts=3):
    acc = None
    for p in _split_bf16(x, parts):
        t = _dot(m_bf16, p)
        acc = t if acc is None else acc + t
    return acc


def _mm_kernel(*refs, prologue, precise, has_bias, has_gate_res, emit_h, nk):
    it = iter(refs)
    x_ref = next(it)
    z_ref = g_ref = sh_ref = sc_ref = None
    if prologue == "norm_mod":
        g_ref, sh_ref, sc_ref = next(it), next(it), next(it)
    elif prologue == "gated_norm":
        z_ref, g_ref = next(it), next(it)
    w_ref = next(it)
    b_ref = next(it) if has_bias else None
    gate_ref = res_ref = None
    if has_gate_res:
        gate_ref, res_ref = next(it), next(it)
    o_ref = next(it)
    hout_ref = next(it) if emit_h else None
    h_ref = next(it) if prologue else None
    acc_ref = next(it) if nk > 1 else None

    j = pl.program_id(1)
    k = pl.program_id(2)

    if prologue:
        @pl.when(j == 0)
        def _():
            x = x_ref[...].astype(F32)
            if prologue == "gated_norm":
                z = z_ref[...].astype(F32)
                x = x * (z * jax.nn.sigmoid(z))
            ms = jnp.mean(x * x, axis=-1, keepdims=True)
            y = x * lax.rsqrt(ms + EPS) * g_ref[...]
            if prologue == "norm_mod":
                y = y * (1.0 + sc_ref[...]) + sh_ref[...]
            h_ref[...] = y.astype(h_ref.dtype)
            if emit_h:
                hout_ref[...] = y.astype(hout_ref.dtype)
        lhs = h_ref[...]
    else:
        lhs = x_ref[...]

    w = w_ref[...]
    if precise:
        l_hi, l_lo = _split_bf16(lhs.astype(F32), 2)
        w_hi, w_lo = _split_bf16(w.astype(F32), 2)
        part = _dot(l_hi, w_hi) + (_dot(l_lo, w_hi) + _dot(l_hi, w_lo))
    else:
        part = _dot(lhs.astype(BF16), w.astype(BF16))

    def finish(acc):
        if has_bias:
            acc = acc + b_ref[...]
        if has_gate_res:
            acc = res_ref[...].astype(F32) + gate_ref[...] * acc
        o_ref[...] = acc.astype(o_ref.dtype)

    if nk == 1:
        finish(part)
    else:
        @pl.when(k == 0)
        def _():
            acc_ref[...] = part

        @pl.when(k > 0)
        def _():
            acc_ref[...] += part

        @pl.when(k == nk - 1)
        def _():
            finish(acc_ref[...])


def fused_matmul(x, w, *, prologue=None, g=None, shift=None, scale=None, z=None, bias=None,
                 gate=None, res=None, rows_per_batch=None, precise=False, out_dtype=F32,
                 emit_h=None, tm=512, tn=1024, tk=None, name="fused_matmul"):
    m, kdim = x.shape
    n = w.shape[1]
    rows_per_batch = m if rows_per_batch is None else rows_per_batch
    tm = min(tm, m, rows_per_batch)
    tn = min(tn, n)
    tk = kdim if tk is None else min(tk, kdim)
    assert m % tm == 0 and n % tn == 0 and kdim % tk == 0, (x.shape, w.shape, tm, tn, tk)
    nk = kdim // tk
    assert not (prologue and nk > 1)
    assert rows_per_batch % tm == 0
    tiles_per_batch = rows_per_batch // tm

    def bidx(i):
        return i // tiles_per_batch

    args = [x]
    in_specs = [pl.BlockSpec((tm, tk), lambda i, j, k: (i, k))]
    if prologue == "norm_mod":
        args += [g.reshape(1, kdim), shift, scale]
        in_specs += [pl.BlockSpec((1, kdim), lambda i, j, k: (0, 0)),
                     pl.BlockSpec((None, 1, kdim), lambda i, j, k: (bidx(i), 0, 0)),
                     pl.BlockSpec((None, 1, kdim), lambda i, j, k: (bidx(i), 0, 0))]
    elif prologue == "gated_norm":
        args += [z, g.reshape(1, kdim)]
        in_specs += [pl.BlockSpec((tm, kdim), lambda i, j, k: (i, 0)),
                     pl.BlockSpec((1, kdim), lambda i, j, k: (0, 0))]
    args.append(w)
    in_specs.append(pl.BlockSpec((tk, tn), lambda i, j, k: (k, j)))
    if bias is not None:
        args.append(bias.reshape(1, n))
        in_specs.append(pl.BlockSpec((1, tn), lambda i, j, k: (0, j)))
    if gate is not None:
        args += [gate, res]
        in_specs += [pl.BlockSpec((None, 1, tn), lambda i, j, k: (bidx(i), 0, j)),
                     pl.BlockSpec((tm, tn), lambda i, j, k: (i, j))]

    out_shape = [jax.ShapeDtypeStruct((m, n), out_dtype)]
    out_specs = [pl.BlockSpec((tm, tn), lambda i, j, k: (i, j))]
    if emit_h is not None:
        out_shape.append(jax.ShapeDtypeStruct((m, kdim), emit_h))
        out_specs.append(pl.BlockSpec((tm, kdim), lambda i, j, k: (i, 0)))

    scratch = []
    if prologue:
        scratch.append(pltpu.VMEM((tm, kdim), F32 if precise else BF16))
    if nk > 1:
        scratch.append(pltpu.VMEM((tm, tn), F32))

    kern = functools.partial(_mm_kernel, prologue=prologue, precise=precise, has_bias=bias is not None,
                             has_gate_res=gate is not None, emit_h=emit_h is not None, nk=nk)
    outs = pl.pallas_call(
        kern, grid=(m // tm, n // tn, nk), in_specs=in_specs, out_specs=out_specs, out_shape=out_shape,
        scratch_shapes=scratch, name=name,
        compiler_params=_compiler_params(("parallel", "arbitrary", "arbitrary")))(*args)
    return outs if emit_h is not None else outs[0]


def _ssd_kernel(*refs, reverse, combine, q):
    it = iter(refs)
    xs_ref, bm_ref, cm_ref, dt_ref, dtt_ref, a_ref, at_ref, h0_ref = (next(it) for _ in range(8))
    yprev_ref = dskip_ref = None
    if combine:
        yprev_ref, dskip_ref = next(it), next(it)
    y_ref, hfin_ref, state_ref = next(it), next(it), next(it)

    c = pl.program_id(1)

    @pl.when(c == 0)
    def _():
        state_ref[...] = h0_ref[...]

    ii = lax.broadcasted_iota(jnp.int32, (q, q), 0)
    jj = lax.broadcasted_iota(jnp.int32, (q, q), 1)
    mask = (ii <= jj) if reverse else (ii >= jj)
    tri = jnp.where(mask, 1.0, 0.0).astype(BF16)

    dt = dt_ref[...]
    dta = dt * a_ref[...]
    dta_t = dtt_ref[...] * at_ref[...]
    a_cs = _dot_exact_lhs(tri, dta)
    a_cs_t = _dot_exact_rhs(dta_t, tri, nt=True)
    edge = 0 if reverse else q - 1
    tot = a_cs[edge:edge + 1, :]
    to_end = jnp.exp(tot - a_cs)
    into = jnp.exp(a_cs)
    chunk_decay = jnp.exp(tot)

    hh = lax.broadcasted_iota(jnp.int32, (SSM_HEADS, SSM_HEADS * SSM_HEAD_DIM), 0)
    cc = lax.broadcasted_iota(jnp.int32, (SSM_HEADS, SSM_HEADS * SSM_HEAD_DIM), 1)
    expand = jnp.where(cc // SSM_HEAD_DIM == hh, 1.0, 0.0).astype(BF16)
    dt_x = _dot_exact_rhs(dt, expand, parts=2)
    to_end_x = _dot_exact_rhs(to_end, expand, parts=2)
    into_x = _dot_exact_rhs(into, expand, parts=2)
    decay_x = _dot_exact_rhs(jnp.broadcast_to(chunk_decay, (8, SSM_HEADS)), expand, parts=3)[0:1, :]

    for g in range(SSM_GROUPS):
        c0 = g * GROUP_WIDTH
        xs_g = xs_ref[:, c0:c0 + GROUP_WIDTH]
        bm_g = bm_ref[:, g * SSM_STATE:(g + 1) * SSM_STATE]
        cm_g = cm_ref[:, g * SSM_STATE:(g + 1) * SSM_STATE].astype(BF16)
        xdt_g = xs_g * dt_x[:, c0:c0 + GROUP_WIDTH]
        xdt_b = xdt_g.astype(BF16)
        cb = _dot_nt(cm_g, bm_g.astype(BF16))
        st = state_ref[g]
        y_g = _dot(cm_g, st.astype(BF16)) * into_x[:, c0:c0 + GROUP_WIDTH]
        if combine:
            y_g = y_g + yprev_ref[:, c0:c0 + GROUP_WIDTH] + dskip_ref[:, c0:c0 + GROUP_WIDTH] * xs_g
        y_ref[:, c0:c0 + GROUP_WIDTH] = y_g
        for r in range(HEADS_PER_GROUP):
            h = g * HEADS_PER_GROUP + r
            seg = a_cs[:, h:h + 1] - a_cs_t[h:h + 1, :]
            lmat = (jnp.exp(jnp.where(mask, seg, -1e30)) * cb).astype(BF16)
            lo = c0 + r * SSM_HEAD_DIM
            y_ref[:, lo:lo + SSM_HEAD_DIM] += _dot(lmat, xdt_b[:, r * SSM_HEAD_DIM:(r + 1) * SSM_HEAD_DIM])
        xdtw = (xdt_g * to_end_x[:, c0:c0 + GROUP_WIDTH]).astype(BF16)
        new = _dot(bm_g.T.astype(BF16), xdtw)
        state_ref[g] = st * decay_x[:, c0:c0 + GROUP_WIDTH] + new

    @pl.when(c == pl.num_programs(1) - 1)
    def _():
        hfin_ref[...] = state_ref[...]


def ssd_scan(xbc, dt, dt_t, a, h0, *, reverse, y_prev=None, d_skip=None, name="ssd_scan"):
    b, l, _ = xbc.shape
    q = SSM_CHUNK
    nc = l // q
    d_inner = SSM_HEADS * SSM_HEAD_DIM
    nbc = SSM_GROUPS * SSM_STATE
    combine = y_prev is not None

    def cidx(c):
        return (nc - 1 - c) if reverse else c

    xs_blocks = d_inner // d_inner
    in_specs = [
        pl.BlockSpec((None, q, d_inner), lambda i, c: (i, cidx(c), 0)),
        pl.BlockSpec((None, q, nbc), lambda i, c: (i, cidx(c), d_inner // nbc)),
        pl.BlockSpec((None, q, nbc), lambda i, c: (i, cidx(c), d_inner // nbc + 1)),
        pl.BlockSpec((None, q, SSM_HEADS), lambda i, c: (i, cidx(c), 0)),
        pl.BlockSpec((None, SSM_HEADS, q), lambda i, c: (i, 0, cidx(c))),
        pl.BlockSpec((1, SSM_HEADS), lambda i, c: (0, 0)),
        pl.BlockSpec((SSM_HEADS, 1), lambda i, c: (0, 0)),
        pl.BlockSpec((None, SSM_GROUPS, SSM_STATE, GROUP_WIDTH), lambda i, c: (i, 0, 0, 0)),
    ]
    del xs_blocks
    args = [xbc, xbc, xbc, dt, dt_t, a.reshape(1, SSM_HEADS), a.reshape(SSM_HEADS, 1), h0]
    if combine:
        in_specs += [pl.BlockSpec((None, q, d_inner), lambda i, c: (i, cidx(c), 0)),
                     pl.BlockSpec((1, d_inner), lambda i, c: (0, 0))]
        args += [y_prev, jnp.repeat(d_skip, SSM_HEAD_DIM).reshape(1, d_inner)]
    out_shape = [jax.ShapeDtypeStruct((b, l, d_inner), F32),
                 jax.ShapeDtypeStruct((b, SSM_GROUPS, SSM_STATE, GROUP_WIDTH), F32)]
    out_specs = [pl.BlockSpec((None, q, d_inner), lambda i, c: (i, cidx(c), 0)),
                 pl.BlockSpec((None, SSM_GROUPS, SSM_STATE, GROUP_WIDTH), lambda i, c: (i, 0, 0, 0))]
    kern = functools.partial(_ssd_kernel, reverse=reverse, combine=combine, q=q)
    return pl.pallas_call(
        kern, grid=(b, nc), in_specs=in_specs, out_specs=out_specs, out_shape=out_shape,
        scratch_shapes=[pltpu.VMEM((SSM_GROUPS, SSM_STATE, GROUP_WIDTH), F32)], name=name,
        compiler_params=_compiler_params(("parallel", "arbitrary")))(*args)


def _rope(x, cos, sin_up, sin_dn):
    return (x * cos + pltpu.roll(x, ROPE_PAIRS, 1) * sin_up
            + pltpu.roll(x, LANES - ROPE_PAIRS, 1) * sin_dn)


LOG2E = 1.4426950408889634


def _attn_kernel(*refs, n_lat, n_ctx, tk, rope, out_scale):
    it = iter(refs)
    q_ref = next(it)
    kl_ref = vl_ref = None
    if n_lat:
        kl_ref, vl_ref = next(it), next(it)
    kc_ref, vc_ref = next(it), next(it)
    if rope:
        cq_ref, suq_ref, sdq_ref, ck_ref, suk_ref, sdk_ref = (next(it) for _ in range(6))
    lam_ref, g_ref, o_ref, k_s, vt_s, m_s, l_s, acc_s, s_s = (next(it) for _ in range(9))

    qi = pl.program_id(2)
    tq = q_ref.shape[0]
    n_tiles = (n_lat + n_ctx) // tk

    @pl.when(qi == 0)
    def _():
        if n_lat:
            k = kl_ref[...]
            if rope:
                k = _rope(k, ck_ref[...], suk_ref[...], sdk_ref[...])
            k_s[0:n_lat, :] = k.astype(BF16)
        k_s[n_lat:n_lat + n_ctx, :] = kc_ref[...].astype(BF16)
        for t in range(n_tiles):
            lo, hi = t * tk, (t + 1) * tk
            pieces = []
            if lo < n_lat:
                pieces.append(vl_ref[lo:min(hi, n_lat), :])
            if hi > n_lat:
                pieces.append(vc_ref[max(lo, n_lat) - n_lat:hi - n_lat, :])
            v = pieces[0] if len(pieces) == 1 else jnp.concatenate(pieces, axis=0)
            vt_s[t] = v.T.astype(BF16)

    q = q_ref[...] * (DIFF_SCALE * LOG2E)
    if rope:
        q = _rope(q, cq_ref[...], suq_ref[...], sdq_ref[...])
    qt = q.T
    row = lax.broadcasted_iota(jnp.int32, qt.shape, 0)
    wq = jnp.concatenate([jnp.where(row < DIFF_HEAD_DIM, qt, 0.0),
                          jnp.where(row >= DIFF_HEAD_DIM, qt, 0.0)], axis=1).astype(BF16)

    m_s[...] = jnp.full(m_s.shape, -1e30, F32)
    l_s[...] = jnp.zeros(l_s.shape, F32)
    acc_s[...] = jnp.zeros(acc_s.shape, F32)

    def scores(t):
        off = pl.multiple_of(t * tk, tk)
        return _dot(k_s[pl.ds(off, tk), :], wq)

    def absorb(buf, t):
        s = s_s[buf]
        m_old = m_s[...]
        m_new = jnp.maximum(m_old, jnp.max(s, axis=0, keepdims=True))
        alpha = jnp.exp2(m_old - m_new)
        p = jnp.exp2(s - m_new)
        l_s[...] = alpha * l_s[...] + jnp.sum(p, axis=0, keepdims=True)
        acc_s[...] = alpha * acc_s[...] + _dot(vt_s[t], p.astype(BF16))
        m_s[...] = m_new

    s_s[0] = scores(0)

    def body(i, carry):
        t0 = 2 * i
        s_s[1] = scores(t0 + 1)
        absorb(0, t0)
        s_s[0] = scores(t0 + 2)
        absorb(1, t0 + 1)
        return carry

    lax.fori_loop(0, n_tiles // 2 - 1, body, 0)
    s_s[1] = scores(n_tiles - 1)
    absorb(0, n_tiles - 2)
    absorb(1, n_tiles - 1)

    o_t = acc_s[...] / l_s[...]
    o = (o_t[:, :tq] - lam_ref[0:1, 0:1] * o_t[:, tq:]).T
    ms = jnp.mean(o * o, axis=-1, keepdims=True)
    o_ref[...] = o * lax.rsqrt(ms + EPS) * (g_ref[...] * out_scale)


ATTN_TILES = dict(tq=256, tk=1408)
ATTN_TILES_CTX = dict(tq=256, tk=128)


def diff_attention(qkv_q, qkv_lat, qkv_ctx, lam, subln_g, out_scale, rope_tabs, *, tq, tk, name):
    b, lq, d3 = qkv_q.shape
    d = d3 // 3
    nh = d // DIFF_V_DIM
    n_lat = 0 if qkv_lat is None else qkv_lat.shape[1]
    n_ctx = qkv_ctx.shape[1]
    rope = rope_tabs is not None
    tq = min(tq, lq)
    tk = min(tk, n_lat + n_ctx)
    assert lq % tq == 0 and (n_lat + n_ctx) % (2 * tk) == 0

    args = [qkv_q]
    in_specs = [pl.BlockSpec((None, tq, DIFF_V_DIM), lambda i, h, t: (i, t, h))]
    if n_lat:
        args += [qkv_lat, qkv_lat]
        in_specs += [pl.BlockSpec((None, n_lat, DIFF_V_DIM), lambda i, h, t: (i, 0, nh + h)),
                     pl.BlockSpec((None, n_lat, DIFF_V_DIM), lambda i, h, t: (i, 0, 2 * nh + h))]
    args += [qkv_ctx, qkv_ctx]
    in_specs += [pl.BlockSpec((None, n_ctx, DIFF_V_DIM), lambda i, h, t: (i, 0, nh + h)),
                 pl.BlockSpec((None, n_ctx, DIFF_V_DIM), lambda i, h, t: (i, 0, 2 * nh + h))]
    if rope:
        args += list(rope_tabs) + list(rope_tabs)
        in_specs += [pl.BlockSpec((tq, LANES), lambda i, h, t: (t, 0))] * 3
        in_specs += [pl.BlockSpec((n_lat, LANES), lambda i, h, t: (0, 0), pipeline_mode=pl.Buffered(1))] * 3
    args += [jnp.broadcast_to(lam.astype(F32).reshape(1, 1), (8, LANES)), subln_g.reshape(1, DIFF_V_DIM)]
    in_specs += [pl.BlockSpec((8, LANES), lambda i, h, t: (0, 0)),
                 pl.BlockSpec((1, DIFF_V_DIM), lambda i, h, t: (0, 0))]

    kern = functools.partial(_attn_kernel, n_lat=n_lat, n_ctx=n_ctx, tk=tk, rope=rope, out_scale=out_scale)
    return pl.pallas_call(
        kern, grid=(b, nh, lq // tq), in_specs=in_specs,
        out_specs=pl.BlockSpec((None, tq, DIFF_V_DIM), lambda i, h, t: (i, t, h)),
        out_shape=jax.ShapeDtypeStruct((b, lq, d), F32),
        scratch_shapes=[pltpu.VMEM((n_lat + n_ctx, DIFF_V_DIM), BF16),
                        pltpu.VMEM(((n_lat + n_ctx) // tk, DIFF_V_DIM, tk), BF16),
                        pltpu.VMEM((1, 2 * tq), F32), pltpu.VMEM((1, 2 * tq), F32),
                        pltpu.VMEM((DIFF_V_DIM, 2 * tq), F32),
                        pltpu.VMEM((2, tk, 2 * tq), F32)],
        name=name,
        compiler_params=_compiler_params(("parallel", "parallel", "arbitrary")))(*args)


def _rope_tables(l):
    rows = l // GRID_W
    row = jnp.repeat(jnp.arange(rows), GRID_W).astype(F32)
    col = jnp.tile(jnp.arange(GRID_W), rows).astype(F32)
    inv = ROPE_THETA ** (-jnp.arange(ROPE_PAIRS, dtype=F32) / ROPE_PAIRS)
    lane = jnp.arange(LANES)
    within = lane % DIFF_HEAD_DIM
    axis = within // (2 * ROPE_PAIRS)
    second = (within // ROPE_PAIRS) % 2
    pos = jnp.where(axis[None, :] == 0, row[:, None], col[:, None])
    ang = pos * inv[within % ROPE_PAIRS][None, :]
    cos, sin = jnp.cos(ang), jnp.sin(ang)
    sin_up = jnp.where(second[None, :] == 1, sin, 0.0)
    sin_dn = jnp.where(second[None, :] == 0, -sin, 0.0)
    return cos, sin_up, sin_dn


FFN_F_SPLIT = 2


def _ffn_kernel(x_ref, w1_ref, w3_ref, w2_ref, gate_ref, o_ref, w1_s, w3_s, w2_s):
    @pl.when(pl.program_id(1) == 0)
    def _():
        w1_s[...] = w1_ref[...].astype(BF16)
        w3_s[...] = w3_ref[...].astype(BF16)
        w2_s[...] = w2_ref[...].astype(BF16)

    x = x_ref[...]
    fs = w1_s.shape[1] // FFN_F_SPLIT
    acc = None
    for s in range(FFN_F_SPLIT):
        h1 = _dot(x, w1_s[:, s * fs:(s + 1) * fs])
        h3 = _dot(x, w3_s[:, s * fs:(s + 1) * fs])
        hid = (h1 * jax.nn.sigmoid(h1) * h3).astype(BF16)
        part = _dot(hid, w2_s[s * fs:(s + 1) * fs, :])
        acc = part if acc is None else acc + part
    o_ref[...] = acc * gate_ref[...]


def expert_ffn(xs, w1, w3, w2, gate, *, tm=512, name="expert_ffn"):
    e, r, d = xs.shape
    f = w1.shape[2]
    tm = min(tm, r)
    assert r % tm == 0 and f % FFN_F_SPLIT == 0
    once = pl.Buffered(1)
    return pl.pallas_call(
        _ffn_kernel, grid=(e, r // tm),
        in_specs=[pl.BlockSpec((None, tm, d), lambda i, t: (i, t, 0)),
                  pl.BlockSpec((None, d, f), lambda i, t: (i, 0, 0), pipeline_mode=once),
                  pl.BlockSpec((None, d, f), lambda i, t: (i, 0, 0), pipeline_mode=once),
                  pl.BlockSpec((None, f, d), lambda i, t: (i, 0, 0), pipeline_mode=once),
                  pl.BlockSpec((None, tm, 1), lambda i, t: (i, t, 0))],
        out_specs=pl.BlockSpec((None, tm, d), lambda i, t: (i, t, 0)),
        out_shape=jax.ShapeDtypeStruct((e, r, d), F32),
        scratch_shapes=[pltpu.VMEM((d, f), BF16), pltpu.VMEM((d, f), BF16), pltpu.VMEM((f, d), BF16)],
        name=name,
        compiler_params=_compiler_params(("arbitrary", "arbitrary")))(xs, w1, w3, w2, gate)


def _pad_cols(w, n):
    return jnp.pad(w, ((0, 0), (0, n - w.shape[1])))


def _ssd_project(x2d, g, shift, scale, rows_per_batch, w_main, w_dt, conv_w, conv_b, dt_bias, b, l):
    d_inner = SSM_HEADS * SSM_HEAD_DIM
    main = fused_matmul(x2d, w_main, prologue="norm_mod", g=g, shift=shift, scale=scale,
                        rows_per_batch=rows_per_batch, name="ssd_in_proj")
    dt_raw = fused_matmul(x2d, w_dt, prologue="norm_mod", g=g, shift=shift, scale=scale,
                          rows_per_batch=rows_per_batch, precise=True, name="ssd_dt_proj")
    main = main.reshape(b, l, -1)
    z = main[..., :d_inner]
    xbc = main[..., d_inner:]
    k = conv_w.shape[0]
    conv = lax.conv_general_dilated(xbc, conv_w[:, None, :], window_strides=(1,), padding=[(k // 2, k // 2)],
                                    dimension_numbers=("NWC", "WIO", "NWC"), feature_group_count=xbc.shape[-1])
    xbc = jax.nn.silu(conv + conv_b)
    dt = jax.nn.softplus(dt_raw[:, :2 * SSM_HEADS].reshape(b, l, 2, SSM_HEADS) + dt_bias)
    return z, xbc, dt


def _ssd_bidirectional(xbc, dt, a, d_skip, h0_f, h0_b):
    dt_f, dt_b = dt[:, :, 0], dt[:, :, 1]
    y_f, hf = ssd_scan(xbc, dt_f, jnp.swapaxes(dt_f, 1, 2), a[0], h0_f, reverse=False, name="ssd_scan_fwd")
    y, hb = ssd_scan(xbc, dt_b, jnp.swapaxes(dt_b, 1, 2), a[1], h0_b, reverse=True, y_prev=y_f, d_skip=d_skip,
                     name="ssd_scan_bwd")
    return y, hf, hb


def _moe(x2d, g, shift, scale, gate_mod, rows_per_batch, b, n, router_w, w1, w3, w2):
    d = x2d.shape[1]
    cap = EC_CAPACITY_FACTOR * n // N_EXPERTS
    logits, h = fused_matmul(x2d, _pad_cols(router_w, LANES), prologue="norm_mod", g=g, shift=shift, scale=scale,
                             rows_per_batch=rows_per_batch, precise=True, emit_h=BF16, name="moe_router")
    aff = jax.nn.softmax(logits[:, :N_EXPERTS].reshape(b, n, N_EXPERTS), axis=-1)
    gate, idx = lax.top_k(jnp.swapaxes(aff, 1, 2), cap)
    bi = jnp.arange(b)[:, None, None]
    h3 = h.reshape(b, n, d)
    xs = jnp.swapaxes(h3[bi, idx], 0, 1).reshape(N_EXPERTS, b * cap, d)
    gate_e = jnp.swapaxes(gate, 0, 1).reshape(N_EXPERTS, b * cap, 1)
    ys = expert_ffn(xs, w1, w3, w2, gate_e)
    ys = jnp.swapaxes(ys.reshape(N_EXPERTS, b, cap, d), 0, 1)
    moe = jnp.zeros((b, n, d), F32).at[bi, idx].add(ys)
    x3 = x2d.reshape(b, n, d)
    gm = gate_mod if gate_mod.shape[0] == b else jnp.broadcast_to(gate_mod, (b, 1, d))
    return (x3 + gm * moe).reshape(b * n, d)


def _dft_tables(l, ch):
    def cs(nn):
        idx = jnp.arange(nn)
        ang = ((idx[:, None] * idx[None, :]) % nn).astype(F32) * (2.0 * math.pi / nn)
        s = 1.0 / math.sqrt(nn)
        return jnp.cos(ang) * s, jnp.sin(ang) * s
    cl, sl = cs(l)
    cc, sc = cs(ch)
    return cl, sl, cc, sc


def _fourier(h2d, b, l, out_w, out_b, gate, res, rows_per_batch):
    d = h2d.shape[1]
    ch = d // FNET_GROUPS
    cl, sl, cc, sc = _dft_tables(l, ch)
    eye = jnp.eye(FNET_GROUPS, dtype=F32)
    w_ch = jnp.concatenate([jnp.kron(eye, cc), jnp.kron(eye, sc)], axis=1).astype(BF16)
    pq = fused_matmul(h2d, w_ch, out_dtype=BF16, name="fnet_channel_dft")
    pq = pq.reshape(b, l, 2, d)
    rhs = jnp.transpose(pq, (2, 1, 0, 3)).reshape(2 * l, b * d)
    lhs = jnp.concatenate([cl, -sl], axis=1).astype(BF16)
    mixed = fused_matmul(lhs, rhs, out_dtype=BF16, tk=2048, name="fnet_position_dft")
    mixed = jnp.transpose(mixed.reshape(l, b, d), (1, 0, 2)).reshape(b * l, d)
    return fused_matmul(mixed, out_w.astype(BF16), bias=out_b, gate=gate, res=res,
                        rows_per_batch=rows_per_batch, name="fnet_out_proj")


def kernel(x, c, ctx, c_ctx, mod_w, mod_b, norm1_g, norm2_g, ssd_in_w, ssd_conv_w, ssd_conv_b, ssd_dt_bias,
           ssd_a_log, ssd_d, ssd_norm_g, ssd_out_w, fnet_out_w, fnet_out_b, diff_qkv_w, diff_out_w, diff_lambda,
           diff_subln_g, router_w, moe_w1, moe_w3, moe_w2, final_g):
    b, l, d = x.shape
    lc = ctx.shape[1]
    d_inner = SSM_HEADS * SSM_HEAD_DIM

    cond = jnp.concatenate([c, c_ctx[None], jnp.zeros(((8 - (b + 1) % 8) % 8, d), F32)], axis=0)
    cond = jax.nn.silu(cond)
    xl = x.reshape(b * l, d)
    xc = ctx.reshape(b * lc, d)

    for i in range(DEPTH):
        need_ctx = i < DEPTH - 1
        kind, j = i % N_MIXERS, i // N_MIXERS
        mod = fused_matmul(cond, mod_w[i], bias=mod_b[i], tn=1024, name="modulation")
        m_l = [mod[:b, t * d:(t + 1) * d].reshape(b, 1, d) for t in range(6)]
        m_c = [mod[b:b + 1, t * d:(t + 1) * d].reshape(1, 1, d) for t in range(6)]

        if kind == 0:
            w_main = ssd_in_w[j][:, :d_inner + d_inner + 2 * SSM_GROUPS * SSM_STATE].astype(BF16)
            w_dt = _pad_cols(ssd_in_w[j][:, w_main.shape[1]:], LANES)
            a = -jnp.exp(ssd_a_log[j].astype(F32))
            h0 = jnp.zeros((b, SSM_GROUPS, SSM_STATE, GROUP_WIDTH), F32)
            z_c, xbc_c, dt_c = _ssd_project(xc, norm1_g[i], m_c[0], m_c[1], None, w_main, w_dt,
                                            ssd_conv_w[j], ssd_conv_b[j], ssd_dt_bias[j], b, lc)
            y_c, hf, hb = _ssd_bidirectional(xbc_c, dt_c, a, ssd_d[j], h0, h0)
            z_l, xbc_l, dt_l = _ssd_project(xl, norm1_g[i], m_l[0], m_l[1], l, w_main, w_dt,
                                            ssd_conv_w[j], ssd_conv_b[j], ssd_dt_bias[j], b, l)
            y_l, _, _ = _ssd_bidirectional(xbc_l, dt_l, a, ssd_d[j], hf, hb)
            w_out = ssd_out_w[j].astype(BF16)
            xl = fused_matmul(y_l.reshape(b * l, d_inner), w_out, prologue="gated_norm",
                              z=z_l.reshape(b * l, d_inner), g=ssd_norm_g[j], gate=m_l[2], res=xl,
                              rows_per_batch=l, name="ssd_out_proj")
            if need_ctx:
                xc = fused_matmul(y_c.reshape(b * lc, d_inner), w_out, prologue="gated_norm",
                                  z=z_c.reshape(b * lc, d_inner), g=ssd_norm_g[j], gate=m_c[2], res=xc,
                                  name="ssd_out_proj_ctx")
        elif kind == 1:
            ident = jnp.eye(d, dtype=BF16)
            _, h_l = fused_matmul(xl, ident[:, :LANES], prologue="norm_mod", g=norm1_g[i], shift=m_l[0],
                                  scale=m_l[1], rows_per_batch=l, emit_h=BF16, name="fnet_norm")
            xl = _fourier(h_l, b, l, fnet_out_w[j], fnet_out_b[j], m_l[2], xl, l)
            if need_ctx:
                _, h_c = fused_matmul(xc, ident[:, :LANES], prologue="norm_mod", g=norm1_g[i], shift=m_c[0],
                                      scale=m_c[1], emit_h=BF16, name="fnet_norm_ctx")
                xc = _fourier(h_c, b, lc, fnet_out_w[j], fnet_out_b[j], m_c[2], xc, None)
        else:
            lambda_init = 0.8 - 0.6 * math.exp(-0.3 * i)
            lp = diff_lambda[j].astype(F32)
            lam = jnp.exp(jnp.sum(lp[0] * lp[1])) - jnp.exp(jnp.sum(lp[2] * lp[3])) + lambda_init
            w_qkv = diff_qkv_w[j].astype(BF16)
            w_out = diff_out_w[j].astype(BF16)
            qkv_c = fused_matmul(xc, w_qkv, prologue="norm_mod", g=norm1_g[i], shift=m_c[0], scale=m_c[1],
                                 name="diff_qkv_ctx").reshape(b, lc, 3 * d)
            qkv_l = fused_matmul(xl, w_qkv, prologue="norm_mod", g=norm1_g[i], shift=m_l[0], scale=m_l[1],
                                 rows_per_batch=l, name="diff_qkv").reshape(b, l, 3 * d)
            o_l = diff_attention(qkv_l, qkv_l, qkv_c, lam, diff_subln_g[j], 1.0 - lambda_init, _rope_tables(l),
                                 name="diff_attention", **ATTN_TILES)
            xl = fused_matmul(o_l.reshape(b * l, d), w_out, gate=m_l[2], res=xl, rows_per_batch=l,
                              name="diff_out_proj")
            if need_ctx:
                o_c = diff_attention(qkv_c, None, qkv_c, lam, diff_subln_g[j], 1.0 - lambda_init, None,
                                     name="diff_attention_ctx", **ATTN_TILES_CTX)
                xc = fused_matmul(o_c.reshape(b * lc, d), w_out, gate=m_c[2], res=xc, name="diff_out_proj_ctx")

        w1, w3, w2 = moe_w1[i], moe_w3[i], moe_w2[i]
        xl = _moe(xl, norm2_g[i], m_l[3], m_l[4], m_l[5], l, b, l, router_w[i], w1, w3, w2)
        if need_ctx:
            xc = _moe(xc, norm2_g[i], m_c[3], m_c[4], m_c[5], None, b, lc, router_w[i], w1, w3, w2)

    zero = jnp.zeros((1, 1, d), F32)
    _, out = fused_matmul(xl, jnp.eye(d, dtype=BF16)[:, :LANES], prologue="norm_mod", g=final_g, shift=zero,
                          scale=zero, emit_h=F32, name="final_norm")
    return out.reshape(b, l, d)
```

```python
import functools
import math

import jax
import jax.numpy as jnp
from jax import lax
from jax.experimental import pallas as pl
from jax.experimental.pallas import tpu as pltpu

F32 = jnp.float32
BF16 = jnp.bfloat16

EPS = 1e-6
DEPTH = 4
N_MIXERS = 3
GRID_W = 64
ROPE_THETA = 10000.0

SSM_HEAD_DIM = 64
SSM_HEADS = 32
SSM_GROUPS = 8
SSM_STATE = 128
SSM_CHUNK = 128
SSM_CONV = 5
HEADS_PER_GROUP = SSM_HEADS // SSM_GROUPS
GROUP_WIDTH = HEADS_PER_GROUP * SSM_HEAD_DIM

FNET_GROUPS = 4

DIFF_HEADS = 8
DIFF_HEAD_DIM = 64
DIFF_V_DIM = 2 * DIFF_HEAD_DIM
DIFF_SCALE = DIFF_HEAD_DIM ** -0.5
ROPE_PAIRS = DIFF_HEAD_DIM // 4

N_EXPERTS = 16
EC_CAPACITY_FACTOR = 2

LANES = 128
VMEM_LIMIT_BYTES = 56 * 1024 * 1024


def _compiler_params(semantics):
    return pltpu.CompilerParams(dimension_semantics=semantics, vmem_limit_bytes=VMEM_LIMIT_BYTES)


def _split_bf16(x, parts):
    out = []
    r = x
    for _ in range(parts):
        p = r.astype(BF16)
        out.append(p)
        r = r - p.astype(F32)
    return out


def _dot(a, b):
    return jnp.dot(a, b, preferred_element_type=F32)


def _dot_nt(a, b):
    return lax.dot_general(a, b, (((1,), (1,)), ((), ())), preferred_element_type=F32)


def _dot_exact_rhs(x, m_bf16, parts=3, nt=False):
    f = _dot_nt if nt else _dot
    acc = None
    for p in _split_bf16(x, parts):
        t = f(p, m_bf16)
        acc = t if acc is None else acc + t
    return acc


def _dot_exact_lhs(m_bf16, x, parts=3):
    acc = None
    for p in _split_bf16(x, parts):
        t = _dot(m_bf16, p)
        acc = t if acc is None else acc + t
    return acc


def _mm_kernel(*refs, prologue, precise, has_bias, has_gate_res, emit_h, nk):
    it = iter(refs)
    x_ref = next(it)
    z_ref = g_ref = sh_ref = sc_ref = None
    if prologue == "norm_mod":
        g_ref, sh_ref, sc_ref = next(it), next(it), next(it)
    elif prologue == "gated_norm":
        z_ref, g_ref = next(it), next(it)
    w_ref = next(it)
    b_ref = next(it) if has_bias else None
    gate_ref = res_ref = None
    if has_gate_res:
        gate_ref, res_ref = next(it), next(it)
    o_ref = next(it)
    hout_ref = next(it) if emit_h else None
    h_ref = next(it) if prologue else None
    acc_ref = next(it) if nk > 1 else None

    j = pl.program_id(1)
    k = pl.program_id(2)

    if prologue:
        @pl.when(j == 0)
        def _():
            x = x_ref[...].astype(F32)
            if prologue == "gated_norm":
                z = z_ref[...].astype(F32)
                x = x * (z * jax.nn.sigmoid(z))
            ms = jnp.mean(x * x, axis=-1, keepdims=True)
            y = x * lax.rsqrt(ms + EPS) * g_ref[...]
            if prologue == "norm_mod":
                y = y * (1.0 + sc_ref[...]) + sh_ref[...]
            h_ref[...] = y.astype(h_ref.dtype)
            if emit_h:
                hout_ref[...] = y.astype(hout_ref.dtype)
        lhs = h_ref[...]
    else:
        lhs = x_ref[...]

    w = w_ref[...]
    if precise:
        l_hi, l_lo = _split_bf16(lhs.astype(F32), 2)
        w_hi, w_lo = _split_bf16(w.astype(F32), 2)
        part = _dot(l_hi, w_hi) + (_dot(l_lo, w_hi) + _dot(l_hi, w_lo))
    else:
        part = _dot(lhs.astype(BF16), w.astype(BF16))

    def finish(acc):
        if has_bias:
            acc = acc + b_ref[...]
        if has_gate_res:
            acc = res_ref[...].astype(F32) + gate_ref[...] * acc
        o_ref[...] = acc.astype(o_ref.dtype)

    if nk == 1:
        finish(part)
    else:
        @pl.when(k == 0)
        def _():
            acc_ref[...] = part

        @pl.when(k > 0)
        def _():
            acc_ref[...] += part

        @pl.when(k == nk - 1)
        def _():
            finish(acc_ref[...])


def fused_matmul(x, w, *, prologue=None, g=None, shift=None, scale=None, z=None, bias=None,
                 gate=None, res=None, rows_per_batch=None, precise=False, out_dtype=F32,
                 emit_h=None, tm=512, tn=1024, tk=None, name="fused_matmul"):
    m, kdim = x.shape
    n = w.shape[1]
    rows_per_batch = m if rows_per_batch is None else rows_per_batch
    tm = min(tm, m, rows_per_batch)
    tn = min(tn, n)
    tk = kdim if tk is None else min(tk, kdim)
    assert m % tm == 0 and n % tn == 0 and kdim % tk == 0, (x.shape, w.shape, tm, tn, tk)
    nk = kdim // tk
    assert not (prologue and nk > 1)
    assert rows_per_batch % tm == 0
    tiles_per_batch = rows_per_batch // tm

    def bidx(i):
        return i // tiles_per_batch

    args = [x]
    in_specs = [pl.BlockSpec((tm, tk), lambda i, j, k: (i, k))]
    if prologue == "norm_mod":
        args += [g.reshape(1, kdim), shift, scale]
        in_specs += [pl.BlockSpec((1, kdim), lambda i, j, k: (0, 0)),
                     pl.BlockSpec((None, 1, kdim), lambda i, j, k: (bidx(i), 0, 0)),
                     pl.BlockSpec((None, 1, kdim), lambda i, j, k: (bidx(i), 0, 0))]
    elif prologue == "gated_norm":
        args += [z, g.reshape(1, kdim)]
        in_specs += [pl.BlockSpec((tm, kdim), lambda i, j, k: (i, 0)),
                     pl.BlockSpec((1, kdim), lambda i, j, k: (0, 0))]
    args.append(w)
    in_specs.append(pl.BlockSpec((tk, tn), lambda i, j, k: (k, j)))
    if bias is not None:
        args.append(bias.reshape(1, n))
        in_specs.append(pl.BlockSpec((1, tn), lambda i, j, k: (0, j)))
    if gate is not None:
        args += [gate, res]
        in_specs += [pl.BlockSpec((None, 1, tn), lambda i, j, k: (bidx(i), 0, j)),
                     pl.BlockSpec((tm, tn), lambda i, j, k: (i, j))]

    out_shape = [jax.ShapeDtypeStruct((m, n), out_dtype)]
    out_specs = [pl.BlockSpec((tm, tn), lambda i, j, k: (i, j))]
    if emit_h is not None:
        out_shape.append(jax.ShapeDtypeStruct((m, kdim), emit_h))
        out_specs.append(pl.BlockSpec((tm, kdim), lambda i, j, k: (i, 0)))

    scratch = []
    if prologue:
        scratch.append(pltpu.VMEM((tm, kdim), F32 if precise else BF16))
    if nk > 1:
        scratch.append(pltpu.VMEM((tm, tn), F32))

    kern = functools.partial(_mm_kernel, prologue=prologue, precise=precise, has_bias=bias is not None,
                             has_gate_res=gate is not None, emit_h=emit_h is not None, nk=nk)
    outs = pl.pallas_call(
        kern, grid=(m // tm, n // tn, nk), in_specs=in_specs, out_specs=out_specs, out_shape=out_shape,
        scratch_shapes=scratch, name=name,
        compiler_params=_compiler_params(("parallel", "arbitrary", "arbitrary")))(*args)
    return outs if emit_h is not None else outs[0]


def _ssd_kernel(*refs, reverse, combine, q):
    it = iter(refs)
    xs_ref, bm_ref, cm_ref, dt_ref, dtt_ref, a_ref, at_ref, h0_ref = (next(it) for _ in range(8))
    yprev_ref = dskip_ref = None
    if combine:
        yprev_ref, dskip_ref = next(it), next(it)
    y_ref, hfin_ref, state_ref = next(it), next(it), next(it)

    c = pl.program_id(1)

    @pl.when(c == 0)
    def _():
        state_ref[...] = h0_ref[...]

    ii = lax.broadcasted_iota(jnp.int32, (q, q), 0)
    jj = lax.broadcasted_iota(jnp.int32, (q, q), 1)
    mask = (ii <= jj) if reverse else (ii >= jj)
    tri = jnp.where(mask, 1.0, 0.0).astype(BF16)

    dt = dt_ref[...]
    dta = dt * a_ref[...]
    dta_t = dtt_ref[...] * at_ref[...]
    a_cs = _dot_exact_lhs(tri, dta)
    a_cs_t = _dot_exact_rhs(dta_t, tri, nt=True)
    edge = 0 if reverse else q - 1
    tot = a_cs[edge:edge + 1, :]
    to_end = jnp.exp(tot - a_cs)
    into = jnp.exp(a_cs)
    chunk_decay = jnp.exp(tot)

    hh = lax.broadcasted_iota(jnp.int32, (SSM_HEADS, SSM_HEADS * SSM_HEAD_DIM), 0)
    cc = lax.broadcasted_iota(jnp.int32, (SSM_HEADS, SSM_HEADS * SSM_HEAD_DIM), 1)
    expand = jnp.where(cc // SSM_HEAD_DIM == hh, 1.0, 0.0).astype(BF16)
    dt_x = _dot_exact_rhs(dt, expand, parts=2)
    to_end_x = _dot_exact_rhs(to_end, expand, parts=2)
    into_x = _dot_exact_rhs(into, expand, parts=2)
    decay_x = _dot_exact_rhs(jnp.broadcast_to(chunk_decay, (8, SSM_HEADS)), expand, parts=3)[0:1, :]

    for g in range(SSM_GROUPS):
        c0 = g * GROUP_WIDTH
        xs_g = xs_ref[:, c0:c0 + GROUP_WIDTH]
        bm_g = bm_ref[:, g * SSM_STATE:(g + 1) * SSM_STATE]
        cm_g = cm_ref[:, g * SSM_STATE:(g + 1) * SSM_STATE].astype(BF16)
        xdt_g = xs_g * dt_x[:, c0:c0 + GROUP_WIDTH]
        xdt_b = xdt_g.astype(BF16)
        cb = _dot_nt(cm_g, bm_g.astype(BF16))
        st = state_ref[g]
        y_g = _dot(cm_g, st.astype(BF16)) * into_x[:, c0:c0 + GROUP_WIDTH]
        if combine:
            y_g = y_g + yprev_ref[:, c0:c0 + GROUP_WIDTH] + dskip_ref[:, c0:c0 + GROUP_WIDTH] * xs_g
        y_ref[:, c0:c0 + GROUP_WIDTH] = y_g
        for r in range(HEADS_PER_GROUP):
            h = g * HEADS_PER_GROUP + r
            seg = a_cs[:, h:h + 1] - a_cs_t[h:h + 1, :]
            lmat = (jnp.exp(jnp.where(mask, seg, -1e30)) * cb).astype(BF16)
            lo = c0 + r * SSM_HEAD_DIM
            y_ref[:, lo:lo + SSM_HEAD_DIM] += _dot(lmat, xdt_b[:, r * SSM_HEAD_DIM:(r + 1) * SSM_HEAD_DIM])
        xdtw = (xdt_g * to_end_x[:, c0:c0 + GROUP_WIDTH]).astype(BF16)
        new = _dot(bm_g.T.astype(BF16), xdtw)
        state_ref[g] = st * decay_x[:, c0:c0 + GROUP_WIDTH] + new

    @pl.when(c == pl.num_programs(1) - 1)
    def _():
        hfin_ref[...] = state_ref[...]


def ssd_scan(xbc, dt, dt_t, a, h0, *, reverse, y_prev=None, d_skip=None, name="ssd_scan"):
    b, l, _ = xbc.shape
    q = SSM_CHUNK
    nc = l // q
    d_inner = SSM_HEADS * SSM_HEAD_DIM
    nbc = SSM_GROUPS * SSM_STATE
    combine = y_prev is not None

    def cidx(c):
        return (nc - 1 - c) if reverse else c

    xs_blocks = d_inner // d_inner
    in_specs = [
        pl.BlockSpec((None, q, d_inner), lambda i, c: (i, cidx(c), 0)),
        pl.BlockSpec((None, q, nbc), lambda i, c: (i, cidx(c), d_inner // nbc)),
        pl.BlockSpec((None, q, nbc), lambda i, c: (i, cidx(c), d_inner // nbc + 1)),
        pl.BlockSpec((None, q, SSM_HEADS), lambda i, c: (i, cidx(c), 0)),
        pl.BlockSpec((None, SSM_HEADS, q), lambda i, c: (i, 0, cidx(c))),
        pl.BlockSpec((1, SSM_HEADS), lambda i, c: (0, 0)),
        pl.BlockSpec((SSM_HEADS, 1), lambda i, c: (0, 0)),
        pl.BlockSpec((None, SSM_GROUPS, SSM_STATE, GROUP_WIDTH), lambda i, c: (i, 0, 0, 0)),
    ]
    del xs_blocks
    args = [xbc, xbc, xbc, dt, dt_t, a.reshape(1, SSM_HEADS), a.reshape(SSM_HEADS, 1), h0]
    if combine:
        in_specs += [pl.BlockSpec((None, q, d_inner), lambda i, c: (i, cidx(c), 0)),
                     pl.BlockSpec((1, d_inner), lambda i, c: (0, 0))]
        args += [y_prev, jnp.repeat(d_skip, SSM_HEAD_DIM).reshape(1, d_inner)]
    out_shape = [jax.ShapeDtypeStruct((b, l, d_inner), F32),
                 jax.ShapeDtypeStruct((b, SSM_GROUPS, SSM_STATE, GROUP_WIDTH), F32)]
    out_specs = [pl.BlockSpec((None, q, d_inner), lambda i, c: (i, cidx(c), 0)),
                 pl.BlockSpec((None, SSM_GROUPS, SSM_STATE, GROUP_WIDTH), lambda i, c: (i, 0, 0, 0))]
    kern = functools.partial(_ssd_kernel, reverse=reverse, combine=combine, q=q)
    return pl.pallas_call(
        kern, grid=(b, nc), in_specs=in_specs, out_specs=out_specs, out_shape=out_shape,
        scratch_shapes=[pltpu.VMEM((SSM_GROUPS, SSM_STATE, GROUP_WIDTH), F32)], name=name,
        compiler_params=_compiler_params(("parallel", "arbitrary")))(*args)


def _rope(x, cos, sin_up, sin_dn):
    return (x * cos + pltpu.roll(x, ROPE_PAIRS, 1) * sin_up
            + pltpu.roll(x, LANES - ROPE_PAIRS, 1) * sin_dn)


LOG2E = 1.4426950408889634


def _attn_kernel(*refs, n_lat, n_ctx, tk, rope, out_scale):
    it = iter(refs)
    q_ref = next(it)
    kl_ref = vl_ref = None
    if n_lat:
        kl_ref, vl_ref = next(it), next(it)
    kc_ref, vc_ref = next(it), next(it)
    if rope:
        cq_ref, suq_ref, sdq_ref, ck_ref, suk_ref, sdk_ref = (next(it) for _ in range(6))
    lam_ref, g_ref, o_ref, k_s, vt_s, m_s, l_s, acc_s, s_s = (next(it) for _ in range(9))

    qi = pl.program_id(2)
    tq = q_ref.shape[0]
    n_tiles = (n_lat + n_ctx) // tk

    @pl.when(qi == 0)
    def _():
        if n_lat:
            k = kl_ref[...]
            if rope:
                k = _rope(k, ck_ref[...], suk_ref[...], sdk_ref[...])
            k_s[0:n_lat, :] = k.astype(BF16)
        k_s[n_lat:n_lat + n_ctx, :] = kc_ref[...].astype(BF16)
        for t in range(n_tiles):
            lo, hi = t * tk, (t + 1) * tk
            pieces = []
            if lo < n_lat:
                pieces.append(vl_ref[lo:min(hi, n_lat), :])
            if hi > n_lat:
                pieces.append(vc_ref[max(lo, n_lat) - n_lat:hi - n_lat, :])
            v = pieces[0] if len(pieces) == 1 else jnp.concatenate(pieces, axis=0)
            vt_s[t] = v.T.astype(BF16)

    q = q_ref[...] * (DIFF_SCALE * LOG2E)
    if rope:
        q = _rope(q, cq_ref[...], suq_ref[...], sdq_ref[...])
    qt = q.T
    row = lax.broadcasted_iota(jnp.int32, qt.shape, 0)
    wq = jnp.concatenate([jnp.where(row < DIFF_HEAD_DIM, qt, 0.0),
                          jnp.where(row >= DIFF_HEAD_DIM, qt, 0.0)], axis=1).astype(BF16)

    m_s[...] = jnp.full(m_s.shape, -1e30, F32)
    l_s[...] = jnp.zeros(l_s.shape, F32)
    acc_s[...] = jnp.zeros(acc_s.shape, F32)

    def scores(t):
        off = pl.multiple_of(t * tk, tk)
        return _dot(k_s[pl.ds(off, tk), :], wq)

    def absorb(buf, t):
        s = s_s[buf]
        m_old = m_s[...]
        m_new = jnp.maximum(m_old, jnp.max(s, axis=0, keepdims=True))
        alpha = jnp.exp2(m_old - m_new)
        p = jnp.exp2(s - m_new)
        l_s[...] = alpha * l_s[...] + jnp.sum(p, axis=0, keepdims=True)
        acc_s[...] = alpha * acc_s[...] + _dot(vt_s[t], p.astype(BF16))
        m_s[...] = m_new

    s_s[0] = scores(0)

    def body(i, carry):
        t0 = 2 * i
        s_s[1] = scores(t0 + 1)
        absorb(0, t0)
        s_s[0] = scores(t0 + 2)
        absorb(1, t0 + 1)
        return carry

    lax.fori_loop(0, n_tiles // 2 - 1, body, 0)
    s_s[1] = scores(n_tiles - 1)
    absorb(0, n_tiles - 2)
    absorb(1, n_tiles - 1)

    o_t = acc_s[...] / l_s[...]
    o = (o_t[:, :tq] - lam_ref[0:1, 0:1] * o_t[:, tq:]).T
    ms = jnp.mean(o * o, axis=-1, keepdims=True)
    o_ref[...] = o * lax.rsqrt(ms + EPS) * (g_ref[...] * out_scale)


ATTN_TILES = dict(tq=256, tk=1408)
ATTN_TILES_CTX = dict(tq=256, tk=128)


def diff_attention(qkv_q, qkv_lat, qkv_ctx, lam, subln_g, out_scale, rope_tabs, *, tq, tk, name):
    b, lq, d3 = qkv_q.shape
    d = d3 // 3
    nh = d // DIFF_V_DIM
    n_lat = 0 if qkv_lat is None else qkv_lat.shape[1]
    n_ctx = qkv_ctx.shape[1]
    rope = rope_tabs is not None
    tq = min(tq, lq)
    tk = min(tk, n_lat + n_ctx)
    assert lq % tq == 0 and (n_lat + n_ctx) % (2 * tk) == 0

    args = [qkv_q]
    in_specs = [pl.BlockSpec((None, tq, DIFF_V_DIM), lambda i, h, t: (i, t, h))]
    if n_lat:
        args += [qkv_lat, qkv_lat]
        in_specs += [pl.BlockSpec((None, n_lat, DIFF_V_DIM), lambda i, h, t: (i, 0, nh + h)),
                     pl.BlockSpec((None, n_lat, DIFF_V_DIM), lambda i, h, t: (i, 0, 2 * nh + h))]
    args += [qkv_ctx, qkv_ctx]
    in_specs += [pl.BlockSpec((None, n_ctx, DIFF_V_DIM), lambda i, h, t: (i, 0, nh + h)),
                 pl.BlockSpec((None, n_ctx, DIFF_V_DIM), lambda i, h, t: (i, 0, 2 * nh + h))]
    if rope:
        args += list(rope_tabs) + list(rope_tabs)
        in_specs += [pl.BlockSpec((tq, LANES), lambda i, h, t: (t, 0))] * 3
        in_specs += [pl.BlockSpec((n_lat, LANES), lambda i, h, t: (0, 0), pipeline_mode=pl.Buffered(1))] * 3
    args += [jnp.broadcast_to(lam.astype(F32).reshape(1, 1), (8, LANES)), subln_g.reshape(1, DIFF_V_DIM)]
    in_specs += [pl.BlockSpec((8, LANES), lambda i, h, t: (0, 0)),
                 pl.BlockSpec((1, DIFF_V_DIM), lambda i, h, t: (0, 0))]

    kern = functools.partial(_attn_kernel, n_lat=n_lat, n_ctx=n_ctx, tk=tk, rope=rope, out_scale=out_scale)
    return pl.pallas_call(
        kern, grid=(b, nh, lq // tq), in_specs=in_specs,
        out_specs=pl.BlockSpec((None, tq, DIFF_V_DIM), lambda i, h, t: (i, t, h)),
        out_shape=jax.ShapeDtypeStruct((b, lq, d), F32),
        scratch_shapes=[pltpu.VMEM((n_lat + n_ctx, DIFF_V_DIM), BF16),
                        pltpu.VMEM(((n_lat + n_ctx) // tk, DIFF_V_DIM, tk), BF16),
                        pltpu.VMEM((1, 2 * tq), F32), pltpu.VMEM((1, 2 * tq), F32),
                        pltpu.VMEM((DIFF_V_DIM, 2 * tq), F32),
                        pltpu.VMEM((2, tk, 2 * tq), F32)],
        name=name,
        compiler_params=_compiler_params(("parallel", "parallel", "arbitrary")))(*args)


def _rope_tables(l):
    rows = l // GRID_W
    row = jnp.repeat(jnp.arange(rows), GRID_W).astype(F32)
    col = jnp.tile(jnp.arange(GRID_W), rows).astype(F32)
    inv = ROPE_THETA ** (-jnp.arange(ROPE_PAIRS, dtype=F32) / ROPE_PAIRS)
    lane = jnp.arange(LANES)
    within = lane % DIFF_HEAD_DIM
    axis = within // (2 * ROPE_PAIRS)
    second = (within // ROPE_PAIRS) % 2
    pos = jnp.where(axis[None, :] == 0, row[:, None], col[:, None])
    ang = pos * inv[within % ROPE_PAIRS][None, :]
    cos, sin = jnp.cos(ang), jnp.sin(ang)
    sin_up = jnp.where(second[None, :] == 1, sin, 0.0)
    sin_dn = jnp.where(second[None, :] == 0, -sin, 0.0)
    return cos, sin_up, sin_dn


FFN_F_SPLIT = 2


def _ffn_kernel(x_ref, w1_ref, w3_ref, w2_ref, gate_ref, o_ref, w1_s, w3_s, w2_s):
    @pl.when(pl.program_id(1) == 0)
    def _():
        w1_s[...] = w1_ref[...].astype(BF16)
        w3_s[...] = w3_ref[...].astype(BF16)
        w2_s[...] = w2_ref[...].astype(BF16)

    x = x_ref[...]
    fs = w1_s.shape[1] // FFN_F_SPLIT
    acc = None
    for s in range(FFN_F_SPLIT):
        h1 = _dot(x, w1_s[:, s * fs:(s + 1) * fs])
        h3 = _dot(x, w3_s[:, s * fs:(s + 1) * fs])
        hid = (h1 * jax.nn.sigmoid(h1) * h3).astype(BF16)
        part = _dot(hid, w2_s[s * fs:(s + 1) * fs, :])
        acc = part if acc is None else acc + part
    o_ref[...] = acc * gate_ref[...]


def expert_ffn(xs, w1, w3, w2, gate, *, tm=512, name="expert_ffn"):
    e, r, d = xs.shape
    f = w1.shape[2]
    tm = min(tm, r)
    assert r % tm == 0 and f % FFN_F_SPLIT == 0
    once = pl.Buffered(1)
    return pl.pallas_call(
        _ffn_kernel, grid=(e, r // tm),
        in_specs=[pl.BlockSpec((None, tm, d), lambda i, t: (i, t, 0)),
                  pl.BlockSpec((None, d, f), lambda i, t: (i, 0, 0), pipeline_mode=once),
                  pl.BlockSpec((None, d, f), lambda i, t: (i, 0, 0), pipeline_mode=once),
                  pl.BlockSpec((None, f, d), lambda i, t: (i, 0, 0), pipeline_mode=once),
                  pl.BlockSpec((None, tm, 1), lambda i, t: (i, t, 0))],
        out_specs=pl.BlockSpec((None, tm, d), lambda i, t: (i, t, 0)),
        out_shape=jax.ShapeDtypeStruct((e, r, d), F32),
        scratch_shapes=[pltpu.VMEM((d, f), BF16), pltpu.VMEM((d, f), BF16), pltpu.VMEM((f, d), BF16)],
        name=name,
        compiler_params=_compiler_params(("arbitrary", "arbitrary")))(xs, w1, w3, w2, gate)


def _pad_cols(w, n):
    return jnp.pad(w, ((0, 0), (0, n - w.shape[1])))


def _ssd_project(x2d, g, shift, scale, rows_per_batch, w_main, w_dt, conv_w, conv_b, dt_bias, b, l):
    d_inner = SSM_HEADS * SSM_HEAD_DIM
    main = fused_matmul(x2d, w_main, prologue="norm_mod", g=g, shift=shift, scale=scale,
                        rows_per_batch=rows_per_batch, name="ssd_in_proj")
    dt_raw = fused_matmul(x2d, w_dt, prologue="norm_mod", g=g, shift=shift, scale=scale,
                          rows_per_batch=rows_per_batch, precise=True, name="ssd_dt_proj")
    xbc = conv_silu(main.reshape(b, l, -1), d_inner, conv_w, conv_b)
    dt = jax.nn.softplus(dt_raw[:, :2 * SSM_HEADS].reshape(b, l, 2, SSM_HEADS) + dt_bias)
    return main, xbc, dt


def _ssd_bidirectional(xbc, dt, a, d_skip, h0_f, h0_b):
    dt_f, dt_b = dt[:, :, 0], dt[:, :, 1]
    y_f, hf = ssd_scan(xbc, dt_f, jnp.swapaxes(dt_f, 1, 2), a[0], h0_f, reverse=False, name="ssd_scan_fwd")
    y, hb = ssd_scan(xbc, dt_b, jnp.swapaxes(dt_b, 1, 2), a[1], h0_b, reverse=True, y_prev=y_f, d_skip=d_skip,
                     name="ssd_scan_bwd")
    return y, hf, hb


def _moe(x2d, g, shift, scale, gate_mod, rows_per_batch, b, n, router_w, w1, w3, w2):
    d = x2d.shape[1]
    cap = EC_CAPACITY_FACTOR * n // N_EXPERTS
    logits, h = fused_matmul(x2d, _pad_cols(router_w, LANES), prologue="norm_mod", g=g, shift=shift, scale=scale,
                             rows_per_batch=rows_per_batch, precise=True, emit_h=BF16, name="moe_router")
    aff = jax.nn.softmax(logits[:, :N_EXPERTS].reshape(b, n, N_EXPERTS), axis=-1)
    gate, idx = lax.top_k(jnp.swapaxes(aff, 1, 2), cap)
    bi = jnp.arange(b)[:, None, None]
    h3 = h.reshape(b, n, d)
    xs = jnp.swapaxes(h3[bi, idx], 0, 1).reshape(N_EXPERTS, b * cap, d)
    gate_e = jnp.swapaxes(gate, 0, 1).reshape(N_EXPERTS, b * cap, 1)
    ys = expert_ffn(xs, w1, w3, w2, gate_e)
    ys = jnp.swapaxes(ys.reshape(N_EXPERTS, b, cap, d), 0, 1)
    moe = jnp.zeros((b, n, d), F32).at[bi, idx].add(ys)
    x3 = x2d.reshape(b, n, d)
    gm = gate_mod if gate_mod.shape[0] == b else jnp.broadcast_to(gate_mod, (b, 1, d))
    return (x3 + gm * moe).reshape(b * n, d)


def _dft_tables(l, ch):
    def cs(nn):
        idx = jnp.arange(nn)
        ang = ((idx[:, None] * idx[None, :]) % nn).astype(F32) * (2.0 * math.pi / nn)
        s = 1.0 / math.sqrt(nn)
        return jnp.cos(ang) * s, jnp.sin(ang) * s
    cl, sl = cs(l)
    cc, sc = cs(ch)
    return cl, sl, cc, sc


def _fourier(h2d, b, l, out_w, out_b, gate, res, rows_per_batch):
    d = h2d.shape[1]
    ch = d // FNET_GROUPS
    cl, sl, cc, sc = _dft_tables(l, ch)
    eye = jnp.eye(FNET_GROUPS, dtype=F32)
    w_ch = jnp.concatenate([jnp.kron(eye, cc), jnp.kron(eye, sc)], axis=1).astype(BF16)
    pq = fused_matmul(h2d, w_ch, out_dtype=BF16, name="fnet_channel_dft")
    pq = pq.reshape(b, l, 2, d)
    rhs = jnp.transpose(pq, (2, 1, 0, 3)).reshape(2 * l, b * d)
    lhs = jnp.concatenate([cl, -sl], axis=1).astype(BF16)
    mixed = fused_matmul(lhs, rhs, out_dtype=BF16, tk=2048, name="fnet_position_dft")
    mixed = jnp.transpose(mixed.reshape(l, b, d), (1, 0, 2)).reshape(b * l, d)
    return fused_matmul(mixed, out_w.astype(BF16), bias=out_b, gate=gate, res=res,
                        rows_per_batch=rows_per_batch, name="fnet_out_proj")


FFT_N1 = 32
FFT_ROWS = 8
FFT_CH = 128


def _fnet_kernel(x_ref, cc_ref, sc_ref, kc_ref, ks_ref, mc_ref, ms_ref, o_ref, zr_s, zq_s, ar_s, ai_s):
    n1, nj, rows, ch = zr_s.shape
    n2 = nj * rows
    x = x_ref[...]
    zr_s[...] = _dot(x, cc_ref[...]).reshape(zr_s.shape)
    zq_s[...] = _dot(x, sc_ref[...]).reshape(zq_s.shape)
    kc, ks = kc_ref[...], ks_ref[...]

    def stage1(j, carry):
        p = zr_s[:, j].reshape(n1 * rows, ch).astype(BF16)
        q = zq_s[:, j].reshape(n1 * rows, ch).astype(BF16)
        ar_s[:, j] = (_dot(kc, p) - _dot(ks, q)).reshape(n1, rows, ch)
        ai_s[:, j] = (_dot(kc, q) + _dot(ks, p)).reshape(n1, rows, ch)
        return carry

    lax.fori_loop(0, nj, stage1, 0)

    def stage2(k1, carry):
        ar = ar_s[k1].reshape(n2, ch).astype(BF16)
        ai = ai_s[k1].reshape(n2, ch).astype(BF16)
        o_ref[k1] = (_dot(mc_ref[k1], ar) - _dot(ms_ref[k1], ai)).astype(o_ref.dtype)
        return carry

    lax.fori_loop(0, n1, stage2, 0)


def _fnet_tables(l, ch):
    n1, n2 = FFT_N1, l // FFT_N1
    def cs(num, den, scale):
        ang = (num % den).astype(F32) * (2.0 * math.pi / den)
        return jnp.cos(ang) * scale, jnp.sin(ang) * scale
    ic = jnp.arange(ch)
    cc, sc = cs(ic[:, None] * ic[None, :], ch, 1.0 / math.sqrt(ch))
    i1 = jnp.arange(n1)
    c1, s1 = cs(i1[:, None] * i1[None, :], n1, 1.0)
    eye = jnp.eye(FFT_ROWS, dtype=F32)
    kc, ks = jnp.kron(c1, eye), jnp.kron(s1, eye)
    k = i1[:, None, None] + n1 * jnp.arange(n2)[None, :, None]
    mc, ms = cs(k * jnp.arange(n2)[None, None, :], l, 1.0 / math.sqrt(l))
    return [t.astype(BF16) for t in (cc, sc, kc, ks, mc, ms)]


def fourier_mix(h, name="fnet_mix"):
    b, l, d = h.shape
    ch = d // FNET_GROUPS
    n1, n2 = FFT_N1, l // FFT_N1
    halves = ch // FFT_CH
    cc, sc, kc, ks, mc, ms = _fnet_tables(l, ch)
    once = pl.Buffered(1)
    const2 = lambda i, g, s: (0, 0)
    return pl.pallas_call(
        _fnet_kernel, grid=(b, FNET_GROUPS, halves),
        in_specs=[pl.BlockSpec((None, l, ch), lambda i, g, s: (i, 0, g)),
                  pl.BlockSpec((ch, FFT_CH), lambda i, g, s: (0, s)),
                  pl.BlockSpec((ch, FFT_CH), lambda i, g, s: (0, s)),
                  pl.BlockSpec((n1 * FFT_ROWS, n1 * FFT_ROWS), const2, pipeline_mode=once),
                  pl.BlockSpec((n1 * FFT_ROWS, n1 * FFT_ROWS), const2, pipeline_mode=once),
                  pl.BlockSpec((n1, n2, n2), lambda i, g, s: (0, 0, 0), pipeline_mode=once),
                  pl.BlockSpec((n1, n2, n2), lambda i, g, s: (0, 0, 0), pipeline_mode=once)],
        out_specs=pl.BlockSpec((None, n1, n2, FFT_CH), lambda i, g, s: (i, 0, 0, g * halves + s)),
        out_shape=jax.ShapeDtypeStruct((b, n1, n2, d), BF16),
        scratch_shapes=[pltpu.VMEM((n1, n2 // FFT_ROWS, FFT_ROWS, FFT_CH), F32) for _ in range(4)],
        name=name,
        compiler_params=_compiler_params(("parallel", "arbitrary", "arbitrary")))(h, cc, sc, kc, ks, mc, ms)


def fnet_out_proj(mixed, w, bias, gate, res, name="fnet_out_proj"):
    b, n1, n2, d = mixed.shape
    n = w.shape[1]
    res3 = res.reshape(b, n2, n1 * n)
    kern = functools.partial(_mm_kernel, prologue=None, precise=False, has_bias=True, has_gate_res=True,
                             emit_h=False, nk=1)
    rows = lambda i, j, k: (i // n1, 0, i % n1)
    out = pl.pallas_call(
        kern, grid=(b * n1, 1, 1),
        in_specs=[pl.BlockSpec((None, None, n2, d), lambda i, j, k: (i // n1, i % n1, 0, 0)),
                  pl.BlockSpec((d, n), lambda i, j, k: (0, 0)),
                  pl.BlockSpec((1, n), lambda i, j, k: (0, 0)),
                  pl.BlockSpec((None, 1, n), lambda i, j, k: (i // n1, 0, 0)),
                  pl.BlockSpec((None, n2, n), rows)],
        out_specs=pl.BlockSpec((None, n2, n), rows),
        out_shape=jax.ShapeDtypeStruct((b, n2, n1 * n), F32), name=name,
        compiler_params=_compiler_params(("parallel", "arbitrary", "arbitrary")))(
            mixed, w, bias.reshape(1, n), gate, res3)
    return out.reshape(b * n2 * n1, n)


CONV_HALO = 8


def _conv_kernel(prev_ref, x_ref, next_ref, w_ref, b_ref, o_ref, *, taps):
    i = pl.program_id(1)
    last = pl.num_programs(1) - 1
    tl = x_ref.shape[0]
    prev = jnp.where(i > 0, prev_ref[...], 0.0)
    nxt = jnp.where(i < last, next_ref[...], 0.0)
    ext = jnp.concatenate([prev, x_ref[...], nxt], axis=0)
    acc = b_ref[...]
    for k in range(taps):
        lo = CONV_HALO - taps // 2 + k
        acc = acc + ext[lo:lo + tl, :] * w_ref[k:k + 1, :]
    o_ref[...] = acc * jax.nn.sigmoid(acc)


def conv_silu(main, col0, conv_w, conv_b, *, tl=512, tc=1024, name="ssd_conv"):
    b, l, _ = main.shape
    taps, c = conv_w.shape
    tl = min(tl, l)
    assert l % tl == 0 and c % tc == 0 and col0 % tc == 0 and tl % CONV_HALO == 0
    cb = col0 // tc
    hb = tl // CONV_HALO
    nh = l // CONV_HALO
    kern = functools.partial(_conv_kernel, taps=taps)
    return pl.pallas_call(
        kern, grid=(b, l // tl, c // tc),
        in_specs=[pl.BlockSpec((None, CONV_HALO, tc), lambda n, i, j: (n, jnp.maximum(i * hb - 1, 0), cb + j)),
                  pl.BlockSpec((None, tl, tc), lambda n, i, j: (n, i, cb + j)),
                  pl.BlockSpec((None, CONV_HALO, tc), lambda n, i, j: (n, jnp.minimum((i + 1) * hb, nh - 1), cb + j)),
                  pl.BlockSpec((taps, tc), lambda n, i, j: (0, j)),
                  pl.BlockSpec((1, tc), lambda n, i, j: (0, j))],
        out_specs=pl.BlockSpec((None, tl, tc), lambda n, i, j: (n, i, j)),
        out_shape=jax.ShapeDtypeStruct((b, l, c), F32), name=name,
        compiler_params=_compiler_params(("parallel", "parallel", "parallel")))(
            main, main, main, conv_w, conv_b.reshape(1, c))


def kernel(x, c, ctx, c_ctx, mod_w, mod_b, norm1_g, norm2_g, ssd_in_w, ssd_conv_w, ssd_conv_b, ssd_dt_bias,
           ssd_a_log, ssd_d, ssd_norm_g, ssd_out_w, fnet_out_w, fnet_out_b, diff_qkv_w, diff_out_w, diff_lambda,
           diff_subln_g, router_w, moe_w1, moe_w3, moe_w2, final_g):
    b, l, d = x.shape
    lc = ctx.shape[1]
    d_inner = SSM_HEADS * SSM_HEAD_DIM

    cond = jnp.concatenate([c, c_ctx[None], jnp.zeros(((8 - (b + 1) % 8) % 8, d), F32)], axis=0)
    cond = jax.nn.silu(cond)
    xl = x.reshape(b * l, d)
    xc = ctx.reshape(b * lc, d)

    for i in range(DEPTH):
        need_ctx = i < DEPTH - 1
        kind, j = i % N_MIXERS, i // N_MIXERS
        mod = fused_matmul(cond, mod_w[i], bias=mod_b[i], tn=1024, name="modulation")
        m_l = [mod[:b, t * d:(t + 1) * d].reshape(b, 1, d) for t in range(6)]
        m_c = [mod[b:b + 1, t * d:(t + 1) * d].reshape(1, 1, d) for t in range(6)]

        if kind == 0:
            w_main = ssd_in_w[j][:, :d_inner + d_inner + 2 * SSM_GROUPS * SSM_STATE].astype(BF16)
            w_dt = _pad_cols(ssd_in_w[j][:, w_main.shape[1]:], LANES)
            a = -jnp.exp(ssd_a_log[j].astype(F32))
            h0 = jnp.zeros((b, SSM_GROUPS, SSM_STATE, GROUP_WIDTH), F32)
            z_c, xbc_c, dt_c = _ssd_project(xc, norm1_g[i], m_c[0], m_c[1], None, w_main, w_dt,
                                            ssd_conv_w[j], ssd_conv_b[j], ssd_dt_bias[j], b, lc)
            y_c, hf, hb = _ssd_bidirectional(xbc_c, dt_c, a, ssd_d[j], h0, h0)
            z_l, xbc_l, dt_l = _ssd_project(xl, norm1_g[i], m_l[0], m_l[1], l, w_main, w_dt,
                                            ssd_conv_w[j], ssd_conv_b[j], ssd_dt_bias[j], b, l)
            y_l, _, _ = _ssd_bidirectional(xbc_l, dt_l, a, ssd_d[j], hf, hb)
            w_out = ssd_out_w[j].astype(BF16)
            xl = fused_matmul(y_l.reshape(b * l, d_inner), w_out, prologue="gated_norm",
                              z=z_l, g=ssd_norm_g[j], gate=m_l[2], res=xl,
                              rows_per_batch=l, name="ssd_out_proj")
            if need_ctx:
                xc = fused_matmul(y_c.reshape(b * lc, d_inner), w_out, prologue="gated_norm",
                                  z=z_c, g=ssd_norm_g[j], gate=m_c[2], res=xc,
                                  name="ssd_out_proj_ctx")
        elif kind == 1:
            ident = jnp.eye(d, dtype=BF16)
            _, h_l = fused_matmul(xl, ident[:, :LANES], prologue="norm_mod", g=norm1_g[i], shift=m_l[0],
                                  scale=m_l[1], rows_per_batch=l, emit_h=BF16, name="fnet_norm")
            mixed = fourier_mix(h_l.reshape(b, l, d))
            xl = fnet_out_proj(mixed, fnet_out_w[j].astype(BF16), fnet_out_b[j], m_l[2], xl)
            if need_ctx:
                _, h_c = fused_matmul(xc, ident[:, :LANES], prologue="norm_mod", g=norm1_g[i], shift=m_c[0],
                                      scale=m_c[1], emit_h=BF16, name="fnet_norm_ctx")
                xc = _fourier(h_c, b, lc, fnet_out_w[j], fnet_out_b[j], m_c[2], xc, None)
        else:
            lambda_init = 0.8 - 0.6 * math.exp(-0.3 * i)
            lp = diff_lambda[j].astype(F32)
            lam = jnp.exp(jnp.sum(lp[0] * lp[1])) - jnp.exp(jnp.sum(lp[2] * lp[3])) + lambda_init
            w_qkv = diff_qkv_w[j].astype(BF16)
            w_out = diff_out_w[j].astype(BF16)
            qkv_c = fused_matmul(xc, w_qkv, prologue="norm_mod", g=norm1_g[i], shift=m_c[0], scale=m_c[1],
                                 name="diff_qkv_ctx").reshape(b, lc, 3 * d)
            qkv_l = fused_matmul(xl, w_qkv, prologue="norm_mod", g=norm1_g[i], shift=m_l[0], scale=m_l[1],
                                 rows_per_batch=l, name="diff_qkv").reshape(b, l, 3 * d)
            o_l = diff_attention(qkv_l, qkv_l, qkv_c, lam, diff_subln_g[j], 1.0 - lambda_init, _rope_tables(l),
                                 name="diff_attention", **ATTN_TILES)
            xl = fused_matmul(o_l.reshape(b * l, d), w_out, gate=m_l[2], res=xl, rows_per_batch=l,
                              name="diff_out_proj")
            if need_ctx:
                o_c = diff_attention(qkv_c, None, qkv_c, lam, diff_subln_g[j], 1.0 - lambda_init, None,
                                     name="diff_attention_ctx", **ATTN_TILES_CTX)
                xc = fused_matmul(o_c.reshape(b * lc, d), w_out, gate=m_c[2], res=xc, name="diff_out_proj_ctx")

        w1, w3, w2 = moe_w1[i], moe_w3[i], moe_w2[i]
        xl = _moe(xl, norm2_g[i], m_l[3], m_l[4], m_l[5], l, b, l, router_w[i], w1, w3, w2)
        if need_ctx:
            xc = _moe(xc, norm2_g[i], m_c[3], m_c[4], m_c[5], None, b, lc, router_w[i], w1, w3, w2)

    zero = jnp.zeros((1, 1, d), F32)
    _, out = fused_matmul(xl, jnp.eye(d, dtype=BF16)[:, :LANES], prologue="norm_mod", g=final_g, shift=zero,
                          scale=zero, emit_h=F32, name="final_norm")
    return out.reshape(b, l, d)
```

```python
import functools
import math

import jax
import jax.numpy as jnp
from jax import lax
from jax.experimental import pallas as pl
from jax.experimental.pallas import tpu as pltpu

F32 = jnp.float32
BF16 = jnp.bfloat16

EPS = 1e-6
DEPTH = 4
N_MIXERS = 3
GRID_W = 64
ROPE_THETA = 10000.0

SSM_HEAD_DIM = 64
SSM_HEADS = 32
SSM_GROUPS = 8
SSM_STATE = 128
SSM_CHUNK = 128
SSM_CONV = 5
HEADS_PER_GROUP = SSM_HEADS // SSM_GROUPS
GROUP_WIDTH = HEADS_PER_GROUP * SSM_HEAD_DIM

FNET_GROUPS = 4

DIFF_HEADS = 8
DIFF_HEAD_DIM = 64
DIFF_V_DIM = 2 * DIFF_HEAD_DIM
DIFF_SCALE = DIFF_HEAD_DIM ** -0.5
ROPE_PAIRS = DIFF_HEAD_DIM // 4

N_EXPERTS = 16
EC_CAPACITY_FACTOR = 2

LANES = 128
VMEM_LIMIT_BYTES = 56 * 1024 * 1024


def _compiler_params(semantics):
    return pltpu.CompilerParams(dimension_semantics=semantics, vmem_limit_bytes=VMEM_LIMIT_BYTES)


def _split_bf16(x, parts):
    out = []
    r = x
    for _ in range(parts):
        p = r.astype(BF16)
        out.append(p)
        r = r - p.astype(F32)
    return out


def _dot(a, b):
    return jnp.dot(a, b, preferred_element_type=F32)


def _dot_nt(a, b):
    return lax.dot_general(a, b, (((1,), (1,)), ((), ())), preferred_element_type=F32)


def _dot_exact_rhs(x, m_bf16, parts=3, nt=False):
    f = _dot_nt if nt else _dot
    acc = None
    for p in _split_bf16(x, parts):
        t = f(p, m_bf16)
        acc = t if acc is None else acc + t
    return acc


def _dot_exact_lhs(m_bf16, x, parts=3):
    acc = None
    for p in _split_bf16(x, parts):
        t = _dot(m_bf16, p)
        acc = t if acc is None else acc + t
    return acc


def _mm_kernel(*refs, prologue, precise, has_bias, has_gate_res, emit_h, nk):
    it = iter(refs)
    x_ref = next(it)
    z_ref = g_ref = sh_ref = sc_ref = None
    if prologue == "norm_mod":
        g_ref, sh_ref, sc_ref = next(it), next(it), next(it)
    elif prologue == "gated_norm":
        z_ref, g_ref = next(it), next(it)
    w_ref = next(it)
    b_ref = next(it) if has_bias else None
    gate_ref = res_ref = None
    if has_gate_res:
        gate_ref, res_ref = next(it), next(it)
    o_ref = next(it)
    hout_ref = next(it) if emit_h else None
    h_ref = next(it) if prologue else None
    acc_ref = next(it) if nk > 1 else None

    j = pl.program_id(1)
    k = pl.program_id(2)

    if prologue:
        @pl.when(j == 0)
        def _():
            x = x_ref[...].astype(F32)
            if prologue == "gated_norm":
                z = z_ref[...].astype(F32)
                x = x * (z * jax.nn.sigmoid(z))
            ms = jnp.mean(x * x, axis=-1, keepdims=True)
            y = x * lax.rsqrt(ms + EPS) * g_ref[...]
            if prologue == "norm_mod":
                y = y * (1.0 + sc_ref[...]) + sh_ref[...]
            h_ref[...] = y.astype(h_ref.dtype)
            if emit_h:
                hout_ref[...] = y.astype(hout_ref.dtype)
        lhs = h_ref[...]
    else:
        lhs = x_ref[...]

    w = w_ref[...]
    if precise:
        l_hi, l_lo = _split_bf16(lhs.astype(F32), 2)
        w_hi, w_lo = _split_bf16(w.astype(F32), 2)
        part = _dot(l_hi, w_hi) + (_dot(l_lo, w_hi) + _dot(l_hi, w_lo))
    else:
        part = _dot(lhs.astype(BF16), w.astype(BF16))

    def finish(acc):
        if has_bias:
            acc = acc + b_ref[...]
        if has_gate_res:
            acc = res_ref[...].astype(F32) + gate_ref[...] * acc
        o_ref[...] = acc.astype(o_ref.dtype)

    if nk == 1:
        finish(part)
    else:
        @pl.when(k == 0)
        def _():
            acc_ref[...] = part

        @pl.when(k > 0)
        def _():
            acc_ref[...] += part

        @pl.when(k == nk - 1)
        def _():
            finish(acc_ref[...])


def fused_matmul(x, w, *, prologue=None, g=None, shift=None, scale=None, z=None, bias=None,
                 gate=None, res=None, rows_per_batch=None, precise=False, out_dtype=F32,
                 emit_h=None, tm=512, tn=1024, tk=None, name="fused_matmul"):
    m, kdim = x.shape
    n = w.shape[1]
    rows_per_batch = m if rows_per_batch is None else rows_per_batch
    tm = min(tm, m, rows_per_batch)
    tn = min(tn, n)
    tk = kdim if tk is None else min(tk, kdim)
    assert m % tm == 0 and n % tn == 0 and kdim % tk == 0, (x.shape, w.shape, tm, tn, tk)
    nk = kdim // tk
    assert not (prologue and nk > 1)
    assert rows_per_batch % tm == 0
    tiles_per_batch = rows_per_batch // tm

    def bidx(i):
        return i // tiles_per_batch

    args = [x]
    in_specs = [pl.BlockSpec((tm, tk), lambda i, j, k: (i, k))]
    if prologue == "norm_mod":
        args += [g.reshape(1, kdim), shift, scale]
        in_specs += [pl.BlockSpec((1, kdim), lambda i, j, k: (0, 0)),
                     pl.BlockSpec((None, 1, kdim), lambda i, j, k: (bidx(i), 0, 0)),
                     pl.BlockSpec((None, 1, kdim), lambda i, j, k: (bidx(i), 0, 0))]
    elif prologue == "gated_norm":
        args += [z, g.reshape(1, kdim)]
        in_specs += [pl.BlockSpec((tm, kdim), lambda i, j, k: (i, 0)),
                     pl.BlockSpec((1, kdim), lambda i, j, k: (0, 0))]
    args.append(w)
    in_specs.append(pl.BlockSpec((tk, tn), lambda i, j, k: (k, j)))
    if bias is not None:
        args.append(bias.reshape(1, n))
        in_specs.append(pl.BlockSpec((1, tn), lambda i, j, k: (0, j)))
    if gate is not None:
        args += [gate, res]
        in_specs += [pl.BlockSpec((None, 1, tn), lambda i, j, k: (bidx(i), 0, j)),
                     pl.BlockSpec((tm, tn), lambda i, j, k: (i, j))]

    out_shape = [jax.ShapeDtypeStruct((m, n), out_dtype)]
    out_specs = [pl.BlockSpec((tm, tn), lambda i, j, k: (i, j))]
    if emit_h is not None:
        out_shape.append(jax.ShapeDtypeStruct((m, kdim), emit_h))
        out_specs.append(pl.BlockSpec((tm, kdim), lambda i, j, k: (i, 0)))

    scratch = []
    if prologue:
        scratch.append(pltpu.VMEM((tm, kdim), F32 if precise else BF16))
    if nk > 1:
        scratch.append(pltpu.VMEM((tm, tn), F32))

    kern = functools.partial(_mm_kernel, prologue=prologue, precise=precise, has_bias=bias is not None,
                             has_gate_res=gate is not None, emit_h=emit_h is not None, nk=nk)
    outs = pl.pallas_call(
        kern, grid=(m // tm, n // tn, nk), in_specs=in_specs, out_specs=out_specs, out_shape=out_shape,
        scratch_shapes=scratch, name=name,
        compiler_params=_compiler_params(("parallel", "arbitrary", "arbitrary")))(*args)
    return outs if emit_h is not None else outs[0]


def _ssd_kernel(*refs, reverse, combine, q):
    it = iter(refs)
    xs_ref, bm_ref, cm_ref, dt_ref, dtt_ref, a_ref, at_ref, h0_ref = (next(it) for _ in range(8))
    yprev_ref = dskip_ref = None
    if combine:
        yprev_ref, dskip_ref = next(it), next(it)
    y_ref, hfin_ref, state_ref = next(it), next(it), next(it)

    c = pl.program_id(1)

    @pl.when(c == 0)
    def _():
        state_ref[...] = h0_ref[...]

    ii = lax.broadcasted_iota(jnp.int32, (q, q), 0)
    jj = lax.broadcasted_iota(jnp.int32, (q, q), 1)
    mask = (ii <= jj) if reverse else (ii >= jj)
    tri = jnp.where(mask, 1.0, 0.0).astype(BF16)

    dt = dt_ref[...]
    dta = dt * a_ref[...]
    dta_t = dtt_ref[...] * at_ref[...]
    a_cs = _dot_exact_lhs(tri, dta)
    a_cs_t = _dot_exact_rhs(dta_t, tri, nt=True)
    edge = 0 if reverse else q - 1
    tot = a_cs[edge:edge + 1, :]
    to_end = jnp.exp(tot - a_cs)
    into = jnp.exp(a_cs)
    chunk_decay = jnp.exp(tot)

    hh = lax.broadcasted_iota(jnp.int32, (SSM_HEADS, SSM_HEADS * SSM_HEAD_DIM), 0)
    cc = lax.broadcasted_iota(jnp.int32, (SSM_HEADS, SSM_HEADS * SSM_HEAD_DIM), 1)
    expand = jnp.where(cc // SSM_HEAD_DIM == hh, 1.0, 0.0).astype(BF16)
    dt_x = _dot_exact_rhs(dt, expand, parts=2)
    to_end_x = _dot_exact_rhs(to_end, expand, parts=2)
    into_x = _dot_exact_rhs(into, expand, parts=2)
    decay_x = _dot_exact_rhs(jnp.broadcast_to(chunk_decay, (8, SSM_HEADS)), expand, parts=3)[0:1, :]

    for g in range(SSM_GROUPS):
        c0 = g * GROUP_WIDTH
        xs_g = xs_ref[:, c0:c0 + GROUP_WIDTH]
        bm_g = bm_ref[:, g * SSM_STATE:(g + 1) * SSM_STATE]
        cm_g = cm_ref[:, g * SSM_STATE:(g + 1) * SSM_STATE].astype(BF16)
        xdt_g = xs_g * dt_x[:, c0:c0 + GROUP_WIDTH]
        xdt_b = xdt_g.astype(BF16)
        cb = _dot_nt(cm_g, bm_g.astype(BF16))
        st = state_ref[g]
        y_g = _dot(cm_g, st.astype(BF16)) * into_x[:, c0:c0 + GROUP_WIDTH]
        if combine:
            y_g = y_g + yprev_ref[:, c0:c0 + GROUP_WIDTH] + dskip_ref[:, c0:c0 + GROUP_WIDTH] * xs_g
        y_ref[:, c0:c0 + GROUP_WIDTH] = y_g
        for r in range(HEADS_PER_GROUP):
            h = g * HEADS_PER_GROUP + r
            seg = a_cs[:, h:h + 1] - a_cs_t[h:h + 1, :]
            lmat = (jnp.exp(jnp.where(mask, seg, -1e30)) * cb).astype(BF16)
            lo = c0 + r * SSM_HEAD_DIM
            y_ref[:, lo:lo + SSM_HEAD_DIM] += _dot(lmat, xdt_b[:, r * SSM_HEAD_DIM:(r + 1) * SSM_HEAD_DIM])
        xdtw = (xdt_g * to_end_x[:, c0:c0 + GROUP_WIDTH]).astype(BF16)
        new = _dot(bm_g.T.astype(BF16), xdtw)
        state_ref[g] = st * decay_x[:, c0:c0 + GROUP_WIDTH] + new

    @pl.when(c == pl.num_programs(1) - 1)
    def _():
        hfin_ref[...] = state_ref[...]


def ssd_scan(xbc, dt, dt_t, a, h0, *, reverse, y_prev=None, d_skip=None, name="ssd_scan"):
    b, l, _ = xbc.shape
    q = SSM_CHUNK
    nc = l // q
    d_inner = SSM_HEADS * SSM_HEAD_DIM
    nbc = SSM_GROUPS * SSM_STATE
    combine = y_prev is not None

    def cidx(c):
        return (nc - 1 - c) if reverse else c

    xs_blocks = d_inner // d_inner
    in_specs = [
        pl.BlockSpec((None, q, d_inner), lambda i, c: (i, cidx(c), 0)),
        pl.BlockSpec((None, q, nbc), lambda i, c: (i, cidx(c), d_inner // nbc)),
        pl.BlockSpec((None, q, nbc), lambda i, c: (i, cidx(c), d_inner // nbc + 1)),
        pl.BlockSpec((None, q, SSM_HEADS), lambda i, c: (i, cidx(c), 0)),
        pl.BlockSpec((None, SSM_HEADS, q), lambda i, c: (i, 0, cidx(c))),
        pl.BlockSpec((1, SSM_HEADS), lambda i, c: (0, 0)),
        pl.BlockSpec((SSM_HEADS, 1), lambda i, c: (0, 0)),
        pl.BlockSpec((None, SSM_GROUPS, SSM_STATE, GROUP_WIDTH), lambda i, c: (i, 0, 0, 0)),
    ]
    del xs_blocks
    args = [xbc, xbc, xbc, dt, dt_t, a.reshape(1, SSM_HEADS), a.reshape(SSM_HEADS, 1), h0]
    if combine:
        in_specs += [pl.BlockSpec((None, q, d_inner), lambda i, c: (i, cidx(c), 0)),
                     pl.BlockSpec((1, d_inner), lambda i, c: (0, 0))]
        args += [y_prev, jnp.repeat(d_skip, SSM_HEAD_DIM).reshape(1, d_inner)]
    out_shape = [jax.ShapeDtypeStruct((b, l, d_inner), F32),
                 jax.ShapeDtypeStruct((b, SSM_GROUPS, SSM_STATE, GROUP_WIDTH), F32)]
    out_specs = [pl.BlockSpec((None, q, d_inner), lambda i, c: (i, cidx(c), 0)),
                 pl.BlockSpec((None, SSM_GROUPS, SSM_STATE, GROUP_WIDTH), lambda i, c: (i, 0, 0, 0))]
    kern = functools.partial(_ssd_kernel, reverse=reverse, combine=combine, q=q)
    return pl.pallas_call(
        kern, grid=(b, nc), in_specs=in_specs, out_specs=out_specs, out_shape=out_shape,
        scratch_shapes=[pltpu.VMEM((SSM_GROUPS, SSM_STATE, GROUP_WIDTH), F32)], name=name,
        compiler_params=_compiler_params(("parallel", "arbitrary")))(*args)


def _rope(x, cos, sin_up, sin_dn):
    return (x * cos + pltpu.roll(x, ROPE_PAIRS, 1) * sin_up
            + pltpu.roll(x, LANES - ROPE_PAIRS, 1) * sin_dn)


LOG2E = 1.4426950408889634


def _attn_kernel(*refs, n_lat, n_ctx, tk, rope, out_scale):
    it = iter(refs)
    q_ref = next(it)
    kl_ref = vl_ref = None
    if n_lat:
        kl_ref, vl_ref = next(it), next(it)
    kc_ref, vc_ref = next(it), next(it)
    if rope:
        cq_ref, suq_ref, sdq_ref, ck_ref, suk_ref, sdk_ref = (next(it) for _ in range(6))
    lam_ref, g_ref, o_ref, k_s, vt_s, m_s, l_s, acc_s, s_s = (next(it) for _ in range(9))

    qi = pl.program_id(2)
    tq = q_ref.shape[0]
    n_tiles = (n_lat + n_ctx) // tk

    @pl.when(qi == 0)
    def _():
        if n_lat:
            k = kl_ref[...]
            if rope:
                k = _rope(k, ck_ref[...], suk_ref[...], sdk_ref[...])
            k_s[0:n_lat, :] = k.astype(BF16)
        k_s[n_lat:n_lat + n_ctx, :] = kc_ref[...].astype(BF16)
        for t in range(n_tiles):
            lo, hi = t * tk, (t + 1) * tk
            pieces = []
            if lo < n_lat:
                pieces.append(vl_ref[lo:min(hi, n_lat), :])
            if hi > n_lat:
                pieces.append(vc_ref[max(lo, n_lat) - n_lat:hi - n_lat, :])
            v = pieces[0] if len(pieces) == 1 else jnp.concatenate(pieces, axis=0)
            vt_s[t] = v.T.astype(BF16)

    q = q_ref[...] * (DIFF_SCALE * LOG2E)
    if rope:
        q = _rope(q, cq_ref[...], suq_ref[...], sdq_ref[...])
    qt = q.T
    row = lax.broadcasted_iota(jnp.int32, qt.shape, 0)
    wq = jnp.concatenate([jnp.where(row < DIFF_HEAD_DIM, qt, 0.0),
                          jnp.where(row >= DIFF_HEAD_DIM, qt, 0.0)], axis=1).astype(BF16)

    m_s[...] = jnp.full(m_s.shape, -1e30, F32)
    l_s[...] = jnp.zeros(l_s.shape, F32)
    acc_s[...] = jnp.zeros(acc_s.shape, F32)

    def scores(t):
        off = pl.multiple_of(t * tk, tk)
        return _dot(k_s[pl.ds(off, tk), :], wq)

    def absorb(buf, t):
        s = s_s[buf]
        m_old = m_s[...]
        m_new = jnp.maximum(m_old, jnp.max(s, axis=0, keepdims=True))
        alpha = jnp.exp2(m_old - m_new)
        p = jnp.exp2(s - m_new)
        l_s[...] = alpha * l_s[...] + jnp.sum(p, axis=0, keepdims=True)
        acc_s[...] = alpha * acc_s[...] + _dot(vt_s[t], p.astype(BF16))
        m_s[...] = m_new

    s_s[0] = scores(0)

    def body(i, carry):
        t0 = 2 * i
        s_s[1] = scores(t0 + 1)
        absorb(0, t0)
        s_s[0] = scores(t0 + 2)
        absorb(1, t0 + 1)
        return carry

    lax.fori_loop(0, n_tiles // 2 - 1, body, 0)
    s_s[1] = scores(n_tiles - 1)
    absorb(0, n_tiles - 2)
    absorb(1, n_tiles - 1)

    o_t = acc_s[...] / l_s[...]
    o = (o_t[:, :tq] - lam_ref[0:1, 0:1] * o_t[:, tq:]).T
    ms = jnp.mean(o * o, axis=-1, keepdims=True)
    o_ref[...] = o * lax.rsqrt(ms + EPS) * (g_ref[...] * out_scale)


ATTN_TILES = dict(tq=256, tk=1408)
ATTN_TILES_CTX = dict(tq=256, tk=128)


def diff_attention(qkv_q, qkv_lat, qkv_ctx, lam, subln_g, out_scale, rope_tabs, *, tq, tk, name):
    b, lq, d3 = qkv_q.shape
    d = d3 // 3
    nh = d // DIFF_V_DIM
    n_lat = 0 if qkv_lat is None else qkv_lat.shape[1]
    n_ctx = qkv_ctx.shape[1]
    rope = rope_tabs is not None
    tq = min(tq, lq)
    tk = min(tk, n_lat + n_ctx)
    assert lq % tq == 0 and (n_lat + n_ctx) % (2 * tk) == 0

    args = [qkv_q]
    in_specs = [pl.BlockSpec((None, tq, DIFF_V_DIM), lambda i, h, t: (i, t, h))]
    if n_lat:
        args += [qkv_lat, qkv_lat]
        in_specs += [pl.BlockSpec((None, n_lat, DIFF_V_DIM), lambda i, h, t: (i, 0, nh + h)),
                     pl.BlockSpec((None, n_lat, DIFF_V_DIM), lambda i, h, t: (i, 0, 2 * nh + h))]
    args += [qkv_ctx, qkv_ctx]
    in_specs += [pl.BlockSpec((None, n_ctx, DIFF_V_DIM), lambda i, h, t: (i, 0, nh + h)),
                 pl.BlockSpec((None, n_ctx, DIFF_V_DIM), lambda i, h, t: (i, 0, 2 * nh + h))]
    if rope:
        args += list(rope_tabs) + list(rope_tabs)
        in_specs += [pl.BlockSpec((tq, LANES), lambda i, h, t: (t, 0))] * 3
        in_specs += [pl.BlockSpec((n_lat, LANES), lambda i, h, t: (0, 0), pipeline_mode=pl.Buffered(1))] * 3
    args += [jnp.broadcast_to(lam.astype(F32).reshape(1, 1), (8, LANES)), subln_g.reshape(1, DIFF_V_DIM)]
    in_specs += [pl.BlockSpec((8, LANES), lambda i, h, t: (0, 0)),
                 pl.BlockSpec((1, DIFF_V_DIM), lambda i, h, t: (0, 0))]

    kern = functools.partial(_attn_kernel, n_lat=n_lat, n_ctx=n_ctx, tk=tk, rope=rope, out_scale=out_scale)
    return pl.pallas_call(
        kern, grid=(b, nh, lq // tq), in_specs=in_specs,
        out_specs=pl.BlockSpec((None, tq, DIFF_V_DIM), lambda i, h, t: (i, t, h)),
        out_shape=jax.ShapeDtypeStruct((b, lq, d), F32),
        scratch_shapes=[pltpu.VMEM((n_lat + n_ctx, DIFF_V_DIM), BF16),
                        pltpu.VMEM(((n_lat + n_ctx) // tk, DIFF_V_DIM, tk), BF16),
                        pltpu.VMEM((1, 2 * tq), F32), pltpu.VMEM((1, 2 * tq), F32),
                        pltpu.VMEM((DIFF_V_DIM, 2 * tq), F32),
                        pltpu.VMEM((2, tk, 2 * tq), F32)],
        name=name,
        compiler_params=_compiler_params(("parallel", "parallel", "arbitrary")))(*args)


def _rope_tables(l):
    rows = l // GRID_W
    row = jnp.repeat(jnp.arange(rows), GRID_W).astype(F32)
    col = jnp.tile(jnp.arange(GRID_W), rows).astype(F32)
    inv = ROPE_THETA ** (-jnp.arange(ROPE_PAIRS, dtype=F32) / ROPE_PAIRS)
    lane = jnp.arange(LANES)
    within = lane % DIFF_HEAD_DIM
    axis = within // (2 * ROPE_PAIRS)
    second = (within // ROPE_PAIRS) % 2
    pos = jnp.where(axis[None, :] == 0, row[:, None], col[:, None])
    ang = pos * inv[within % ROPE_PAIRS][None, :]
    cos, sin = jnp.cos(ang), jnp.sin(ang)
    sin_up = jnp.where(second[None, :] == 1, sin, 0.0)
    sin_dn = jnp.where(second[None, :] == 0, -sin, 0.0)
    return cos, sin_up, sin_dn


FFN_F_SPLIT = 2


def _ffn_kernel(x_ref, w1_ref, w3_ref, w2_ref, gate_ref, o_ref, w1_s, w3_s, w2_s):
    @pl.when(pl.program_id(1) == 0)
    def _():
        w1_s[...] = w1_ref[...].astype(BF16)
        w3_s[...] = w3_ref[...].astype(BF16)
        w2_s[...] = w2_ref[...].astype(BF16)

    x = x_ref[...]
    fs = w1_s.shape[1] // FFN_F_SPLIT
    acc = None
    for s in range(FFN_F_SPLIT):
        h1 = _dot(x, w1_s[:, s * fs:(s + 1) * fs])
        h3 = _dot(x, w3_s[:, s * fs:(s + 1) * fs])
        hid = (h1 * jax.nn.sigmoid(h1) * h3).astype(BF16)
        part = _dot(hid, w2_s[s * fs:(s + 1) * fs, :])
        acc = part if acc is None else acc + part
    o_ref[...] = acc * gate_ref[...]


def expert_ffn(xs, w1, w3, w2, gate, *, tm=512, name="expert_ffn"):
    e, r, d = xs.shape
    f = w1.shape[2]
    tm = min(tm, r)
    assert r % tm == 0 and f % FFN_F_SPLIT == 0
    once = pl.Buffered(1)
    return pl.pallas_call(
        _ffn_kernel, grid=(e, r // tm),
        in_specs=[pl.BlockSpec((None, tm, d), lambda i, t: (i, t, 0)),
                  pl.BlockSpec((None, d, f), lambda i, t: (i, 0, 0), pipeline_mode=once),
                  pl.BlockSpec((None, d, f), lambda i, t: (i, 0, 0), pipeline_mode=once),
                  pl.BlockSpec((None, f, d), lambda i, t: (i, 0, 0), pipeline_mode=once),
                  pl.BlockSpec((None, tm, 1), lambda i, t: (i, t, 0))],
        out_specs=pl.BlockSpec((None, tm, d), lambda i, t: (i, t, 0)),
        out_shape=jax.ShapeDtypeStruct((e, r, d), F32),
        scratch_shapes=[pltpu.VMEM((d, f), BF16), pltpu.VMEM((d, f), BF16), pltpu.VMEM((f, d), BF16)],
        name=name,
        compiler_params=_compiler_params(("arbitrary", "arbitrary")))(xs, w1, w3, w2, gate)


ROUTE_CHUNK = 1024


def _lane_prefix_sum(x):
    n = x.shape[-1]
    lane = lax.broadcasted_iota(jnp.int32, x.shape, x.ndim - 1)
    shift = 1
    while shift < n:
        x = x + jnp.where(lane >= shift, pltpu.roll(x, shift, x.ndim - 1), 0)
        shift *= 2
    return x


def _route_kernel(logit_ref, idx_ref, gate_ref, key_s, aff_s, cnt_s, acc_s, *, cap):
    n = logit_ref.shape[0]
    lt = logit_ref[...].T[:N_EXPERTS, :]
    ex = jnp.exp(lt - jnp.max(lt, axis=0, keepdims=True))
    aff = ex / jnp.sum(ex, axis=0, keepdims=True)
    bits = pltpu.bitcast(aff, jnp.int32)

    def bit_step(i, thr):
        trial = thr | jnp.left_shift(jnp.int32(1), 30 - i)
        cnt = jnp.sum((bits >= trial).astype(jnp.int32), axis=1, keepdims=True)
        return jnp.where(cnt >= cap, trial, thr)

    thr = lax.fori_loop(0, 31, bit_step, jnp.zeros((N_EXPERTS, 1), jnp.int32))
    gt = bits > thr
    eq = (bits == thr).astype(jnp.int32)
    need = cap - jnp.sum(gt.astype(jnp.int32), axis=1, keepdims=True)
    sel = jnp.where(gt, 1, jnp.where((_lane_prefix_sum(eq) - eq) < need, eq, 0))
    key = jnp.where(sel > 0, _lane_prefix_sum(sel) - sel, -1)

    chunk = key_s.shape[2]
    n_chunks = n // chunk
    st = min(ROUTE_SLOT_TILE, cap)
    count = _lane_prefix_sum(sel)
    for c in range(n_chunks):
        lanes = slice(c * chunk, (c + 1) * chunk)
        key_s[c] = key[:, lanes]
        aff_s[c] = aff[:, lanes]
        cnt_s[c] = count[:, (c + 1) * chunk - LANES:(c + 1) * chunk]
    acc_s[...] = jnp.zeros(acc_s.shape, F32)
    exact = lambda v: v.astype(BF16).astype(F32)
    slot0 = lax.broadcasted_iota(jnp.int32, (st, chunk), 0)
    lane = lax.broadcasted_iota(jnp.int32, (1, chunk), 1)

    def per_expert(e, carry):
        def per_chunk(c, lo):
            hi = cnt_s[c, pl.ds(e, 1), LANES - 1:LANES][0, 0]
            a = aff_s[c, pl.ds(e, 1), :]
            a_hi = exact(a)
            a_mid = exact(a - a_hi)
            a_lo = exact(a - a_hi - a_mid)
            tok = lane + c * chunk
            rows = jnp.concatenate([(tok // LANES).astype(F32), (tok % LANES).astype(F32), a_hi, a_mid, a_lo,
                                    jnp.zeros((11, chunk), F32)], axis=0).astype(BF16)
            k = key_s[c, pl.ds(e, 1), :]

            def per_tile(t, carry2):
                r0 = pl.multiple_of(t * st, st)
                onehot = jnp.where(k == slot0 + r0, 1.0, 0.0).astype(BF16)
                acc_s[e, pl.ds(r0, st), 0:16] += _dot_nt(onehot, rows)
                return carry2

            lax.fori_loop(lo // st, jnp.where(hi > lo, (hi - 1) // st + 1, lo // st), per_tile, 0)
            return hi

        lax.fori_loop(0, n_chunks, per_chunk, jnp.int32(0))
        acc_t = acc_s[e].T
        idx_ref[pl.ds(e, 1), :] = (acc_t[0:1] * LANES + acc_t[1:2]).astype(jnp.int32)
        gate_ref[pl.ds(e, 1), :] = acc_t[2:3] + acc_t[3:4] + acc_t[4:5]
        return carry

    lax.fori_loop(0, N_EXPERTS, per_expert, 0)


ROUTE_SLOT_TILE = 256


def route(logits, b, n, cap, name="moe_route"):
    kern = functools.partial(_route_kernel, cap=cap)
    chunk = min(ROUTE_CHUNK, n)
    assert n % chunk == 0 and cap % min(ROUTE_SLOT_TILE, cap) == 0
    return pl.pallas_call(
        kern, grid=(b,),
        in_specs=[pl.BlockSpec((n, LANES), lambda i: (i, 0))],
        out_specs=[pl.BlockSpec((None, N_EXPERTS, cap), lambda i: (i, 0, 0)),
                   pl.BlockSpec((None, N_EXPERTS, cap), lambda i: (i, 0, 0))],
        out_shape=[jax.ShapeDtypeStruct((b, N_EXPERTS, cap), jnp.int32),
                   jax.ShapeDtypeStruct((b, N_EXPERTS, cap), F32)],
        scratch_shapes=[pltpu.VMEM((n // chunk, N_EXPERTS, chunk), jnp.int32),
                        pltpu.VMEM((n // chunk, N_EXPERTS, chunk), F32),
                        pltpu.VMEM((n // chunk, N_EXPERTS, LANES), jnp.int32),
                        pltpu.VMEM((N_EXPERTS, cap, LANES), F32)],
        name=name, compiler_params=_compiler_params(("parallel",)))(logits)


COMBINE_ROWS = 256
COMBINE_GROUP = 8


def _combine_kernel(idx_ref, ys_ref, o_ref, *, cap):
    bb, e, j = pl.program_id(0), pl.program_id(1), pl.program_id(2)
    rows = ys_ref.shape[0]

    @pl.when((e == 0) & (j == 0))
    def _():
        o_ref[...] = jnp.zeros(o_ref.shape, F32)

    base = (bb * pl.num_programs(1) + e) * cap + j * rows

    def group(g, carry):
        r = g * COMBINE_GROUP
        ids = [idx_ref[base + r + u] for u in range(COMBINE_GROUP)]
        old = [o_ref[ids[u]] for u in range(COMBINE_GROUP)]
        for u in range(COMBINE_GROUP):
            o_ref[ids[u]] = old[u] + ys_ref[r + u]
        return carry

    lax.fori_loop(0, rows // COMBINE_GROUP, group, 0)


def combine(ys, idx, n, name="moe_combine"):
    e, r, d = ys.shape
    b, _, cap = idx.shape
    sub = d // LANES
    rows = min(COMBINE_ROWS, cap)
    assert cap % rows == 0 and rows % COMBINE_GROUP == 0
    tiles = cap // rows
    grid_spec = pltpu.PrefetchScalarGridSpec(
        num_scalar_prefetch=1, grid=(b, e, tiles),
        in_specs=[pl.BlockSpec((None, rows, sub, LANES), lambda i, k, j, idx_ref: (k, i * tiles + j, 0, 0))],
        out_specs=pl.BlockSpec((None, n, sub, LANES), lambda i, k, j, idx_ref: (i, 0, 0, 0),
                               pipeline_mode=pl.Buffered(1)))
    out = pl.pallas_call(
        functools.partial(_combine_kernel, cap=cap), grid_spec=grid_spec,
        out_shape=jax.ShapeDtypeStruct((b, n, sub, LANES), F32), name=name,
        compiler_params=_compiler_params(("arbitrary", "arbitrary", "arbitrary")))(
            idx.reshape(-1), ys.reshape(e, r, sub, LANES))
    return out.reshape(b, n, d)


def _pad_cols(w, n):
    return jnp.pad(w, ((0, 0), (0, n - w.shape[1])))


def _ssd_project(x2d, g, shift, scale, rows_per_batch, w_main, w_dt, conv_w, conv_b, dt_bias, b, l):
    d_inner = SSM_HEADS * SSM_HEAD_DIM
    main = fused_matmul(x2d, w_main, prologue="norm_mod", g=g, shift=shift, scale=scale,
                        rows_per_batch=rows_per_batch, name="ssd_in_proj")
    dt_raw = fused_matmul(x2d, w_dt, prologue="norm_mod", g=g, shift=shift, scale=scale,
                          rows_per_batch=rows_per_batch, precise=True, name="ssd_dt_proj")
    xbc = conv_silu(main.reshape(b, l, -1), d_inner, conv_w, conv_b)
    dt = jax.nn.softplus(dt_raw[:, :2 * SSM_HEADS].reshape(b, l, 2, SSM_HEADS) + dt_bias)
    return main, xbc, dt


def _ssd_bidirectional(xbc, dt, a, d_skip, h0_f, h0_b):
    dt_f, dt_b = dt[:, :, 0], dt[:, :, 1]
    y_f, hf = ssd_scan(xbc, dt_f, jnp.swapaxes(dt_f, 1, 2), a[0], h0_f, reverse=False, name="ssd_scan_fwd")
    y, hb = ssd_scan(xbc, dt_b, jnp.swapaxes(dt_b, 1, 2), a[1], h0_b, reverse=True, y_prev=y_f, d_skip=d_skip,
                     name="ssd_scan_bwd")
    return y, hf, hb


def _moe(x2d, g, shift, scale, gate_mod, rows_per_batch, b, n, router_w, w1, w3, w2):
    d = x2d.shape[1]
    cap = EC_CAPACITY_FACTOR * n // N_EXPERTS
    logits, h = fused_matmul(x2d, _pad_cols(router_w, LANES), prologue="norm_mod", g=g, shift=shift, scale=scale,
                             rows_per_batch=rows_per_batch, precise=True, emit_h=BF16, name="moe_router")
    bi = jnp.arange(b)[:, None, None]
    h3 = h.reshape(b, n, d)
    in_pallas = n % ROUTE_CHUNK == 0
    if in_pallas:
        idx, gate = route(logits, b, n, cap)
    else:
        aff = jax.nn.softmax(logits[:, :N_EXPERTS].reshape(b, n, N_EXPERTS), axis=-1)
        gate, idx = lax.top_k(jnp.swapaxes(aff, 1, 2), cap)
    xs = jnp.swapaxes(h3[bi, idx], 0, 1).reshape(N_EXPERTS, b * cap, d)
    gate_e = jnp.swapaxes(gate, 0, 1).reshape(N_EXPERTS, b * cap, 1)
    ys = expert_ffn(xs, w1, w3, w2, gate_e)
    if in_pallas:
        moe = combine(ys, idx, n)
    else:
        ys = jnp.swapaxes(ys.reshape(N_EXPERTS, b, cap, d), 0, 1)
        moe = jnp.zeros((b, n, d), F32).at[bi, idx].add(ys)
    x3 = x2d.reshape(b, n, d)
    gm = gate_mod if gate_mod.shape[0] == b else jnp.broadcast_to(gate_mod, (b, 1, d))
    return (x3 + gm * moe).reshape(b * n, d)


def _dft_tables(l, ch):
    def cs(nn):
        idx = jnp.arange(nn)
        ang = ((idx[:, None] * idx[None, :]) % nn).astype(F32) * (2.0 * math.pi / nn)
        s = 1.0 / math.sqrt(nn)
        return jnp.cos(ang) * s, jnp.sin(ang) * s
    cl, sl = cs(l)
    cc, sc = cs(ch)
    return cl, sl, cc, sc


def _fourier(h2d, b, l, out_w, out_b, gate, res, rows_per_batch):
    d = h2d.shape[1]
    ch = d // FNET_GROUPS
    cl, sl, cc, sc = _dft_tables(l, ch)
    eye = jnp.eye(FNET_GROUPS, dtype=F32)
    w_ch = jnp.concatenate([jnp.kron(eye, cc), jnp.kron(eye, sc)], axis=1).astype(BF16)
    pq = fused_matmul(h2d, w_ch, out_dtype=BF16, name="fnet_channel_dft")
    pq = pq.reshape(b, l, 2, d)
    rhs = jnp.transpose(pq, (2, 1, 0, 3)).reshape(2 * l, b * d)
    lhs = jnp.concatenate([cl, -sl], axis=1).astype(BF16)
    mixed = fused_matmul(lhs, rhs, out_dtype=BF16, tk=2048, name="fnet_position_dft")
    mixed = jnp.transpose(mixed.reshape(l, b, d), (1, 0, 2)).reshape(b * l, d)
    return fused_matmul(mixed, out_w.astype(BF16), bias=out_b, gate=gate, res=res,
                        rows_per_batch=rows_per_batch, name="fnet_out_proj")


FFT_N1 = 32
FFT_ROWS = 8
FFT_CH = 128


def _fnet_kernel(x_ref, cc_ref, sc_ref, kc_ref, ks_ref, mc_ref, ms_ref, o_ref, zr_s, zq_s, ar_s, ai_s):
    n1, nj, rows, ch = zr_s.shape
    n2 = nj * rows
    x = x_ref[...]
    zr_s[...] = _dot(x, cc_ref[...]).reshape(zr_s.shape)
    zq_s[...] = _dot(x, sc_ref[...]).reshape(zq_s.shape)
    kc, ks = kc_ref[...], ks_ref[...]

    def stage1(j, carry):
        p = zr_s[:, j].reshape(n1 * rows, ch).astype(BF16)
        q = zq_s[:, j].reshape(n1 * rows, ch).astype(BF16)
        ar_s[:, j] = (_dot(kc, p) - _dot(ks, q)).reshape(n1, rows, ch)
        ai_s[:, j] = (_dot(kc, q) + _dot(ks, p)).reshape(n1, rows, ch)
        return carry

    lax.fori_loop(0, nj, stage1, 0)

    def stage2(k1, carry):
        ar = ar_s[k1].reshape(n2, ch).astype(BF16)
        ai = ai_s[k1].reshape(n2, ch).astype(BF16)
        o_ref[k1] = (_dot(mc_ref[k1], ar) - _dot(ms_ref[k1], ai)).astype(o_ref.dtype)
        return carry

    lax.fori_loop(0, n1, stage2, 0)


def _fnet_tables(l, ch):
    n1, n2 = FFT_N1, l // FFT_N1
    def cs(num, den, scale):
        ang = (num % den).astype(F32) * (2.0 * math.pi / den)
        return jnp.cos(ang) * scale, jnp.sin(ang) * scale
    ic = jnp.arange(ch)
    cc, sc = cs(ic[:, None] * ic[None, :], ch, 1.0 / math.sqrt(ch))
    i1 = jnp.arange(n1)
    c1, s1 = cs(i1[:, None] * i1[None, :], n1, 1.0)
    eye = jnp.eye(FFT_ROWS, dtype=F32)
    kc, ks = jnp.kron(c1, eye), jnp.kron(s1, eye)
    k = i1[:, None, None] + n1 * jnp.arange(n2)[None, :, None]
    mc, ms = cs(k * jnp.arange(n2)[None, None, :], l, 1.0 / math.sqrt(l))
    return [t.astype(BF16) for t in (cc, sc, kc, ks, mc, ms)]


def fourier_mix(h, name="fnet_mix"):
    b, l, d = h.shape
    ch = d // FNET_GROUPS
    n1, n2 = FFT_N1, l // FFT_N1
    halves = ch // FFT_CH
    cc, sc, kc, ks, mc, ms = _fnet_tables(l, ch)
    once = pl.Buffered(1)
    const2 = lambda i, g, s: (0, 0)
    return pl.pallas_call(
        _fnet_kernel, grid=(b, FNET_GROUPS, halves),
        in_specs=[pl.BlockSpec((None, l, ch), lambda i, g, s: (i, 0, g)),
                  pl.BlockSpec((ch, FFT_CH), lambda i, g, s: (0, s)),
                  pl.BlockSpec((ch, FFT_CH), lambda i, g, s: (0, s)),
                  pl.BlockSpec((n1 * FFT_ROWS, n1 * FFT_ROWS), const2, pipeline_mode=once),
                  pl.BlockSpec((n1 * FFT_ROWS, n1 * FFT_ROWS), const2, pipeline_mode=once),
                  pl.BlockSpec((n1, n2, n2), lambda i, g, s: (0, 0, 0), pipeline_mode=once),
                  pl.BlockSpec((n1, n2, n2), lambda i, g, s: (0, 0, 0), pipeline_mode=once)],
        out_specs=pl.BlockSpec((None, n1, n2, FFT_CH), lambda i, g, s: (i, 0, 0, g * halves + s)),
        out_shape=jax.ShapeDtypeStruct((b, n1, n2, d), BF16),
        scratch_shapes=[pltpu.VMEM((n1, n2 // FFT_ROWS, FFT_ROWS, FFT_CH), F32) for _ in range(4)],
        name=name,
        compiler_params=_compiler_params(("parallel", "arbitrary", "arbitrary")))(h, cc, sc, kc, ks, mc, ms)


def fnet_out_proj(mixed, w, bias, gate, res, name="fnet_out_proj"):
    b, n1, n2, d = mixed.shape
    n = w.shape[1]
    res3 = res.reshape(b, n2, n1 * n)
    kern = functools.partial(_mm_kernel, prologue=None, precise=False, has_bias=True, has_gate_res=True,
                             emit_h=False, nk=1)
    rows = lambda i, j, k: (i // n1, 0, i % n1)
    out = pl.pallas_call(
        kern, grid=(b * n1, 1, 1),
        in_specs=[pl.BlockSpec((None, None, n2, d), lambda i, j, k: (i // n1, i % n1, 0, 0)),
                  pl.BlockSpec((d, n), lambda i, j, k: (0, 0)),
                  pl.BlockSpec((1, n), lambda i, j, k: (0, 0)),
                  pl.BlockSpec((None, 1, n), lambda i, j, k: (i // n1, 0, 0)),
                  pl.BlockSpec((None, n2, n), rows)],
        out_specs=pl.BlockSpec((None, n2, n), rows),
        out_shape=jax.ShapeDtypeStruct((b, n2, n1 * n), F32), name=name,
        compiler_params=_compiler_params(("parallel", "arbitrary", "arbitrary")))(
            mixed, w, bias.reshape(1, n), gate, res3)
    return out.reshape(b * n2 * n1, n)


CONV_HALO = 8


def _conv_kernel(prev_ref, x_ref, next_ref, w_ref, b_ref, o_ref, *, taps):
    i = pl.program_id(1)
    last = pl.num_programs(1) - 1
    tl = x_ref.shape[0]
    prev = jnp.where(i > 0, prev_ref[...], 0.0)
    nxt = jnp.where(i < last, next_ref[...], 0.0)
    ext = jnp.concatenate([prev, x_ref[...], nxt], axis=0)
    acc = b_ref[...]
    for k in range(taps):
        lo = CONV_HALO - taps // 2 + k
        acc = acc + ext[lo:lo + tl, :] * w_ref[k:k + 1, :]
    o_ref[...] = acc * jax.nn.sigmoid(acc)


def conv_silu(main, col0, conv_w, conv_b, *, tl=512, tc=1024, name="ssd_conv"):
    b, l, _ = main.shape
    taps, c = conv_w.shape
    tl = min(tl, l)
    assert l % tl == 0 and c % tc == 0 and col0 % tc == 0 and tl % CONV_HALO == 0
    cb = col0 // tc
    hb = tl // CONV_HALO
    nh = l // CONV_HALO
    kern = functools.partial(_conv_kernel, taps=taps)
    return pl.pallas_call(
        kern, grid=(b, l // tl, c // tc),
        in_specs=[pl.BlockSpec((None, CONV_HALO, tc), lambda n, i, j: (n, jnp.maximum(i * hb - 1, 0), cb + j)),
                  pl.BlockSpec((None, tl, tc), lambda n, i, j: (n, i, cb + j)),
                  pl.BlockSpec((None, CONV_HALO, tc), lambda n, i, j: (n, jnp.minimum((i + 1) * hb, nh - 1), cb + j)),
                  pl.BlockSpec((taps, tc), lambda n, i, j: (0, j)),
                  pl.BlockSpec((1, tc), lambda n, i, j: (0, j))],
        out_specs=pl.BlockSpec((None, tl, tc), lambda n, i, j: (n, i, j)),
        out_shape=jax.ShapeDtypeStruct((b, l, c), F32), name=name,
        compiler_params=_compiler_params(("parallel", "parallel", "parallel")))(
            main, main, main, conv_w, conv_b.reshape(1, c))


def kernel(x, c, ctx, c_ctx, mod_w, mod_b, norm1_g, norm2_g, ssd_in_w, ssd_conv_w, ssd_conv_b, ssd_dt_bias,
           ssd_a_log, ssd_d, ssd_norm_g, ssd_out_w, fnet_out_w, fnet_out_b, diff_qkv_w, diff_out_w, diff_lambda,
           diff_subln_g, router_w, moe_w1, moe_w3, moe_w2, final_g):
    b, l, d = x.shape
    lc = ctx.shape[1]
    d_inner = SSM_HEADS * SSM_HEAD_DIM

    cond = jnp.concatenate([c, c_ctx[None], jnp.zeros(((8 - (b + 1) % 8) % 8, d), F32)], axis=0)
    cond = jax.nn.silu(cond)
    xl = x.reshape(b * l, d)
    xc = ctx.reshape(b * lc, d)

    for i in range(DEPTH):
        need_ctx = i < DEPTH - 1
        kind, j = i % N_MIXERS, i // N_MIXERS
        mod = fused_matmul(cond, mod_w[i], bias=mod_b[i], tn=1024, name="modulation")
        m_l = [mod[:b, t * d:(t + 1) * d].reshape(b, 1, d) for t in range(6)]
        m_c = [mod[b:b + 1, t * d:(t + 1) * d].reshape(1, 1, d) for t in range(6)]

        if kind == 0:
            w_main = ssd_in_w[j][:, :d_inner + d_inner + 2 * SSM_GROUPS * SSM_STATE].astype(BF16)
            w_dt = _pad_cols(ssd_in_w[j][:, w_main.shape[1]:], LANES)
            a = -jnp.exp(ssd_a_log[j].astype(F32))
            h0 = jnp.zeros((b, SSM_GROUPS, SSM_STATE, GROUP_WIDTH), F32)
            z_c, xbc_c, dt_c = _ssd_project(xc, norm1_g[i], m_c[0], m_c[1], None, w_main, w_dt,
                                            ssd_conv_w[j], ssd_conv_b[j], ssd_dt_bias[j], b, lc)
            y_c, hf, hb = _ssd_bidirectional(xbc_c, dt_c, a, ssd_d[j], h0, h0)
            z_l, xbc_l, dt_l = _ssd_project(xl, norm1_g[i], m_l[0], m_l[1], l, w_main, w_dt,
                                            ssd_conv_w[j], ssd_conv_b[j], ssd_dt_bias[j], b, l)
            y_l, _, _ = _ssd_bidirectional(xbc_l, dt_l, a, ssd_d[j], hf, hb)
            w_out = ssd_out_w[j].astype(BF16)
            xl = fused_matmul(y_l.reshape(b * l, d_inner), w_out, prologue="gated_norm",
                              z=z_l, g=ssd_norm_g[j], gate=m_l[2], res=xl,
                              rows_per_batch=l, name="ssd_out_proj")
            if need_ctx:
                xc = fused_matmul(y_c.reshape(b * lc, d_inner), w_out, prologue="gated_norm",
                                  z=z_c, g=ssd_norm_g[j], gate=m_c[2], res=xc,
                                  name="ssd_out_proj_ctx")
        elif kind == 1:
            ident = jnp.eye(d, dtype=BF16)
            _, h_l = fused_matmul(xl, ident[:, :LANES], prologue="norm_mod", g=norm1_g[i], shift=m_l[0],
                                  scale=m_l[1], rows_per_batch=l, emit_h=BF16, name="fnet_norm")
            mixed = fourier_mix(h_l.reshape(b, l, d))
            xl = fnet_out_proj(mixed, fnet_out_w[j].astype(BF16), fnet_out_b[j], m_l[2], xl)
            if need_ctx:
                _, h_c = fused_matmul(xc, ident[:, :LANES], prologue="norm_mod", g=norm1_g[i], shift=m_c[0],
                                      scale=m_c[1], emit_h=BF16, name="fnet_norm_ctx")
                xc = _fourier(h_c, b, lc, fnet_out_w[j], fnet_out_b[j], m_c[2], xc, None)
        else:
            lambda_init = 0.8 - 0.6 * math.exp(-0.3 * i)
            lp = diff_lambda[j].astype(F32)
            lam = jnp.exp(jnp.sum(lp[0] * lp[1])) - jnp.exp(jnp.sum(lp[2] * lp[3])) + lambda_init
            w_qkv = diff_qkv_w[j].astype(BF16)
            w_out = diff_out_w[j].astype(BF16)
            qkv_c = fused_matmul(xc, w_qkv, prologue="norm_mod", g=norm1_g[i], shift=m_c[0], scale=m_c[1],
                                 name="diff_qkv_ctx").reshape(b, lc, 3 * d)
            qkv_l = fused_matmul(xl, w_qkv, prologue="norm_mod", g=norm1_g[i], shift=m_l[0], scale=m_l[1],
                                 rows_per_batch=l, name="diff_qkv").reshape(b, l, 3 * d)
            o_l = diff_attention(qkv_l, qkv_l, qkv_c, lam, diff_subln_g[j], 1.0 - lambda_init, _rope_tables(l),
                                 name="diff_attention", **ATTN_TILES)
            xl = fused_matmul(o_l.reshape(b * l, d), w_out, gate=m_l[2], res=xl, rows_per_batch=l,
                              name="diff_out_proj")
            if need_ctx:
                o_c = diff_attention(qkv_c, None, qkv_c, lam, diff_subln_g[j], 1.0 - lambda_init, None,
                                     name="diff_attention_ctx", **ATTN_TILES_CTX)
                xc = fused_matmul(o_c.reshape(b * lc, d), w_out, gate=m_c[2], res=xc, name="diff_out_proj_ctx")

        w1, w3, w2 = moe_w1[i], moe_w3[i], moe_w2[i]
        xl = _moe(xl, norm2_g[i], m_l[3], m_l[4], m_l[5], l, b, l, router_w[i], w1, w3, w2)
        if need_ctx:
            xc = _moe(xc, norm2_g[i], m_c[3], m_c[4], m_c[5], None, b, lc, router_w[i], w1, w3, w2)

    zero = jnp.zeros((1, 1, d), F32)
    _, out = fused_matmul(xl, jnp.eye(d, dtype=BF16)[:, :LANES], prologue="norm_mod", g=final_g, shift=zero,
                          scale=zero, emit_h=F32, name="final_norm")
    return out.reshape(b, l, d)
```

```python
import functools
import math

import jax
import jax.numpy as jnp
from jax import lax
from jax.experimental import pallas as pl
from jax.experimental.pallas import tpu as pltpu

F32 = jnp.float32
BF16 = jnp.bfloat16

EPS = 1e-6
DEPTH = 4
N_MIXERS = 3
GRID_W = 64
ROPE_THETA = 10000.0

SSM_HEAD_DIM = 64
SSM_HEADS = 32
SSM_GROUPS = 8
SSM_STATE = 128
SSM_CHUNK = 128
SSM_CONV = 5
HEADS_PER_GROUP = SSM_HEADS // SSM_GROUPS
GROUP_WIDTH = HEADS_PER_GROUP * SSM_HEAD_DIM

FNET_GROUPS = 4

DIFF_HEADS = 8
DIFF_HEAD_DIM = 64
DIFF_V_DIM = 2 * DIFF_HEAD_DIM
DIFF_SCALE = DIFF_HEAD_DIM ** -0.5
ROPE_PAIRS = DIFF_HEAD_DIM // 4

N_EXPERTS = 16
EC_CAPACITY_FACTOR = 2

LANES = 128
VMEM_LIMIT_BYTES = 56 * 1024 * 1024


def _compiler_params(semantics):
    return pltpu.CompilerParams(dimension_semantics=semantics, vmem_limit_bytes=VMEM_LIMIT_BYTES)


def _split_bf16(x, parts):
    out = []
    r = x
    for _ in range(parts):
        p = r.astype(BF16)
        out.append(p)
        r = r - p.astype(F32)
    return out


def _dot(a, b):
    return jnp.dot(a, b, preferred_element_type=F32)


def _dot_nt(a, b):
    return lax.dot_general(a, b, (((1,), (1,)), ((), ())), preferred_element_type=F32)


def _dot_exact_rhs(x, m_bf16, parts=3, nt=False):
    f = _dot_nt if nt else _dot
    acc = None
    for p in _split_bf16(x, parts):
        t = f(p, m_bf16)
        acc = t if acc is None else acc + t
    return acc


def _dot_exact_lhs(m_bf16, x, parts=3):
    acc = None
    for p in _split_bf16(x, parts):
        t = _dot(m_bf16, p)
        acc = t if acc is None else acc + t
    return acc


def _mm_kernel(*refs, prologue, precise, has_bias, has_gate_res, emit_h, nk):
    it = iter(refs)
    x_ref = next(it)
    z_ref = g_ref = sh_ref = sc_ref = None
    if prologue == "norm_mod":
        g_ref, sh_ref, sc_ref = next(it), next(it), next(it)
    elif prologue == "gated_norm":
        z_ref, g_ref = next(it), next(it)
    w_ref = next(it)
    b_ref = next(it) if has_bias else None
    gate_ref = res_ref = None
    if has_gate_res:
        gate_ref, res_ref = next(it), next(it)
    o_ref = next(it)
    hout_ref = next(it) if emit_h else None
    h_ref = next(it) if prologue else None
    acc_ref = next(it) if nk > 1 else None

    j = pl.program_id(1)
    k = pl.program_id(2)

    if prologue:
        @pl.when(j == 0)
        def _():
            x = x_ref[...].astype(F32)
            if prologue == "gated_norm":
                z = z_ref[...].astype(F32)
                x = x * (z * jax.nn.sigmoid(z))
            ms = jnp.mean(x * x, axis=-1, keepdims=True)
            y = x * lax.rsqrt(ms + EPS) * g_ref[...]
            if prologue == "norm_mod":
                y = y * (1.0 + sc_ref[...]) + sh_ref[...]
            h_ref[...] = y.astype(h_ref.dtype)
            if emit_h:
                hout_ref[...] = y.astype(hout_ref.dtype)
        lhs = h_ref[...]
    else:
        lhs = x_ref[...]

    w = w_ref[...]
    if precise:
        l_hi, l_lo = _split_bf16(lhs.astype(F32), 2)
        w_hi, w_lo = _split_bf16(w.astype(F32), 2)
        part = _dot(l_hi, w_hi) + (_dot(l_lo, w_hi) + _dot(l_hi, w_lo))
    else:
        part = _dot(lhs.astype(BF16), w.astype(BF16))

    def finish(acc):
        if has_bias:
            acc = acc + b_ref[...]
        if has_gate_res:
            acc = res_ref[...].astype(F32) + gate_ref[...] * acc
        o_ref[...] = acc.astype(o_ref.dtype)

    if nk == 1:
        finish(part)
    else:
        @pl.when(k == 0)
        def _():
            acc_ref[...] = part

        @pl.when(k > 0)
        def _():
            acc_ref[...] += part

        @pl.when(k == nk - 1)
        def _():
            finish(acc_ref[...])


def fused_matmul(x, w, *, prologue=None, g=None, shift=None, scale=None, z=None, bias=None,
                 gate=None, res=None, rows_per_batch=None, precise=False, out_dtype=F32,
                 emit_h=None, tm=512, tn=1024, tk=None, name="fused_matmul"):
    m, kdim = x.shape
    n = w.shape[1]
    rows_per_batch = m if rows_per_batch is None else rows_per_batch
    tm = min(tm, m, rows_per_batch)
    tn = min(tn, n)
    tk = kdim if tk is None else min(tk, kdim)
    assert m % tm == 0 and n % tn == 0 and kdim % tk == 0, (x.shape, w.shape, tm, tn, tk)
    nk = kdim // tk
    assert not (prologue and nk > 1)
    assert rows_per_batch % tm == 0
    tiles_per_batch = rows_per_batch // tm

    def bidx(i):
        return i // tiles_per_batch

    args = [x]
    in_specs = [pl.BlockSpec((tm, tk), lambda i, j, k: (i, k))]
    if prologue == "norm_mod":
        args += [g.reshape(1, kdim), shift, scale]
        in_specs += [pl.BlockSpec((1, kdim), lambda i, j, k: (0, 0)),
                     pl.BlockSpec((None, 1, kdim), lambda i, j, k: (bidx(i), 0, 0)),
                     pl.BlockSpec((None, 1, kdim), lambda i, j, k: (bidx(i), 0, 0))]
    elif prologue == "gated_norm":
        args += [z, g.reshape(1, kdim)]
        in_specs += [pl.BlockSpec((tm, kdim), lambda i, j, k: (i, 0)),
                     pl.BlockSpec((1, kdim), lambda i, j, k: (0, 0))]
    args.append(w)
    in_specs.append(pl.BlockSpec((tk, tn), lambda i, j, k: (k, j)))
    if bias is not None:
        args.append(bias.reshape(1, n))
        in_specs.append(pl.BlockSpec((1, tn), lambda i, j, k: (0, j)))
    if gate is not None:
        args += [gate, res]
        in_specs += [pl.BlockSpec((None, 1, tn), lambda i, j, k: (bidx(i), 0, j)),
                     pl.BlockSpec((tm, tn), lambda i, j, k: (i, j))]

    out_shape = [jax.ShapeDtypeStruct((m, n), out_dtype)]
    out_specs = [pl.BlockSpec((tm, tn), lambda i, j, k: (i, j))]
    if emit_h is not None:
        out_shape.append(jax.ShapeDtypeStruct((m, kdim), emit_h))
        out_specs.append(pl.BlockSpec((tm, kdim), lambda i, j, k: (i, 0)))

    scratch = []
    if prologue:
        scratch.append(pltpu.VMEM((tm, kdim), F32 if precise else BF16))
    if nk > 1:
        scratch.append(pltpu.VMEM((tm, tn), F32))

    kern = functools.partial(_mm_kernel, prologue=prologue, precise=precise, has_bias=bias is not None,
                             has_gate_res=gate is not None, emit_h=emit_h is not None, nk=nk)
    outs = pl.pallas_call(
        kern, grid=(m // tm, n // tn, nk), in_specs=in_specs, out_specs=out_specs, out_shape=out_shape,
        scratch_shapes=scratch, name=name,
        compiler_params=_compiler_params(("parallel", "arbitrary", "arbitrary")))(*args)
    return outs if emit_h is not None else outs[0]


def _ssd_kernel(*refs, reverse, combine, q):
    it = iter(refs)
    xs_ref, bm_ref, cm_ref, dt_ref, dtt_ref, a_ref, at_ref, h0_ref = (next(it) for _ in range(8))
    yprev_ref = dskip_ref = None
    if combine:
        yprev_ref, dskip_ref = next(it), next(it)
    y_ref, hfin_ref, state_ref = next(it), next(it), next(it)

    c = pl.program_id(1)

    @pl.when(c == 0)
    def _():
        state_ref[...] = h0_ref[...]

    ii = lax.broadcasted_iota(jnp.int32, (q, q), 0)
    jj = lax.broadcasted_iota(jnp.int32, (q, q), 1)
    mask = (ii <= jj) if reverse else (ii >= jj)
    tri = jnp.where(mask, 1.0, 0.0).astype(BF16)

    dt = dt_ref[...]
    dta = dt * a_ref[...]
    dta_t = dtt_ref[...] * at_ref[...]
    a_cs = _dot_exact_lhs(tri, dta)
    a_cs_t = _dot_exact_rhs(dta_t, tri, nt=True)
    edge = 0 if reverse else q - 1
    tot = a_cs[edge:edge + 1, :]
    to_end = jnp.exp(tot - a_cs)
    into = jnp.exp(a_cs)
    chunk_decay = jnp.exp(tot)

    hh = lax.broadcasted_iota(jnp.int32, (SSM_HEADS, SSM_HEADS * SSM_HEAD_DIM), 0)
    cc = lax.broadcasted_iota(jnp.int32, (SSM_HEADS, SSM_HEADS * SSM_HEAD_DIM), 1)
    expand = jnp.where(cc // SSM_HEAD_DIM == hh, 1.0, 0.0).astype(BF16)
    dt_x = _dot_exact_rhs(dt, expand, parts=2)
    to_end_x = _dot_exact_rhs(to_end, expand, parts=2)
    into_x = _dot_exact_rhs(into, expand, parts=2)
    decay_x = _dot_exact_rhs(jnp.broadcast_to(chunk_decay, (8, SSM_HEADS)), expand, parts=3)[0:1, :]

    for g in range(SSM_GROUPS):
        c0 = g * GROUP_WIDTH
        xs_g = xs_ref[:, c0:c0 + GROUP_WIDTH]
        bm_g = bm_ref[:, g * SSM_STATE:(g + 1) * SSM_STATE]
        cm_g = cm_ref[:, g * SSM_STATE:(g + 1) * SSM_STATE].astype(BF16)
        xdt_g = xs_g * dt_x[:, c0:c0 + GROUP_WIDTH]
        xdt_b = xdt_g.astype(BF16)
        cb = _dot_nt(cm_g, bm_g.astype(BF16))
        st = state_ref[g]
        y_g = _dot(cm_g, st.astype(BF16)) * into_x[:, c0:c0 + GROUP_WIDTH]
        if combine:
            y_g = y_g + yprev_ref[:, c0:c0 + GROUP_WIDTH] + dskip_ref[:, c0:c0 + GROUP_WIDTH] * xs_g
        y_ref[:, c0:c0 + GROUP_WIDTH] = y_g
        for r in range(HEADS_PER_GROUP):
            h = g * HEADS_PER_GROUP + r
            seg = a_cs[:, h:h + 1] - a_cs_t[h:h + 1, :]
            lmat = (jnp.exp(jnp.where(mask, seg, -1e30)) * cb).astype(BF16)
            lo = c0 + r * SSM_HEAD_DIM
            y_ref[:, lo:lo + SSM_HEAD_DIM] += _dot(lmat, xdt_b[:, r * SSM_HEAD_DIM:(r + 1) * SSM_HEAD_DIM])
        xdtw = (xdt_g * to_end_x[:, c0:c0 + GROUP_WIDTH]).astype(BF16)
        new = _dot(bm_g.T.astype(BF16), xdtw)
        state_ref[g] = st * decay_x[:, c0:c0 + GROUP_WIDTH] + new

    @pl.when(c == pl.num_programs(1) - 1)
    def _():
        hfin_ref[...] = state_ref[...]


def ssd_scan(xbc, dt, dt_t, a, h0, *, reverse, y_prev=None, d_skip=None, name="ssd_scan"):
    b, l, _ = xbc.shape
    q = SSM_CHUNK
    nc = l // q
    d_inner = SSM_HEADS * SSM_HEAD_DIM
    nbc = SSM_GROUPS * SSM_STATE
    combine = y_prev is not None

    def cidx(c):
        return (nc - 1 - c) if reverse else c

    xs_blocks = d_inner // d_inner
    in_specs = [
        pl.BlockSpec((None, q, d_inner), lambda i, c: (i, cidx(c), 0)),
        pl.BlockSpec((None, q, nbc), lambda i, c: (i, cidx(c), d_inner // nbc)),
        pl.BlockSpec((None, q, nbc), lambda i, c: (i, cidx(c), d_inner // nbc + 1)),
        pl.BlockSpec((None, q, SSM_HEADS), lambda i, c: (i, cidx(c), 0)),
        pl.BlockSpec((None, SSM_HEADS, q), lambda i, c: (i, 0, cidx(c))),
        pl.BlockSpec((1, SSM_HEADS), lambda i, c: (0, 0)),
        pl.BlockSpec((SSM_HEADS, 1), lambda i, c: (0, 0)),
        pl.BlockSpec((None, SSM_GROUPS, SSM_STATE, GROUP_WIDTH), lambda i, c: (i, 0, 0, 0)),
    ]
    del xs_blocks
    args = [xbc, xbc, xbc, dt, dt_t, a.reshape(1, SSM_HEADS), a.reshape(SSM_HEADS, 1), h0]
    if combine:
        in_specs += [pl.BlockSpec((None, q, d_inner), lambda i, c: (i, cidx(c), 0)),
                     pl.BlockSpec((1, d_inner), lambda i, c: (0, 0))]
        args += [y_prev, jnp.repeat(d_skip, SSM_HEAD_DIM).reshape(1, d_inner)]
    out_shape = [jax.ShapeDtypeStruct((b, l, d_inner), F32),
                 jax.ShapeDtypeStruct((b, SSM_GROUPS, SSM_STATE, GROUP_WIDTH), F32)]
    out_specs = [pl.BlockSpec((None, q, d_inner), lambda i, c: (i, cidx(c), 0)),
                 pl.BlockSpec((None, SSM_GROUPS, SSM_STATE, GROUP_WIDTH), lambda i, c: (i, 0, 0, 0))]
    kern = functools.partial(_ssd_kernel, reverse=reverse, combine=combine, q=q)
    return pl.pallas_call(
        kern, grid=(b, nc), in_specs=in_specs, out_specs=out_specs, out_shape=out_shape,
        scratch_shapes=[pltpu.VMEM((SSM_GROUPS, SSM_STATE, GROUP_WIDTH), F32)], name=name,
        compiler_params=_compiler_params(("parallel", "arbitrary")))(*args)


def _rope(x, cos, sin_up, sin_dn):
    return (x * cos + pltpu.roll(x, ROPE_PAIRS, 1) * sin_up
            + pltpu.roll(x, LANES - ROPE_PAIRS, 1) * sin_dn)


LOG2E = 1.4426950408889634


def _attn_kernel(*refs, n_lat, n_ctx, tk, rope, out_scale):
    it = iter(refs)
    q_ref = next(it)
    kl_ref = vl_ref = None
    if n_lat:
        kl_ref, vl_ref = next(it), next(it)
    kc_ref, vc_ref = next(it), next(it)
    if rope:
        cq_ref, suq_ref, sdq_ref, ck_ref, suk_ref, sdk_ref = (next(it) for _ in range(6))
    lam_ref, g_ref, o_ref, k_s, vt_s, m_s, l_s, acc_s, s_s = (next(it) for _ in range(9))

    qi = pl.program_id(2)
    tq = q_ref.shape[0]
    n_tiles = (n_lat + n_ctx) // tk

    @pl.when(qi == 0)
    def _():
        if n_lat:
            k = kl_ref[...]
            if rope:
                k = _rope(k, ck_ref[...], suk_ref[...], sdk_ref[...])
            k_s[0:n_lat, :] = k.astype(BF16)
        k_s[n_lat:n_lat + n_ctx, :] = kc_ref[...].astype(BF16)
        for t in range(n_tiles):
            lo, hi = t * tk, (t + 1) * tk
            pieces = []
            if lo < n_lat:
                pieces.append(vl_ref[lo:min(hi, n_lat), :])
            if hi > n_lat:
                pieces.append(vc_ref[max(lo, n_lat) - n_lat:hi - n_lat, :])
            v = pieces[0] if len(pieces) == 1 else jnp.concatenate(pieces, axis=0)
            vt_s[t] = v.T.astype(BF16)

    q = q_ref[...] * (DIFF_SCALE * LOG2E)
    if rope:
        q = _rope(q, cq_ref[...], suq_ref[...], sdq_ref[...])
    qt = q.T
    row = lax.broadcasted_iota(jnp.int32, qt.shape, 0)
    wq = jnp.concatenate([jnp.where(row < DIFF_HEAD_DIM, qt, 0.0),
                          jnp.where(row >= DIFF_HEAD_DIM, qt, 0.0)], axis=1).astype(BF16)

    m_s[...] = jnp.full(m_s.shape, -1e30, F32)
    l_s[...] = jnp.zeros(l_s.shape, F32)
    acc_s[...] = jnp.zeros(acc_s.shape, F32)

    def scores(t):
        off = pl.multiple_of(t * tk, tk)
        return _dot(k_s[pl.ds(off, tk), :], wq)

    def absorb(buf, t):
        s = s_s[buf]
        m_old = m_s[...]
        m_new = jnp.maximum(m_old, jnp.max(s, axis=0, keepdims=True))
        alpha = jnp.exp2(m_old - m_new)
        p = jnp.exp2(s - m_new)
        l_s[...] = alpha * l_s[...] + jnp.sum(p, axis=0, keepdims=True)
        acc_s[...] = alpha * acc_s[...] + _dot(vt_s[t], p.astype(BF16))
        m_s[...] = m_new

    s_s[0] = scores(0)

    def body(i, carry):
        t0 = 2 * i
        s_s[1] = scores(t0 + 1)
        absorb(0, t0)
        s_s[0] = scores(t0 + 2)
        absorb(1, t0 + 1)
        return carry

    lax.fori_loop(0, n_tiles // 2 - 1, body, 0)
    s_s[1] = scores(n_tiles - 1)
    absorb(0, n_tiles - 2)
    absorb(1, n_tiles - 1)

    o_t = acc_s[...] / l_s[...]
    o = (o_t[:, :tq] - lam_ref[0:1, 0:1] * o_t[:, tq:]).T
    ms = jnp.mean(o * o, axis=-1, keepdims=True)
    o_ref[...] = o * lax.rsqrt(ms + EPS) * (g_ref[...] * out_scale)


ATTN_TILES = dict(tq=512, tk=1408)
ATTN_TILES_CTX = dict(tq=256, tk=128)


def diff_attention(qkv_q, qkv_lat, qkv_ctx, lam, subln_g, out_scale, rope_tabs, *, tq, tk, name):
    b, lq, d3 = qkv_q.shape
    d = d3 // 3
    nh = d // DIFF_V_DIM
    n_lat = 0 if qkv_lat is None else qkv_lat.shape[1]
    n_ctx = qkv_ctx.shape[1]
    rope = rope_tabs is not None
    tq = min(tq, lq)
    tk = min(tk, n_lat + n_ctx)
    assert lq % tq == 0 and (n_lat + n_ctx) % (2 * tk) == 0

    args = [qkv_q]
    in_specs = [pl.BlockSpec((None, tq, DIFF_V_DIM), lambda i, h, t: (i, t, h))]
    if n_lat:
        args += [qkv_lat, qkv_lat]
        in_specs += [pl.BlockSpec((None, n_lat, DIFF_V_DIM), lambda i, h, t: (i, 0, nh + h)),
                     pl.BlockSpec((None, n_lat, DIFF_V_DIM), lambda i, h, t: (i, 0, 2 * nh + h))]
    args += [qkv_ctx, qkv_ctx]
    in_specs += [pl.BlockSpec((None, n_ctx, DIFF_V_DIM), lambda i, h, t: (i, 0, nh + h)),
                 pl.BlockSpec((None, n_ctx, DIFF_V_DIM), lambda i, h, t: (i, 0, 2 * nh + h))]
    if rope:
        args += list(rope_tabs) + list(rope_tabs)
        in_specs += [pl.BlockSpec((tq, LANES), lambda i, h, t: (t, 0))] * 3
        in_specs += [pl.BlockSpec((n_lat, LANES), lambda i, h, t: (0, 0), pipeline_mode=pl.Buffered(1))] * 3
    args += [jnp.broadcast_to(lam.astype(F32).reshape(1, 1), (8, LANES)), subln_g.reshape(1, DIFF_V_DIM)]
    in_specs += [pl.BlockSpec((8, LANES), lambda i, h, t: (0, 0)),
                 pl.BlockSpec((1, DIFF_V_DIM), lambda i, h, t: (0, 0))]

    kern = functools.partial(_attn_kernel, n_lat=n_lat, n_ctx=n_ctx, tk=tk, rope=rope, out_scale=out_scale)
    return pl.pallas_call(
        kern, grid=(b, nh, lq // tq), in_specs=in_specs,
        out_specs=pl.BlockSpec((None, tq, DIFF_V_DIM), lambda i, h, t: (i, t, h)),
        out_shape=jax.ShapeDtypeStruct((b, lq, d), F32),
        scratch_shapes=[pltpu.VMEM((n_lat + n_ctx, DIFF_V_DIM), BF16),
                        pltpu.VMEM(((n_lat + n_ctx) // tk, DIFF_V_DIM, tk), BF16),
                        pltpu.VMEM((1, 2 * tq), F32), pltpu.VMEM((1, 2 * tq), F32),
                        pltpu.VMEM((DIFF_V_DIM, 2 * tq), F32),
                        pltpu.VMEM((2, tk, 2 * tq), F32)],
        name=name,
        compiler_params=_compiler_params(("parallel", "parallel", "arbitrary")))(*args)


def _rope_tables(l):
    rows = l // GRID_W
    row = jnp.repeat(jnp.arange(rows), GRID_W).astype(F32)
    col = jnp.tile(jnp.arange(GRID_W), rows).astype(F32)
    inv = ROPE_THETA ** (-jnp.arange(ROPE_PAIRS, dtype=F32) / ROPE_PAIRS)
    lane = jnp.arange(LANES)
    within = lane % DIFF_HEAD_DIM
    axis = within // (2 * ROPE_PAIRS)
    second = (within // ROPE_PAIRS) % 2
    pos = jnp.where(axis[None, :] == 0, row[:, None], col[:, None])
    ang = pos * inv[within % ROPE_PAIRS][None, :]
    cos, sin = jnp.cos(ang), jnp.sin(ang)
    sin_up = jnp.where(second[None, :] == 1, sin, 0.0)
    sin_dn = jnp.where(second[None, :] == 0, -sin, 0.0)
    return cos, sin_up, sin_dn


FFN_F_SPLIT = 2


def _ffn_kernel(x_ref, w1_ref, w3_ref, w2_ref, gate_ref, o_ref, w1_s, w3_s, w2_s):
    @pl.when(pl.program_id(1) == 0)
    def _():
        w1_s[...] = w1_ref[...].astype(BF16)
        w3_s[...] = w3_ref[...].astype(BF16)
        w2_s[...] = w2_ref[...].astype(BF16)

    x = x_ref[...]
    fs = w1_s.shape[1] // FFN_F_SPLIT
    acc = None
    for s in range(FFN_F_SPLIT):
        h1 = _dot(x, w1_s[:, s * fs:(s + 1) * fs])
        h3 = _dot(x, w3_s[:, s * fs:(s + 1) * fs])
        hid = (h1 * jax.nn.sigmoid(h1) * h3).astype(BF16)
        part = _dot(hid, w2_s[s * fs:(s + 1) * fs, :])
        acc = part if acc is None else acc + part
    o_ref[...] = acc * gate_ref[...]


def _ffn_gather_kernel(rows_ref, x_hbm, g_ref, sh_ref, sc_ref, w1_ref, w3_ref, w2_ref, gate_ref, o_ref,
                       xbuf, sem, w1_s, w3_s, w2_s):
    e, t = pl.program_id(0), pl.program_id(1)
    tiles = pl.num_programs(1)
    tm = xbuf.shape[1]
    step = e * tiles + t
    last = pl.num_programs(0) * tiles - 1
    slot = lax.rem(step, 2)

    def gather(step_id, slot_id):
        for i in range(tm):
            row = rows_ref[step_id * tm + i]
            pltpu.make_async_copy(x_hbm.at[pl.ds(row, 1), :], xbuf.at[slot_id, pl.ds(i, 1), :],
                                  sem.at[slot_id]).start()

    @pl.when(step == 0)
    def _():
        gather(0, 0)

    @pl.when(t == 0)
    def _():
        w1_s[...] = w1_ref[...].astype(BF16)
        w3_s[...] = w3_ref[...].astype(BF16)
        w2_s[...] = w2_ref[...].astype(BF16)

    pltpu.make_async_copy(x_hbm.at[pl.ds(0, tm), :], xbuf.at[slot], sem.at[slot]).wait()
    xr = xbuf[slot]
    ms = jnp.mean(xr * xr, axis=-1, keepdims=True)
    x = (xr * lax.rsqrt(ms + EPS) * g_ref[...] * (1.0 + sc_ref[...]) + sh_ref[...]).astype(BF16)

    gather(jnp.where(step < last, step + 1, 0), 1 - slot)

    fs = w1_s.shape[1] // FFN_F_SPLIT
    acc = None
    for s in range(FFN_F_SPLIT):
        h1 = _dot(x, w1_s[:, s * fs:(s + 1) * fs])
        h3 = _dot(x, w3_s[:, s * fs:(s + 1) * fs])
        hid = (h1 * jax.nn.sigmoid(h1) * h3).astype(BF16)
        part = _dot(hid, w2_s[s * fs:(s + 1) * fs, :])
        acc = part if acc is None else acc + part
    o_ref[...] = acc * gate_ref[...]

    @pl.when(step == last)
    def _():
        pltpu.make_async_copy(x_hbm.at[pl.ds(0, tm), :], xbuf.at[1 - slot], sem.at[1 - slot]).wait()


def expert_ffn_gather(x2d, rows, g, shift, scale, w1, w3, w2, gate, *, rows_per_batch, tm=512,
                      name="expert_ffn_gather"):
    d = x2d.shape[1]
    e, _, f = w1.shape
    r = rows.shape[0] // e
    tm = min(tm, r, rows_per_batch)
    assert r % tm == 0 and rows_per_batch % tm == 0 and f % FFN_F_SPLIT == 0
    tiles = r // tm
    tiles_per_batch = rows_per_batch // tm
    nb = shift.shape[0]
    once = pl.Buffered(1)
    bsel = lambda i, t, rows_ref: ((t // tiles_per_batch) % nb, 0, 0)
    grid_spec = pltpu.PrefetchScalarGridSpec(
        num_scalar_prefetch=1, grid=(e, tiles),
        in_specs=[pl.BlockSpec(memory_space=pl.ANY),
                  pl.BlockSpec((1, d), lambda i, t, rows_ref: (0, 0)),
                  pl.BlockSpec((None, 1, d), bsel),
                  pl.BlockSpec((None, 1, d), bsel),
                  pl.BlockSpec((None, d, f), lambda i, t, rows_ref: (i, 0, 0), pipeline_mode=once),
                  pl.BlockSpec((None, d, f), lambda i, t, rows_ref: (i, 0, 0), pipeline_mode=once),
                  pl.BlockSpec((None, f, d), lambda i, t, rows_ref: (i, 0, 0), pipeline_mode=once),
                  pl.BlockSpec((None, tm, 1), lambda i, t, rows_ref: (i, t, 0))],
        out_specs=pl.BlockSpec((None, tm, d), lambda i, t, rows_ref: (i, t, 0)),
        scratch_shapes=[pltpu.VMEM((2, tm, d), F32), pltpu.SemaphoreType.DMA((2,)),
                        pltpu.VMEM((d, f), BF16), pltpu.VMEM((d, f), BF16), pltpu.VMEM((f, d), BF16)])
    return pl.pallas_call(
        _ffn_gather_kernel, grid_spec=grid_spec,
        out_shape=jax.ShapeDtypeStruct((e, r, d), F32), name=name,
        compiler_params=_compiler_params(("arbitrary", "arbitrary")))(
            rows, x2d, g.reshape(1, d), shift, scale, w1, w3, w2, gate)


def expert_ffn(xs, w1, w3, w2, gate, *, tm=512, name="expert_ffn"):
    e, r, d = xs.shape
    f = w1.shape[2]
    tm = min(tm, r)
    assert r % tm == 0 and f % FFN_F_SPLIT == 0
    once = pl.Buffered(1)
    return pl.pallas_call(
        _ffn_kernel, grid=(e, r // tm),
        in_specs=[pl.BlockSpec((None, tm, d), lambda i, t: (i, t, 0)),
                  pl.BlockSpec((None, d, f), lambda i, t: (i, 0, 0), pipeline_mode=once),
                  pl.BlockSpec((None, d, f), lambda i, t: (i, 0, 0), pipeline_mode=once),
                  pl.BlockSpec((None, f, d), lambda i, t: (i, 0, 0), pipeline_mode=once),
                  pl.BlockSpec((None, tm, 1), lambda i, t: (i, t, 0))],
        out_specs=pl.BlockSpec((None, tm, d), lambda i, t: (i, t, 0)),
        out_shape=jax.ShapeDtypeStruct((e, r, d), F32),
        scratch_shapes=[pltpu.VMEM((d, f), BF16), pltpu.VMEM((d, f), BF16), pltpu.VMEM((f, d), BF16)],
        name=name,
        compiler_params=_compiler_params(("arbitrary", "arbitrary")))(xs, w1, w3, w2, gate)


ROUTE_CHUNK = 1024


def _lane_prefix_sum(x):
    n = x.shape[-1]
    lane = lax.broadcasted_iota(jnp.int32, x.shape, x.ndim - 1)
    shift = 1
    while shift < n:
        x = x + jnp.where(lane >= shift, pltpu.roll(x, shift, x.ndim - 1), 0)
        shift *= 2
    return x


def _route_kernel(logit_ref, idx_ref, gate_ref, key_s, aff_s, cnt_s, acc_s, *, cap):
    n = logit_ref.shape[0]
    lt = logit_ref[...].T[:N_EXPERTS, :]
    ex = jnp.exp(lt - jnp.max(lt, axis=0, keepdims=True))
    aff = ex / jnp.sum(ex, axis=0, keepdims=True)
    bits = pltpu.bitcast(aff, jnp.int32)

    def bit_step(i, thr):
        trial = thr | jnp.left_shift(jnp.int32(1), 30 - i)
        cnt = jnp.sum((bits >= trial).astype(jnp.int32), axis=1, keepdims=True)
        return jnp.where(cnt >= cap, trial, thr)

    thr = lax.fori_loop(0, 31, bit_step, jnp.zeros((N_EXPERTS, 1), jnp.int32))
    gt = bits > thr
    eq = (bits == thr).astype(jnp.int32)
    need = cap - jnp.sum(gt.astype(jnp.int32), axis=1, keepdims=True)
    sel = jnp.where(gt, 1, jnp.where((_lane_prefix_sum(eq) - eq) < need, eq, 0))
    key = jnp.where(sel > 0, _lane_prefix_sum(sel) - sel, -1)

    chunk = key_s.shape[2]
    n_chunks = n // chunk
    st = min(ROUTE_SLOT_TILE, cap)
    count = _lane_prefix_sum(sel)
    for c in range(n_chunks):
        lanes = slice(c * chunk, (c + 1) * chunk)
        key_s[c] = key[:, lanes]
        aff_s[c] = aff[:, lanes]
        cnt_s[c] = count[:, (c + 1) * chunk - LANES:(c + 1) * chunk]
    acc_s[...] = jnp.zeros(acc_s.shape, F32)
    exact = lambda v: v.astype(BF16).astype(F32)
    slot0 = lax.broadcasted_iota(jnp.int32, (st, chunk), 0)
    lane = lax.broadcasted_iota(jnp.int32, (1, chunk), 1)

    def per_expert(e, carry):
        def per_chunk(c, lo):
            hi = cnt_s[c, pl.ds(e, 1), LANES - 1:LANES][0, 0]
            a = aff_s[c, pl.ds(e, 1), :]
            a_hi = exact(a)
            a_mid = exact(a - a_hi)
            a_lo = exact(a - a_hi - a_mid)
            tok = lane + c * chunk
            rows = jnp.concatenate([(tok // LANES).astype(F32), (tok % LANES).astype(F32), a_hi, a_mid, a_lo,
                                    jnp.zeros((11, chunk), F32)], axis=0).astype(BF16)
            k = key_s[c, pl.ds(e, 1), :]

            def per_tile(t, carry2):
                r0 = pl.multiple_of(t * st, st)
                onehot = jnp.where(k == slot0 + r0, 1.0, 0.0).astype(BF16)
                acc_s[e, pl.ds(r0, st), 0:16] += _dot_nt(onehot, rows)
                return carry2

            lax.fori_loop(lo // st, jnp.where(hi > lo, (hi - 1) // st + 1, lo // st), per_tile, 0)
            return hi

        lax.fori_loop(0, n_chunks, per_chunk, jnp.int32(0))
        acc_t = acc_s[e].T
        idx_ref[pl.ds(e, 1), :] = (acc_t[0:1] * LANES + acc_t[1:2]).astype(jnp.int32)
        gate_ref[pl.ds(e, 1), :] = acc_t[2:3] + acc_t[3:4] + acc_t[4:5]
        return carry

    lax.fori_loop(0, N_EXPERTS, per_expert, 0)


ROUTE_SLOT_TILE = 256


def route(logits, b, n, cap, name="moe_route"):
    kern = functools.partial(_route_kernel, cap=cap)
    chunk = min(ROUTE_CHUNK, n)
    assert n % chunk == 0 and cap % min(ROUTE_SLOT_TILE, cap) == 0
    return pl.pallas_call(
        kern, grid=(b,),
        in_specs=[pl.BlockSpec((n, LANES), lambda i: (i, 0))],
        out_specs=[pl.BlockSpec((None, N_EXPERTS, cap), lambda i: (i, 0, 0)),
                   pl.BlockSpec((None, N_EXPERTS, cap), lambda i: (i, 0, 0))],
        out_shape=[jax.ShapeDtypeStruct((b, N_EXPERTS, cap), jnp.int32),
                   jax.ShapeDtypeStruct((b, N_EXPERTS, cap), F32)],
        scratch_shapes=[pltpu.VMEM((n // chunk, N_EXPERTS, chunk), jnp.int32),
                        pltpu.VMEM((n // chunk, N_EXPERTS, chunk), F32),
                        pltpu.VMEM((n // chunk, N_EXPERTS, LANES), jnp.int32),
                        pltpu.VMEM((N_EXPERTS, cap, LANES), F32)],
        name=name, compiler_params=_compiler_params(("parallel",)))(logits)


COMBINE_ROWS = 256
COMBINE_GROUP = 8


def _combine_kernel(idx_ref, ys_ref, o_ref, *, cap):
    bb, e, j = pl.program_id(0), pl.program_id(1), pl.program_id(2)
    rows = ys_ref.shape[0]

    @pl.when((e == 0) & (j == 0))
    def _():
        o_ref[...] = jnp.zeros(o_ref.shape, F32)

    base = (bb * pl.num_programs(1) + e) * cap + j * rows

    def group(g, carry):
        r = g * COMBINE_GROUP
        ids = [idx_ref[base + r + u] for u in range(COMBINE_GROUP)]
        old = [o_ref[ids[u]] for u in range(COMBINE_GROUP)]
        for u in range(COMBINE_GROUP):
            o_ref[ids[u]] = old[u] + ys_ref[r + u]
        return carry

    lax.fori_loop(0, rows // COMBINE_GROUP, group, 0)


def combine(ys, idx, n, name="moe_combine"):
    e, r, d = ys.shape
    b, _, cap = idx.shape
    sub = d // LANES
    rows = min(COMBINE_ROWS, cap)
    assert cap % rows == 0 and rows % COMBINE_GROUP == 0
    tiles = cap // rows
    grid_spec = pltpu.PrefetchScalarGridSpec(
        num_scalar_prefetch=1, grid=(b, e, tiles),
        in_specs=[pl.BlockSpec((None, rows, sub, LANES), lambda i, k, j, idx_ref: (k, i * tiles + j, 0, 0))],
        out_specs=pl.BlockSpec((None, n, sub, LANES), lambda i, k, j, idx_ref: (i, 0, 0, 0),
                               pipeline_mode=pl.Buffered(1)))
    out = pl.pallas_call(
        functools.partial(_combine_kernel, cap=cap), grid_spec=grid_spec,
        out_shape=jax.ShapeDtypeStruct((b, n, sub, LANES), F32), name=name,
        compiler_params=_compiler_params(("arbitrary", "arbitrary", "arbitrary")))(
            idx.reshape(-1), ys.reshape(e, r, sub, LANES))
    return out.reshape(b, n, d)


def _pad_cols(w, n):
    return jnp.pad(w, ((0, 0), (0, n - w.shape[1])))


def _ssd_project(x2d, g, shift, scale, rows_per_batch, w_main, w_dt, conv_w, conv_b, dt_bias, b, l):
    d_inner = SSM_HEADS * SSM_HEAD_DIM
    main = fused_matmul(x2d, w_main, prologue="norm_mod", g=g, shift=shift, scale=scale,
                        rows_per_batch=rows_per_batch, name="ssd_in_proj")
    dt_raw = fused_matmul(x2d, w_dt, prologue="norm_mod", g=g, shift=shift, scale=scale,
                          rows_per_batch=rows_per_batch, precise=True, name="ssd_dt_proj")
    xbc = conv_silu(main.reshape(b, l, -1), d_inner, conv_w, conv_b)
    dt = jax.nn.softplus(dt_raw[:, :2 * SSM_HEADS].reshape(b, l, 2, SSM_HEADS) + dt_bias)
    return main, xbc, dt


def _ssd_bidirectional(xbc, dt, a, d_skip, h0_f, h0_b):
    dt_f, dt_b = dt[:, :, 0], dt[:, :, 1]
    y_f, hf = ssd_scan(xbc, dt_f, jnp.swapaxes(dt_f, 1, 2), a[0], h0_f, reverse=False, name="ssd_scan_fwd")
    y, hb = ssd_scan(xbc, dt_b, jnp.swapaxes(dt_b, 1, 2), a[1], h0_b, reverse=True, y_prev=y_f, d_skip=d_skip,
                     name="ssd_scan_bwd")
    return y, hf, hb


def _moe(x2d, g, shift, scale, gate_mod, rows_per_batch, b, n, router_w, w1, w3, w2):
    d = x2d.shape[1]
    cap = EC_CAPACITY_FACTOR * n // N_EXPERTS
    in_pallas = n % ROUTE_CHUNK == 0
    router_w = _pad_cols(router_w, LANES)
    bi = jnp.arange(b)[:, None, None]
    if in_pallas:
        logits = fused_matmul(x2d, router_w, prologue="norm_mod", g=g, shift=shift, scale=scale,
                              rows_per_batch=rows_per_batch, precise=True, name="moe_router")
        idx, gate = route(logits, b, n, cap)
        rows = jnp.swapaxes(idx + bi * n, 0, 1).reshape(-1)
        gate_e = jnp.swapaxes(gate, 0, 1).reshape(N_EXPERTS, b * cap, 1)
        ys = expert_ffn_gather(x2d, rows, g, shift, scale, w1, w3, w2, gate_e, rows_per_batch=cap)
        moe = combine(ys, idx, n)
    else:
        logits, h = fused_matmul(x2d, router_w, prologue="norm_mod", g=g, shift=shift, scale=scale,
                                 rows_per_batch=rows_per_batch, precise=True, emit_h=BF16, name="moe_router_ctx")
        aff = jax.nn.softmax(logits[:, :N_EXPERTS].reshape(b, n, N_EXPERTS), axis=-1)
        gate, idx = lax.top_k(jnp.swapaxes(aff, 1, 2), cap)
        xs = jnp.swapaxes(h.reshape(b, n, d)[bi, idx], 0, 1).reshape(N_EXPERTS, b * cap, d)
        gate_e = jnp.swapaxes(gate, 0, 1).reshape(N_EXPERTS, b * cap, 1)
        ys = expert_ffn(xs, w1, w3, w2, gate_e)
        ys = jnp.swapaxes(ys.reshape(N_EXPERTS, b, cap, d), 0, 1)
        moe = jnp.zeros((b, n, d), F32).at[bi, idx].add(ys)
    x3 = x2d.reshape(b, n, d)
    gm = gate_mod if gate_mod.shape[0] == b else jnp.broadcast_to(gate_mod, (b, 1, d))
    return (x3 + gm * moe).reshape(b * n, d)


def _dft_tables(l, ch):
    def cs(nn):
        idx = jnp.arange(nn)
        ang = ((idx[:, None] * idx[None, :]) % nn).astype(F32) * (2.0 * math.pi / nn)
        s = 1.0 / math.sqrt(nn)
        return jnp.cos(ang) * s, jnp.sin(ang) * s
    cl, sl = cs(l)
    cc, sc = cs(ch)
    return cl, sl, cc, sc


def _fourier(h2d, b, l, out_w, out_b, gate, res, rows_per_batch):
    d = h2d.shape[1]
    ch = d // FNET_GROUPS
    cl, sl, cc, sc = _dft_tables(l, ch)
    eye = jnp.eye(FNET_GROUPS, dtype=F32)
    w_ch = jnp.concatenate([jnp.kron(eye, cc), jnp.kron(eye, sc)], axis=1).astype(BF16)
    pq = fused_matmul(h2d, w_ch, out_dtype=BF16, name="fnet_channel_dft")
    pq = pq.reshape(b, l, 2, d)
    rhs = jnp.transpose(pq, (2, 1, 0, 3)).reshape(2 * l, b * d)
    lhs = jnp.concatenate([cl, -sl], axis=1).astype(BF16)
    mixed = fused_matmul(lhs, rhs, out_dtype=BF16, tk=2048, name="fnet_position_dft")
    mixed = jnp.transpose(mixed.reshape(l, b, d), (1, 0, 2)).reshape(b * l, d)
    return fused_matmul(mixed, out_w.astype(BF16), bias=out_b, gate=gate, res=res,
                        rows_per_batch=rows_per_batch, name="fnet_out_proj")


FFT_N1 = 32
FFT_ROWS = 8
FFT_CH = 128


def _fnet_kernel(x_ref, cc_ref, sc_ref, kc_ref, ks_ref, mc_ref, ms_ref, o_ref, zr_s, zq_s, ar_s, ai_s):
    n1, nj, rows, ch = zr_s.shape
    n2 = nj * rows
    x = x_ref[...]
    zr_s[...] = _dot(x, cc_ref[...]).reshape(zr_s.shape)
    zq_s[...] = _dot(x, sc_ref[...]).reshape(zq_s.shape)
    kc, ks = kc_ref[...], ks_ref[...]

    def stage1(j, carry):
        p = zr_s[:, j].reshape(n1 * rows, ch).astype(BF16)
        q = zq_s[:, j].reshape(n1 * rows, ch).astype(BF16)
        ar_s[:, j] = (_dot(kc, p) - _dot(ks, q)).reshape(n1, rows, ch)
        ai_s[:, j] = (_dot(kc, q) + _dot(ks, p)).reshape(n1, rows, ch)
        return carry

    lax.fori_loop(0, nj, stage1, 0)

    def stage2(k1, carry):
        ar = ar_s[k1].reshape(n2, ch).astype(BF16)
        ai = ai_s[k1].reshape(n2, ch).astype(BF16)
        o_ref[k1] = (_dot(mc_ref[k1], ar) - _dot(ms_ref[k1], ai)).astype(o_ref.dtype)
        return carry

    lax.fori_loop(0, n1, stage2, 0)


def _fnet_tables(l, ch):
    n1, n2 = FFT_N1, l // FFT_N1
    def cs(num, den, scale):
        ang = (num % den).astype(F32) * (2.0 * math.pi / den)
        return jnp.cos(ang) * scale, jnp.sin(ang) * scale
    ic = jnp.arange(ch)
    cc, sc = cs(ic[:, None] * ic[None, :], ch, 1.0 / math.sqrt(ch))
    i1 = jnp.arange(n1)
    c1, s1 = cs(i1[:, None] * i1[None, :], n1, 1.0)
    eye = jnp.eye(FFT_ROWS, dtype=F32)
    kc, ks = jnp.kron(c1, eye), jnp.kron(s1, eye)
    k = i1[:, None, None] + n1 * jnp.arange(n2)[None, :, None]
    mc, ms = cs(k * jnp.arange(n2)[None, None, :], l, 1.0 / math.sqrt(l))
    return [t.astype(BF16) for t in (cc, sc, kc, ks, mc, ms)]


def fourier_mix(h, name="fnet_mix"):
    b, l, d = h.shape
    ch = d // FNET_GROUPS
    n1, n2 = FFT_N1, l // FFT_N1
    halves = ch // FFT_CH
    cc, sc, kc, ks, mc, ms = _fnet_tables(l, ch)
    once = pl.Buffered(1)
    const2 = lambda i, g, s: (0, 0)
    return pl.pallas_call(
        _fnet_kernel, grid=(b, FNET_GROUPS, halves),
        in_specs=[pl.BlockSpec((None, l, ch), lambda i, g, s: (i, 0, g)),
                  pl.BlockSpec((ch, FFT_CH), lambda i, g, s: (0, s)),
                  pl.BlockSpec((ch, FFT_CH), lambda i, g, s: (0, s)),
                  pl.BlockSpec((n1 * FFT_ROWS, n1 * FFT_ROWS), const2, pipeline_mode=once),
                  pl.BlockSpec((n1 * FFT_ROWS, n1 * FFT_ROWS), const2, pipeline_mode=once),
                  pl.BlockSpec((n1, n2, n2), lambda i, g, s: (0, 0, 0), pipeline_mode=once),
                  pl.BlockSpec((n1, n2, n2), lambda i, g, s: (0, 0, 0), pipeline_mode=once)],
        out_specs=pl.BlockSpec((None, n1, n2, FFT_CH), lambda i, g, s: (i, 0, 0, g * halves + s)),
        out_shape=jax.ShapeDtypeStruct((b, n1, n2, d), BF16),
        scratch_shapes=[pltpu.VMEM((n1, n2 // FFT_ROWS, FFT_ROWS, FFT_CH), F32) for _ in range(4)],
        name=name,
        compiler_params=_compiler_params(("parallel", "arbitrary", "arbitrary")))(h, cc, sc, kc, ks, mc, ms)


def fnet_out_proj(mixed, w, bias, gate, res, name="fnet_out_proj"):
    b, n1, n2, d = mixed.shape
    n = w.shape[1]
    res3 = res.reshape(b, n2, n1 * n)
    kern = functools.partial(_mm_kernel, prologue=None, precise=False, has_bias=True, has_gate_res=True,
                             emit_h=False, nk=1)
    rows = lambda i, j, k: (i // n1, 0, i % n1)
    out = pl.pallas_call(
        kern, grid=(b * n1, 1, 1),
        in_specs=[pl.BlockSpec((None, None, n2, d), lambda i, j, k: (i // n1, i % n1, 0, 0)),
                  pl.BlockSpec((d, n), lambda i, j, k: (0, 0)),
                  pl.BlockSpec((1, n), lambda i, j, k: (0, 0)),
                  pl.BlockSpec((None, 1, n), lambda i, j, k: (i // n1, 0, 0)),
                  pl.BlockSpec((None, n2, n), rows)],
        out_specs=pl.BlockSpec((None, n2, n), rows),
        out_shape=jax.ShapeDtypeStruct((b, n2, n1 * n), F32), name=name,
        compiler_params=_compiler_params(("parallel", "arbitrary", "arbitrary")))(
            mixed, w, bias.reshape(1, n), gate, res3)
    return out.reshape(b * n2 * n1, n)


CONV_HALO = 8


def _conv_kernel(prev_ref, x_ref, next_ref, w_ref, b_ref, o_ref, *, taps):
    i = pl.program_id(1)
    last = pl.num_programs(1) - 1
    tl = x_ref.shape[0]
    prev = jnp.where(i > 0, prev_ref[...], 0.0)
    nxt = jnp.where(i < last, next_ref[...], 0.0)
    ext = jnp.concatenate([prev, x_ref[...], nxt], axis=0)
    acc = b_ref[...]
    for k in range(taps):
        lo = CONV_HALO - taps // 2 + k
        acc = acc + ext[lo:lo + tl, :] * w_ref[k:k + 1, :]
    o_ref[...] = acc * jax.nn.sigmoid(acc)


def conv_silu(main, col0, conv_w, conv_b, *, tl=512, tc=1024, name="ssd_conv"):
    b, l, _ = main.shape
    taps, c = conv_w.shape
    tl = min(tl, l)
    assert l % tl == 0 and c % tc == 0 and col0 % tc == 0 and tl % CONV_HALO == 0
    cb = col0 // tc
    hb = tl // CONV_HALO
    nh = l // CONV_HALO
    kern = functools.partial(_conv_kernel, taps=taps)
    return pl.pallas_call(
        kern, grid=(b, l // tl, c // tc),
        in_specs=[pl.BlockSpec((None, CONV_HALO, tc), lambda n, i, j: (n, jnp.maximum(i * hb - 1, 0), cb + j)),
                  pl.BlockSpec((None, tl, tc), lambda n, i, j: (n, i, cb + j)),
                  pl.BlockSpec((None, CONV_HALO, tc), lambda n, i, j: (n, jnp.minimum((i + 1) * hb, nh - 1), cb + j)),
                  pl.BlockSpec((taps, tc), lambda n, i, j: (0, j)),
                  pl.BlockSpec((1, tc), lambda n, i, j: (0, j))],
        out_specs=pl.BlockSpec((None, tl, tc), lambda n, i, j: (n, i, j)),
        out_shape=jax.ShapeDtypeStruct((b, l, c), F32), name=name,
        compiler_params=_compiler_params(("parallel", "parallel", "parallel")))(
            main, main, main, conv_w, conv_b.reshape(1, c))


def kernel(x, c, ctx, c_ctx, mod_w, mod_b, norm1_g, norm2_g, ssd_in_w, ssd_conv_w, ssd_conv_b, ssd_dt_bias,
           ssd_a_log, ssd_d, ssd_norm_g, ssd_out_w, fnet_out_w, fnet_out_b, diff_qkv_w, diff_out_w, diff_lambda,
           diff_subln_g, router_w, moe_w1, moe_w3, moe_w2, final_g):
    b, l, d = x.shape
    lc = ctx.shape[1]
    d_inner = SSM_HEADS * SSM_HEAD_DIM

    cond = jnp.concatenate([c, c_ctx[None], jnp.zeros(((8 - (b + 1) % 8) % 8, d), F32)], axis=0)
    cond = jax.nn.silu(cond)
    xl = x.reshape(b * l, d)
    xc = ctx.reshape(b * lc, d)

    for i in range(DEPTH):
        need_ctx = i < DEPTH - 1
        kind, j = i % N_MIXERS, i // N_MIXERS
        mod = fused_matmul(cond, mod_w[i], bias=mod_b[i], tn=1024, name="modulation")
        m_l = [mod[:b, t * d:(t + 1) * d].reshape(b, 1, d) for t in range(6)]
        m_c = [mod[b:b + 1, t * d:(t + 1) * d].reshape(1, 1, d) for t in range(6)]

        if kind == 0:
            w_main = ssd_in_w[j][:, :d_inner + d_inner + 2 * SSM_GROUPS * SSM_STATE].astype(BF16)
            w_dt = _pad_cols(ssd_in_w[j][:, w_main.shape[1]:], LANES)
            a = -jnp.exp(ssd_a_log[j].astype(F32))
            h0 = jnp.zeros((b, SSM_GROUPS, SSM_STATE, GROUP_WIDTH), F32)
            z_c, xbc_c, dt_c = _ssd_project(xc, norm1_g[i], m_c[0], m_c[1], None, w_main, w_dt,
                                            ssd_conv_w[j], ssd_conv_b[j], ssd_dt_bias[j], b, lc)
            y_c, hf, hb = _ssd_bidirectional(xbc_c, dt_c, a, ssd_d[j], h0, h0)
            z_l, xbc_l, dt_l = _ssd_project(xl, norm1_g[i], m_l[0], m_l[1], l, w_main, w_dt,
                                            ssd_conv_w[j], ssd_conv_b[j], ssd_dt_bias[j], b, l)
            y_l, _, _ = _ssd_bidirectional(xbc_l, dt_l, a, ssd_d[j], hf, hb)
            w_out = ssd_out_w[j].astype(BF16)
            xl = fused_matmul(y_l.reshape(b * l, d_inner), w_out, prologue="gated_norm",
                              z=z_l, g=ssd_norm_g[j], gate=m_l[2], res=xl,
                              rows_per_batch=l, name="ssd_out_proj")
            if need_ctx:
                xc = fused_matmul(y_c.reshape(b * lc, d_inner), w_out, prologue="gated_norm",
                                  z=z_c, g=ssd_norm_g[j], gate=m_c[2], res=xc,
                                  name="ssd_out_proj_ctx")
        elif kind == 1:
            ident = jnp.eye(d, dtype=BF16)
            _, h_l = fused_matmul(xl, ident[:, :LANES], prologue="norm_mod", g=norm1_g[i], shift=m_l[0],
                                  scale=m_l[1], rows_per_batch=l, emit_h=BF16, name="fnet_norm")
            mixed = fourier_mix(h_l.reshape(b, l, d))
            xl = fnet_out_proj(mixed, fnet_out_w[j].astype(BF16), fnet_out_b[j], m_l[2], xl)
            if need_ctx:
                _, h_c = fused_matmul(xc, ident[:, :LANES], prologue="norm_mod", g=norm1_g[i], shift=m_c[0],
                                      scale=m_c[1], emit_h=BF16, name="fnet_norm_ctx")
                xc = _fourier(h_c, b, lc, fnet_out_w[j], fnet_out_b[j], m_c[2], xc, None)
        else:
            lambda_init = 0.8 - 0.6 * math.exp(-0.3 * i)
            lp = diff_lambda[j].astype(F32)
            lam = jnp.exp(jnp.sum(lp[0] * lp[1])) - jnp.exp(jnp.sum(lp[2] * lp[3])) + lambda_init
            w_qkv = diff_qkv_w[j].astype(BF16)
            w_out = diff_out_w[j].astype(BF16)
            qkv_c = fused_matmul(xc, w_qkv, prologue="norm_mod", g=norm1_g[i], shift=m_c[0], scale=m_c[1],
                                 name="diff_qkv_ctx").reshape(b, lc, 3 * d)
            qkv_l = fused_matmul(xl, w_qkv, prologue="norm_mod", g=norm1_g[i], shift=m_l[0], scale=m_l[1],
                                 rows_per_batch=l, name="diff_qkv").reshape(b, l, 3 * d)
            o_l = diff_attention(qkv_l, qkv_l, qkv_c, lam, diff_subln_g[j], 1.0 - lambda_init, _rope_tables(l),
                                 name="diff_attention", **ATTN_TILES)
            xl = fused_matmul(o_l.reshape(b * l, d), w_out, gate=m_l[2], res=xl, rows_per_batch=l,
                              name="diff_out_proj")
            if need_ctx:
                o_c = diff_attention(qkv_c, None, qkv_c, lam, diff_subln_g[j], 1.0 - lambda_init, None,
                                     name="diff_attention_ctx", **ATTN_TILES_CTX)
                xc = fused_matmul(o_c.reshape(b * lc, d), w_out, gate=m_c[2], res=xc, name="diff_out_proj_ctx")

        w1, w3, w2 = moe_w1[i], moe_w3[i], moe_w2[i]
        xl = _moe(xl, norm2_g[i], m_l[3], m_l[4], m_l[5], l, b, l, router_w[i], w1, w3, w2)
        if need_ctx:
            xc = _moe(xc, norm2_g[i], m_c[3], m_c[4], m_c[5], None, b, lc, router_w[i], w1, w3, w2)

    zero = jnp.zeros((1, 1, d), F32)
    _, out = fused_matmul(xl, jnp.eye(d, dtype=BF16)[:, :LANES], prologue="norm_mod", g=final_g, shift=zero,
                          scale=zero, emit_h=F32, name="final_norm")
    return out.reshape(b, l, d)
```

```python
import functools
import math

import jax
import jax.numpy as jnp
from jax import lax
from jax.experimental import pallas as pl
from jax.experimental.pallas import tpu as pltpu

F32 = jnp.float32
BF16 = jnp.bfloat16

EPS = 1e-6
DEPTH = 4
N_MIXERS = 3
GRID_W = 64
ROPE_THETA = 10000.0

SSM_HEAD_DIM = 64
SSM_HEADS = 32
SSM_GROUPS = 8
SSM_STATE = 128
SSM_CHUNK = 128
SSM_CONV = 5
HEADS_PER_GROUP = SSM_HEADS // SSM_GROUPS
GROUP_WIDTH = HEADS_PER_GROUP * SSM_HEAD_DIM

FNET_GROUPS = 4

DIFF_HEADS = 8
DIFF_HEAD_DIM = 64
DIFF_V_DIM = 2 * DIFF_HEAD_DIM
DIFF_SCALE = DIFF_HEAD_DIM ** -0.5
ROPE_PAIRS = DIFF_HEAD_DIM // 4

N_EXPERTS = 16
EC_CAPACITY_FACTOR = 2

LANES = 128
VMEM_LIMIT_BYTES = 56 * 1024 * 1024


def _compiler_params(semantics):
    return pltpu.CompilerParams(dimension_semantics=semantics, vmem_limit_bytes=VMEM_LIMIT_BYTES)


def _split_bf16(x, parts):
    out = []
    r = x
    for _ in range(parts):
        p = r.astype(BF16)
        out.append(p)
        r = r - p.astype(F32)
    return out


def _dot(a, b):
    return jnp.dot(a, b, preferred_element_type=F32)


def _dot_nt(a, b):
    return lax.dot_general(a, b, (((1,), (1,)), ((), ())), preferred_element_type=F32)


def _dot_exact_rhs(x, m_bf16, parts=3, nt=False):
    f = _dot_nt if nt else _dot
    acc = None
    for p in _split_bf16(x, parts):
        t = f(p, m_bf16)
        acc = t if acc is None else acc + t
    return acc


def _dot_exact_lhs(m_bf16, x, parts=3):
    acc = None
    for p in _split_bf16(x, parts):
        t = _dot(m_bf16, p)
        acc = t if acc is None else acc + t
    return acc


def _mm_kernel(*refs, prologue, precise, has_bias, has_gate_res, emit_h, nk):
    it = iter(refs)
    x_ref = next(it)
    z_ref = g_ref = sh_ref = sc_ref = None
    if prologue == "norm_mod":
        g_ref, sh_ref, sc_ref = next(it), next(it), next(it)
    elif prologue == "gated_norm":
        z_ref, g_ref = next(it), next(it)
    w_ref = next(it)
    b_ref = next(it) if has_bias else None
    gate_ref = res_ref = None
    if has_gate_res:
        gate_ref, res_ref = next(it), next(it)
    o_ref = next(it)
    hout_ref = next(it) if emit_h else None
    h_ref = next(it) if prologue else None
    acc_ref = next(it) if nk > 1 else None

    j = pl.program_id(1)
    k = pl.program_id(2)

    if prologue:
        @pl.when(j == 0)
        def _():
            x = x_ref[...].astype(F32)
            if prologue == "gated_norm":
                z = z_ref[...].astype(F32)
                x = x * (z * jax.nn.sigmoid(z))
            ms = jnp.mean(x * x, axis=-1, keepdims=True)
            y = x * lax.rsqrt(ms + EPS) * g_ref[...]
            if prologue == "norm_mod":
                y = y * (1.0 + sc_ref[...]) + sh_ref[...]
            h_ref[...] = y.astype(h_ref.dtype)
            if emit_h:
                hout_ref[...] = y.astype(hout_ref.dtype)
        lhs = h_ref[...]
    else:
        lhs = x_ref[...]

    w = w_ref[...]
    if precise:
        l_hi, l_lo = _split_bf16(lhs.astype(F32), 2)
        w_hi, w_lo = _split_bf16(w.astype(F32), 2)
        part = _dot(l_hi, w_hi) + (_dot(l_lo, w_hi) + _dot(l_hi, w_lo))
    else:
        part = _dot(lhs.astype(BF16), w.astype(BF16))

    def finish(acc):
        if has_bias:
            acc = acc + b_ref[...]
        if has_gate_res:
            acc = res_ref[...].astype(F32) + gate_ref[...] * acc
        o_ref[...] = acc.astype(o_ref.dtype)

    if nk == 1:
        finish(part)
    else:
        @pl.when(k == 0)
        def _():
            acc_ref[...] = part

        @pl.when(k > 0)
        def _():
            acc_ref[...] += part

        @pl.when(k == nk - 1)
        def _():
            finish(acc_ref[...])


def fused_matmul(x, w, *, prologue=None, g=None, shift=None, scale=None, z=None, bias=None,
                 gate=None, res=None, rows_per_batch=None, precise=False, out_dtype=F32,
                 emit_h=None, tm=512, tn=1024, tk=None, name="fused_matmul"):
    m, kdim = x.shape
    n = w.shape[1]
    rows_per_batch = m if rows_per_batch is None else rows_per_batch
    tm = min(tm, m, rows_per_batch)
    tn = min(tn, n)
    tk = kdim if tk is None else min(tk, kdim)
    assert m % tm == 0 and n % tn == 0 and kdim % tk == 0, (x.shape, w.shape, tm, tn, tk)
    nk = kdim // tk
    assert not (prologue and nk > 1)
    assert rows_per_batch % tm == 0
    tiles_per_batch = rows_per_batch // tm

    def bidx(i):
        return i // tiles_per_batch

    args = [x]
    in_specs = [pl.BlockSpec((tm, tk), lambda i, j, k: (i, k))]
    if prologue == "norm_mod":
        args += [g.reshape(1, kdim), shift, scale]
        in_specs += [pl.BlockSpec((1, kdim), lambda i, j, k: (0, 0)),
                     pl.BlockSpec((None, 1, kdim), lambda i, j, k: (bidx(i), 0, 0)),
                     pl.BlockSpec((None, 1, kdim), lambda i, j, k: (bidx(i), 0, 0))]
    elif prologue == "gated_norm":
        args += [z, g.reshape(1, kdim)]
        in_specs += [pl.BlockSpec((tm, kdim), lambda i, j, k: (i, 0)),
                     pl.BlockSpec((1, kdim), lambda i, j, k: (0, 0))]
    args.append(w)
    in_specs.append(pl.BlockSpec((tk, tn), lambda i, j, k: (k, j)))
    if bias is not None:
        args.append(bias.reshape(1, n))
        in_specs.append(pl.BlockSpec((1, tn), lambda i, j, k: (0, j)))
    if gate is not None:
        args += [gate, res]
        in_specs += [pl.BlockSpec((None, 1, tn), lambda i, j, k: (bidx(i), 0, j)),
                     pl.BlockSpec((tm, tn), lambda i, j, k: (i, j))]

    out_shape = [jax.ShapeDtypeStruct((m, n), out_dtype)]
    out_specs = [pl.BlockSpec((tm, tn), lambda i, j, k: (i, j))]
    if emit_h is not None:
        out_shape.append(jax.ShapeDtypeStruct((m, kdim), emit_h))
        out_specs.append(pl.BlockSpec((tm, kdim), lambda i, j, k: (i, 0)))

    scratch = []
    if prologue:
        scratch.append(pltpu.VMEM((tm, kdim), F32 if precise else BF16))
    if nk > 1:
        scratch.append(pltpu.VMEM((tm, tn), F32))

    kern = functools.partial(_mm_kernel, prologue=prologue, precise=precise, has_bias=bias is not None,
                             has_gate_res=gate is not None, emit_h=emit_h is not None, nk=nk)
    outs = pl.pallas_call(
        kern, grid=(m // tm, n // tn, nk), in_specs=in_specs, out_specs=out_specs, out_shape=out_shape,
        scratch_shapes=scratch, name=name,
        compiler_params=_compiler_params(("parallel", "arbitrary", "arbitrary")))(*args)
    return outs if emit_h is not None else outs[0]


def _ssd_kernel(*refs, reverse, combine, q):
    it = iter(refs)
    xs_ref, bm_ref, cm_ref, dt_ref, dtt_ref, a_ref, at_ref, h0_ref = (next(it) for _ in range(8))
    yprev_ref = dskip_ref = None
    if combine:
        yprev_ref, dskip_ref = next(it), next(it)
    y_ref, hfin_ref, state_ref = next(it), next(it), next(it)

    c = pl.program_id(1)

    @pl.when(c == 0)
    def _():
        state_ref[...] = h0_ref[...]

    ii = lax.broadcasted_iota(jnp.int32, (q, q), 0)
    jj = lax.broadcasted_iota(jnp.int32, (q, q), 1)
    mask = (ii <= jj) if reverse else (ii >= jj)
    tri = jnp.where(mask, 1.0, 0.0).astype(BF16)

    dt = dt_ref[...]
    dta = dt * a_ref[...]
    dta_t = dtt_ref[...] * at_ref[...]
    a_cs = _dot_exact_lhs(tri, dta)
    a_cs_t = _dot_exact_rhs(dta_t, tri, nt=True)
    edge = 0 if reverse else q - 1
    tot = a_cs[edge:edge + 1, :]
    to_end = jnp.exp(tot - a_cs)
    into = jnp.exp(a_cs)
    chunk_decay = jnp.exp(tot)

    hh = lax.broadcasted_iota(jnp.int32, (SSM_HEADS, SSM_HEADS * SSM_HEAD_DIM), 0)
    cc = lax.broadcasted_iota(jnp.int32, (SSM_HEADS, SSM_HEADS * SSM_HEAD_DIM), 1)
    expand = jnp.where(cc // SSM_HEAD_DIM == hh, 1.0, 0.0).astype(BF16)
    dt_x = _dot_exact_rhs(dt, expand, parts=1)
    to_end_x = _dot_exact_rhs(to_end, expand, parts=1)
    into_x = _dot_exact_rhs(into, expand, parts=1)
    decay_x = _dot_exact_rhs(jnp.broadcast_to(chunk_decay, (8, SSM_HEADS)), expand, parts=3)[0:1, :]

    for g in range(SSM_GROUPS):
        c0 = g * GROUP_WIDTH
        xs_g = xs_ref[:, c0:c0 + GROUP_WIDTH]
        bm_g = bm_ref[:, g * SSM_STATE:(g + 1) * SSM_STATE]
        cm_g = cm_ref[:, g * SSM_STATE:(g + 1) * SSM_STATE].astype(BF16)
        xdt_g = xs_g * dt_x[:, c0:c0 + GROUP_WIDTH]
        xdt_b = xdt_g.astype(BF16)
        cb = _dot_nt(cm_g, bm_g.astype(BF16))
        st = state_ref[g]
        y_g = _dot(cm_g, st.astype(BF16)) * into_x[:, c0:c0 + GROUP_WIDTH]
        if combine:
            y_g = y_g + yprev_ref[:, c0:c0 + GROUP_WIDTH] + dskip_ref[:, c0:c0 + GROUP_WIDTH] * xs_g
        y_ref[:, c0:c0 + GROUP_WIDTH] = y_g
        for r in range(HEADS_PER_GROUP):
            h = g * HEADS_PER_GROUP + r
            seg = a_cs[:, h:h + 1] - a_cs_t[h:h + 1, :]
            lmat = (jnp.exp(jnp.where(mask, seg, -1e30)) * cb).astype(BF16)
            lo = c0 + r * SSM_HEAD_DIM
            y_ref[:, lo:lo + SSM_HEAD_DIM] += _dot(lmat, xdt_b[:, r * SSM_HEAD_DIM:(r + 1) * SSM_HEAD_DIM])
        xdtw = (xdt_g * to_end_x[:, c0:c0 + GROUP_WIDTH]).astype(BF16)
        new = _dot(bm_g.T.astype(BF16), xdtw)
        state_ref[g] = st * decay_x[:, c0:c0 + GROUP_WIDTH] + new

    @pl.when(c == pl.num_programs(1) - 1)
    def _():
        hfin_ref[...] = state_ref[...]


def ssd_scan(xbc, dt, dt_t, a, h0, *, reverse, y_prev=None, d_skip=None, name="ssd_scan"):
    b, l, _ = xbc.shape
    q = SSM_CHUNK
    nc = l // q
    d_inner = SSM_HEADS * SSM_HEAD_DIM
    nbc = SSM_GROUPS * SSM_STATE
    combine = y_prev is not None

    def cidx(c):
        return (nc - 1 - c) if reverse else c

    xs_blocks = d_inner // d_inner
    in_specs = [
        pl.BlockSpec((None, q, d_inner), lambda i, c: (i, cidx(c), 0)),
        pl.BlockSpec((None, q, nbc), lambda i, c: (i, cidx(c), d_inner // nbc)),
        pl.BlockSpec((None, q, nbc), lambda i, c: (i, cidx(c), d_inner // nbc + 1)),
        pl.BlockSpec((None, q, SSM_HEADS), lambda i, c: (i, cidx(c), 0)),
        pl.BlockSpec((None, SSM_HEADS, q), lambda i, c: (i, 0, cidx(c))),
        pl.BlockSpec((1, SSM_HEADS), lambda i, c: (0, 0)),
        pl.BlockSpec((SSM_HEADS, 1), lambda i, c: (0, 0)),
        pl.BlockSpec((None, SSM_GROUPS, SSM_STATE, GROUP_WIDTH), lambda i, c: (i, 0, 0, 0)),
    ]
    del xs_blocks
    args = [xbc, xbc, xbc, dt, dt_t, a.reshape(1, SSM_HEADS), a.reshape(SSM_HEADS, 1), h0]
    if combine:
        in_specs += [pl.BlockSpec((None, q, d_inner), lambda i, c: (i, cidx(c), 0)),
                     pl.BlockSpec((1, d_inner), lambda i, c: (0, 0))]
        args += [y_prev, jnp.repeat(d_skip, SSM_HEAD_DIM).reshape(1, d_inner)]
    out_shape = [jax.ShapeDtypeStruct((b, l, d_inner), F32),
                 jax.ShapeDtypeStruct((b, SSM_GROUPS, SSM_STATE, GROUP_WIDTH), F32)]
    out_specs = [pl.BlockSpec((None, q, d_inner), lambda i, c: (i, cidx(c), 0)),
                 pl.BlockSpec((None, SSM_GROUPS, SSM_STATE, GROUP_WIDTH), lambda i, c: (i, 0, 0, 0))]
    kern = functools.partial(_ssd_kernel, reverse=reverse, combine=combine, q=q)
    return pl.pallas_call(
        kern, grid=(b, nc), in_specs=in_specs, out_specs=out_specs, out_shape=out_shape,
        scratch_shapes=[pltpu.VMEM((SSM_GROUPS, SSM_STATE, GROUP_WIDTH), F32)], name=name,
        compiler_params=_compiler_params(("parallel", "arbitrary")))(*args)


def _rope(x, cos, sin_up, sin_dn):
    return (x * cos + pltpu.roll(x, ROPE_PAIRS, 1) * sin_up
            + pltpu.roll(x, LANES - ROPE_PAIRS, 1) * sin_dn)


LOG2E = 1.4426950408889634


def _attn_kernel(*refs, n_lat, n_ctx, tk, rope, out_scale):
    it = iter(refs)
    q_ref = next(it)
    kl_ref = vl_ref = None
    if n_lat:
        kl_ref, vl_ref = next(it), next(it)
    kc_ref, vc_ref = next(it), next(it)
    if rope:
        cq_ref, suq_ref, sdq_ref, ck_ref, suk_ref, sdk_ref = (next(it) for _ in range(6))
    lam_ref, g_ref, o_ref, k_s, vt_s, m_s, l_s, acc_s, s_s = (next(it) for _ in range(9))

    qi = pl.program_id(2)
    tq = q_ref.shape[0]
    n_tiles = (n_lat + n_ctx) // tk

    @pl.when(qi == 0)
    def _():
        if n_lat:
            k = kl_ref[...]
            if rope:
                k = _rope(k, ck_ref[...], suk_ref[...], sdk_ref[...])
            k_s[0:n_lat, :] = k.astype(BF16)
        k_s[n_lat:n_lat + n_ctx, :] = kc_ref[...].astype(BF16)
        for t in range(n_tiles):
            lo, hi = t * tk, (t + 1) * tk
            pieces = []
            if lo < n_lat:
                pieces.append(vl_ref[lo:min(hi, n_lat), :])
            if hi > n_lat:
                pieces.append(vc_ref[max(lo, n_lat) - n_lat:hi - n_lat, :])
            v = pieces[0] if len(pieces) == 1 else jnp.concatenate(pieces, axis=0)
            vt_s[t] = v.T.astype(BF16)

    q = q_ref[...] * (DIFF_SCALE * LOG2E)
    if rope:
        q = _rope(q, cq_ref[...], suq_ref[...], sdq_ref[...])
    qt = q.T
    row = lax.broadcasted_iota(jnp.int32, qt.shape, 0)
    wq = jnp.concatenate([jnp.where(row < DIFF_HEAD_DIM, qt, 0.0),
                          jnp.where(row >= DIFF_HEAD_DIM, qt, 0.0)], axis=1).astype(BF16)

    m_s[...] = jnp.full(m_s.shape, -1e30, F32)
    l_s[...] = jnp.zeros(l_s.shape, F32)
    acc_s[...] = jnp.zeros(acc_s.shape, F32)

    def scores(t):
        off = pl.multiple_of(t * tk, tk)
        return _dot(k_s[pl.ds(off, tk), :], wq)

    def absorb(buf, t):
        s = s_s[buf]
        m_old = m_s[...]
        m_new = jnp.maximum(m_old, jnp.max(s, axis=0, keepdims=True))
        alpha = jnp.exp2(m_old - m_new)
        p = jnp.exp2(s - m_new)
        l_s[...] = alpha * l_s[...] + jnp.sum(p, axis=0, keepdims=True)
        acc_s[...] = alpha * acc_s[...] + _dot(vt_s[t], p.astype(BF16))
        m_s[...] = m_new

    s_s[0] = scores(0)

    def body(i, carry):
        t0 = 2 * i
        s_s[1] = scores(t0 + 1)
        absorb(0, t0)
        s_s[0] = scores(t0 + 2)
        absorb(1, t0 + 1)
        return carry

    lax.fori_loop(0, n_tiles // 2 - 1, body, 0)
    s_s[1] = scores(n_tiles - 1)
    absorb(0, n_tiles - 2)
    absorb(1, n_tiles - 1)

    o_t = acc_s[...] / l_s[...]
    o = (o_t[:, :tq] - lam_ref[0:1, 0:1] * o_t[:, tq:]).T
    ms = jnp.mean(o * o, axis=-1, keepdims=True)
    o_ref[...] = o * lax.rsqrt(ms + EPS) * (g_ref[...] * out_scale)


ATTN_TILES = dict(tq=512, tk=1408)
ATTN_TILES_CTX = dict(tq=256, tk=128)


def diff_attention(qkv_q, qkv_lat, qkv_ctx, lam, subln_g, out_scale, rope_tabs, *, tq, tk, name):
    b, lq, d3 = qkv_q.shape
    d = d3 // 3
    nh = d // DIFF_V_DIM
    n_lat = 0 if qkv_lat is None else qkv_lat.shape[1]
    n_ctx = qkv_ctx.shape[1]
    rope = rope_tabs is not None
    tq = min(tq, lq)
    tk = min(tk, n_lat + n_ctx)
    assert lq % tq == 0 and (n_lat + n_ctx) % (2 * tk) == 0

    args = [qkv_q]
    in_specs = [pl.BlockSpec((None, tq, DIFF_V_DIM), lambda i, h, t: (i, t, h))]
    if n_lat:
        args += [qkv_lat, qkv_lat]
        in_specs += [pl.BlockSpec((None, n_lat, DIFF_V_DIM), lambda i, h, t: (i, 0, nh + h)),
                     pl.BlockSpec((None, n_lat, DIFF_V_DIM), lambda i, h, t: (i, 0, 2 * nh + h))]
    args += [qkv_ctx, qkv_ctx]
    in_specs += [pl.BlockSpec((None, n_ctx, DIFF_V_DIM), lambda i, h, t: (i, 0, nh + h)),
                 pl.BlockSpec((None, n_ctx, DIFF_V_DIM), lambda i, h, t: (i, 0, 2 * nh + h))]
    if rope:
        args += list(rope_tabs) + list(rope_tabs)
        in_specs += [pl.BlockSpec((tq, LANES), lambda i, h, t: (t, 0))] * 3
        in_specs += [pl.BlockSpec((n_lat, LANES), lambda i, h, t: (0, 0), pipeline_mode=pl.Buffered(1))] * 3
    args += [jnp.broadcast_to(lam.astype(F32).reshape(1, 1), (8, LANES)), subln_g.reshape(1, DIFF_V_DIM)]
    in_specs += [pl.BlockSpec((8, LANES), lambda i, h, t: (0, 0)),
                 pl.BlockSpec((1, DIFF_V_DIM), lambda i, h, t: (0, 0))]

    kern = functools.partial(_attn_kernel, n_lat=n_lat, n_ctx=n_ctx, tk=tk, rope=rope, out_scale=out_scale)
    return pl.pallas_call(
        kern, grid=(b, nh, lq // tq), in_specs=in_specs,
        out_specs=pl.BlockSpec((None, tq, DIFF_V_DIM), lambda i, h, t: (i, t, h)),
        out_shape=jax.ShapeDtypeStruct((b, lq, d), F32),
        scratch_shapes=[pltpu.VMEM((n_lat + n_ctx, DIFF_V_DIM), BF16),
                        pltpu.VMEM(((n_lat + n_ctx) // tk, DIFF_V_DIM, tk), BF16),
                        pltpu.VMEM((1, 2 * tq), F32), pltpu.VMEM((1, 2 * tq), F32),
                        pltpu.VMEM((DIFF_V_DIM, 2 * tq), F32),
                        pltpu.VMEM((2, tk, 2 * tq), F32)],
        name=name,
        compiler_params=_compiler_params(("parallel", "parallel", "arbitrary")))(*args)


def _rope_tables(l):
    rows = l // GRID_W
    row = jnp.repeat(jnp.arange(rows), GRID_W).astype(F32)
    col = jnp.tile(jnp.arange(GRID_W), rows).astype(F32)
    inv = ROPE_THETA ** (-jnp.arange(ROPE_PAIRS, dtype=F32) / ROPE_PAIRS)
    lane = jnp.arange(LANES)
    within = lane % DIFF_HEAD_DIM
    axis = within // (2 * ROPE_PAIRS)
    second = (within // ROPE_PAIRS) % 2
    pos = jnp.where(axis[None, :] == 0, row[:, None], col[:, None])
    ang = pos * inv[within % ROPE_PAIRS][None, :]
    cos, sin = jnp.cos(ang), jnp.sin(ang)
    sin_up = jnp.where(second[None, :] == 1, sin, 0.0)
    sin_dn = jnp.where(second[None, :] == 0, -sin, 0.0)
    return cos, sin_up, sin_dn


FFN_F_SPLIT = 2


def _ffn_kernel(x_ref, w1_ref, w3_ref, w2_ref, gate_ref, o_ref, w1_s, w3_s, w2_s):
    @pl.when(pl.program_id(1) == 0)
    def _():
        w1_s[...] = w1_ref[...].astype(BF16)
        w3_s[...] = w3_ref[...].astype(BF16)
        w2_s[...] = w2_ref[...].astype(BF16)

    x = x_ref[...]
    fs = w1_s.shape[1] // FFN_F_SPLIT
    acc = None
    for s in range(FFN_F_SPLIT):
        h1 = _dot(x, w1_s[:, s * fs:(s + 1) * fs])
        h3 = _dot(x, w3_s[:, s * fs:(s + 1) * fs])
        hid = (h1 * jax.nn.sigmoid(h1) * h3).astype(BF16)
        part = _dot(hid, w2_s[s * fs:(s + 1) * fs, :])
        acc = part if acc is None else acc + part
    o_ref[...] = acc * gate_ref[...]


def _ffn_gather_kernel(rows_ref, x_hbm, g_ref, sh_ref, sc_ref, w1_ref, w3_ref, w2_ref, gate_ref, o_ref,
                       xbuf, sem, w1_s, w3_s, w2_s):
    e, t = pl.program_id(0), pl.program_id(1)
    tiles = pl.num_programs(1)
    tm = xbuf.shape[1]
    step = e * tiles + t
    last = pl.num_programs(0) * tiles - 1
    slot = lax.rem(step, 2)

    def gather(step_id, slot_id):
        for i in range(tm):
            row = rows_ref[step_id * tm + i]
            pltpu.make_async_copy(x_hbm.at[pl.ds(row, 1), :], xbuf.at[slot_id, pl.ds(i, 1), :],
                                  sem.at[slot_id]).start()

    @pl.when(step == 0)
    def _():
        gather(0, 0)

    @pl.when(t == 0)
    def _():
        w1_s[...] = w1_ref[...].astype(BF16)
        w3_s[...] = w3_ref[...].astype(BF16)
        w2_s[...] = w2_ref[...].astype(BF16)

    pltpu.make_async_copy(x_hbm.at[pl.ds(0, tm), :], xbuf.at[slot], sem.at[slot]).wait()
    xr = xbuf[slot]
    ms = jnp.mean(xr * xr, axis=-1, keepdims=True)
    x = (xr * lax.rsqrt(ms + EPS) * g_ref[...] * (1.0 + sc_ref[...]) + sh_ref[...]).astype(BF16)

    gather(jnp.where(step < last, step + 1, 0), 1 - slot)

    fs = w1_s.shape[1] // FFN_F_SPLIT
    acc = None
    for s in range(FFN_F_SPLIT):
        h1 = _dot(x, w1_s[:, s * fs:(s + 1) * fs])
        h3 = _dot(x, w3_s[:, s * fs:(s + 1) * fs])
        hid = (h1 * jax.nn.sigmoid(h1) * h3).astype(BF16)
        part = _dot(hid, w2_s[s * fs:(s + 1) * fs, :])
        acc = part if acc is None else acc + part
    o_ref[...] = acc * gate_ref[...]

    @pl.when(step == last)
    def _():
        pltpu.make_async_copy(x_hbm.at[pl.ds(0, tm), :], xbuf.at[1 - slot], sem.at[1 - slot]).wait()


def _with_layer(w):
    return w if isinstance(w, tuple) else (w[None], 0)


def expert_ffn_gather(x2d, rows, g, shift, scale, w1, w3, w2, gate, *, rows_per_batch, tm=512,
                      name="expert_ffn_gather"):
    d = x2d.shape[1]
    (w1, l1), (w3, l3), (w2, l2) = _with_layer(w1), _with_layer(w3), _with_layer(w2)
    _, e, _, f = w1.shape
    r = rows.shape[0] // e
    tm = min(tm, r, rows_per_batch)
    assert r % tm == 0 and rows_per_batch % tm == 0 and f % FFN_F_SPLIT == 0
    tiles = r // tm
    tiles_per_batch = rows_per_batch // tm
    nb = shift.shape[0]
    once = pl.Buffered(1)
    bsel = lambda i, t, rows_ref: ((t // tiles_per_batch) % nb, 0, 0)
    grid_spec = pltpu.PrefetchScalarGridSpec(
        num_scalar_prefetch=1, grid=(e, tiles),
        in_specs=[pl.BlockSpec(memory_space=pl.ANY),
                  pl.BlockSpec((1, d), lambda i, t, rows_ref: (0, 0)),
                  pl.BlockSpec((None, 1, d), bsel),
                  pl.BlockSpec((None, 1, d), bsel),
                  pl.BlockSpec((None, None, d, f), lambda i, t, rows_ref: (l1, i, 0, 0), pipeline_mode=once),
                  pl.BlockSpec((None, None, d, f), lambda i, t, rows_ref: (l3, i, 0, 0), pipeline_mode=once),
                  pl.BlockSpec((None, None, f, d), lambda i, t, rows_ref: (l2, i, 0, 0), pipeline_mode=once),
                  pl.BlockSpec((None, tm, 1), lambda i, t, rows_ref: (i, t, 0))],
        out_specs=pl.BlockSpec((None, tm, d), lambda i, t, rows_ref: (i, t, 0)),
        scratch_shapes=[pltpu.VMEM((2, tm, d), F32), pltpu.SemaphoreType.DMA((2,)),
                        pltpu.VMEM((d, f), BF16), pltpu.VMEM((d, f), BF16), pltpu.VMEM((f, d), BF16)])
    return pl.pallas_call(
        _ffn_gather_kernel, grid_spec=grid_spec,
        out_shape=jax.ShapeDtypeStruct((e, r, d), F32), name=name,
        compiler_params=_compiler_params(("arbitrary", "arbitrary")))(
            rows, x2d, g.reshape(1, d), shift, scale, w1, w3, w2, gate)


def expert_ffn(xs, w1, w3, w2, gate, *, tm=512, name="expert_ffn"):
    e, r, d = xs.shape
    (w1, l1), (w3, l3), (w2, l2) = _with_layer(w1), _with_layer(w3), _with_layer(w2)
    f = w1.shape[3]
    tm = min(tm, r)
    assert r % tm == 0 and f % FFN_F_SPLIT == 0
    once = pl.Buffered(1)
    return pl.pallas_call(
        _ffn_kernel, grid=(e, r // tm),
        in_specs=[pl.BlockSpec((None, tm, d), lambda i, t: (i, t, 0)),
                  pl.BlockSpec((None, None, d, f), lambda i, t: (l1, i, 0, 0), pipeline_mode=once),
                  pl.BlockSpec((None, None, d, f), lambda i, t: (l3, i, 0, 0), pipeline_mode=once),
                  pl.BlockSpec((None, None, f, d), lambda i, t: (l2, i, 0, 0), pipeline_mode=once),
                  pl.BlockSpec((None, tm, 1), lambda i, t: (i, t, 0))],
        out_specs=pl.BlockSpec((None, tm, d), lambda i, t: (i, t, 0)),
        out_shape=jax.ShapeDtypeStruct((e, r, d), F32),
        scratch_shapes=[pltpu.VMEM((d, f), BF16), pltpu.VMEM((d, f), BF16), pltpu.VMEM((f, d), BF16)],
        name=name,
        compiler_params=_compiler_params(("arbitrary", "arbitrary")))(xs, w1, w3, w2, gate)


ROUTE_CHUNK = 1024


def _lane_prefix_sum(x):
    n = x.shape[-1]
    lane = lax.broadcasted_iota(jnp.int32, x.shape, x.ndim - 1)
    shift = 1
    while shift < n:
        x = x + jnp.where(lane >= shift, pltpu.roll(x, shift, x.ndim - 1), 0)
        shift *= 2
    return x


def _route_kernel(logit_ref, idx_ref, gate_ref, key_s, aff_s, cnt_s, acc_s, *, cap):
    n = logit_ref.shape[0]
    lt = logit_ref[...].T[:N_EXPERTS, :]
    ex = jnp.exp(lt - jnp.max(lt, axis=0, keepdims=True))
    aff = ex / jnp.sum(ex, axis=0, keepdims=True)
    bits = pltpu.bitcast(aff, jnp.int32)

    def bit_step(i, thr):
        trial = thr | jnp.left_shift(jnp.int32(1), 30 - i)
        cnt = jnp.sum((bits >= trial).astype(jnp.int32), axis=1, keepdims=True)
        return jnp.where(cnt >= cap, trial, thr)

    thr = lax.fori_loop(0, 31, bit_step, jnp.zeros((N_EXPERTS, 1), jnp.int32))
    gt = bits > thr
    eq = (bits == thr).astype(jnp.int32)
    need = cap - jnp.sum(gt.astype(jnp.int32), axis=1, keepdims=True)
    sel = jnp.where(gt, 1, jnp.where((_lane_prefix_sum(eq) - eq) < need, eq, 0))
    key = jnp.where(sel > 0, _lane_prefix_sum(sel) - sel, -1)

    chunk = key_s.shape[2]
    n_chunks = n // chunk
    st = min(ROUTE_SLOT_TILE, cap)
    count = _lane_prefix_sum(sel)
    for c in range(n_chunks):
        lanes = slice(c * chunk, (c + 1) * chunk)
        key_s[c] = key[:, lanes]
        aff_s[c] = aff[:, lanes]
        cnt_s[c] = count[:, (c + 1) * chunk - LANES:(c + 1) * chunk]
    acc_s[...] = jnp.zeros(acc_s.shape, F32)
    exact = lambda v: v.astype(BF16).astype(F32)
    slot0 = lax.broadcasted_iota(jnp.int32, (st, chunk), 0)
    lane = lax.broadcasted_iota(jnp.int32, (1, chunk), 1)

    def per_expert(e, carry):
        def per_chunk(c, lo):
            hi = cnt_s[c, pl.ds(e, 1), LANES - 1:LANES][0, 0]
            a = aff_s[c, pl.ds(e, 1), :]
            a_hi = exact(a)
            a_mid = exact(a - a_hi)
            a_lo = exact(a - a_hi - a_mid)
            tok = lane + c * chunk
            rows = jnp.concatenate([(tok // LANES).astype(F32), (tok % LANES).astype(F32), a_hi, a_mid, a_lo,
                                    jnp.zeros((11, chunk), F32)], axis=0).astype(BF16)
            k = key_s[c, pl.ds(e, 1), :]

            def per_tile(t, carry2):
                r0 = pl.multiple_of(t * st, st)
                onehot = jnp.where(k == slot0 + r0, 1.0, 0.0).astype(BF16)
                acc_s[e, pl.ds(r0, st), 0:16] += _dot_nt(onehot, rows)
                return carry2

            lax.fori_loop(lo // st, jnp.where(hi > lo, (hi - 1) // st + 1, lo // st), per_tile, 0)
            return hi

        lax.fori_loop(0, n_chunks, per_chunk, jnp.int32(0))
        acc_t = acc_s[e].T
        idx_ref[pl.ds(e, 1), :] = (acc_t[0:1] * LANES + acc_t[1:2]).astype(jnp.int32)
        gate_ref[pl.ds(e, 1), :] = acc_t[2:3] + acc_t[3:4] + acc_t[4:5]
        return carry

    lax.fori_loop(0, N_EXPERTS, per_expert, 0)


ROUTE_SLOT_TILE = 256


def route(logits, b, n, cap, name="moe_route"):
    kern = functools.partial(_route_kernel, cap=cap)
    chunk = min(ROUTE_CHUNK, n)
    assert n % chunk == 0 and cap % min(ROUTE_SLOT_TILE, cap) == 0
    return pl.pallas_call(
        kern, grid=(b,),
        in_specs=[pl.BlockSpec((n, LANES), lambda i: (i, 0))],
        out_specs=[pl.BlockSpec((None, N_EXPERTS, cap), lambda i: (i, 0, 0)),
                   pl.BlockSpec((None, N_EXPERTS, cap), lambda i: (i, 0, 0))],
        out_shape=[jax.ShapeDtypeStruct((b, N_EXPERTS, cap), jnp.int32),
                   jax.ShapeDtypeStruct((b, N_EXPERTS, cap), F32)],
        scratch_shapes=[pltpu.VMEM((n // chunk, N_EXPERTS, chunk), jnp.int32),
                        pltpu.VMEM((n // chunk, N_EXPERTS, chunk), F32),
                        pltpu.VMEM((n // chunk, N_EXPERTS, LANES), jnp.int32),
                        pltpu.VMEM((N_EXPERTS, cap, LANES), F32)],
        name=name, compiler_params=_compiler_params(("parallel",)))(logits)


COMBINE_ROWS = 256
COMBINE_GROUP = 8


def _combine_kernel(idx_ref, ys_ref, o_ref, *, cap):
    bb, e, j = pl.program_id(0), pl.program_id(1), pl.program_id(2)
    rows = ys_ref.shape[0]

    @pl.when((e == 0) & (j == 0))
    def _():
        o_ref[...] = jnp.zeros(o_ref.shape, F32)

    base = (bb * pl.num_programs(1) + e) * cap + j * rows

    def group(g, carry):
        r = g * COMBINE_GROUP
        ids = [idx_ref[base + r + u] for u in range(COMBINE_GROUP)]
        old = [o_ref[ids[u]] for u in range(COMBINE_GROUP)]
        for u in range(COMBINE_GROUP):
            o_ref[ids[u]] = old[u] + ys_ref[r + u]
        return carry

    lax.fori_loop(0, rows // COMBINE_GROUP, group, 0)


def combine(ys, idx, n, name="moe_combine"):
    e, r, d = ys.shape
    b, _, cap = idx.shape
    sub = d // LANES
    rows = min(COMBINE_ROWS, cap)
    assert cap % rows == 0 and rows % COMBINE_GROUP == 0
    tiles = cap // rows
    grid_spec = pltpu.PrefetchScalarGridSpec(
        num_scalar_prefetch=1, grid=(b, e, tiles),
        in_specs=[pl.BlockSpec((None, rows, sub, LANES), lambda i, k, j, idx_ref: (k, i * tiles + j, 0, 0))],
        out_specs=pl.BlockSpec((None, n, sub, LANES), lambda i, k, j, idx_ref: (i, 0, 0, 0),
                               pipeline_mode=pl.Buffered(1)))
    out = pl.pallas_call(
        functools.partial(_combine_kernel, cap=cap), grid_spec=grid_spec,
        out_shape=jax.ShapeDtypeStruct((b, n, sub, LANES), F32), name=name,
        compiler_params=_compiler_params(("arbitrary", "arbitrary", "arbitrary")))(
            idx.reshape(-1), ys.reshape(e, r, sub, LANES))
    return out.reshape(b, n, d)


def _pad_cols(w, n):
    return jnp.pad(w, ((0, 0), (0, n - w.shape[1])))


def _ssd_project(x2d, g, shift, scale, rows_per_batch, w_main, w_dt, conv_w, conv_b, dt_bias, b, l):
    d_inner = SSM_HEADS * SSM_HEAD_DIM
    main = fused_matmul(x2d, w_main, prologue="norm_mod", g=g, shift=shift, scale=scale,
                        rows_per_batch=rows_per_batch, out_dtype=BF16, name="ssd_in_proj")
    dt_raw = fused_matmul(x2d, w_dt, prologue="norm_mod", g=g, shift=shift, scale=scale,
                          rows_per_batch=rows_per_batch, precise=True, name="ssd_dt_proj")
    xbc = conv_silu(main.reshape(b, l, -1), d_inner, conv_w, conv_b)
    dt = jax.nn.softplus(dt_raw[:, :2 * SSM_HEADS].reshape(b, l, 2, SSM_HEADS) + dt_bias)
    return main, xbc, dt


def _ssd_bidirectional(xbc, dt, a, d_skip, h0_f, h0_b):
    dt_f, dt_b = dt[:, :, 0], dt[:, :, 1]
    y_f, hf = ssd_scan(xbc, dt_f, jnp.swapaxes(dt_f, 1, 2), a[0], h0_f, reverse=False, name="ssd_scan_fwd")
    y, hb = ssd_scan(xbc, dt_b, jnp.swapaxes(dt_b, 1, 2), a[1], h0_b, reverse=True, y_prev=y_f, d_skip=d_skip,
                     name="ssd_scan_bwd")
    return y, hf, hb


def _moe(x2d, g, shift, scale, gate_mod, rows_per_batch, b, n, router_w, w1, w3, w2):
    d = x2d.shape[1]
    cap = EC_CAPACITY_FACTOR * n // N_EXPERTS
    in_pallas = n % ROUTE_CHUNK == 0
    router_w = _pad_cols(router_w, LANES)
    bi = jnp.arange(b)[:, None, None]
    if in_pallas:
        logits = fused_matmul(x2d, router_w, prologue="norm_mod", g=g, shift=shift, scale=scale,
                              rows_per_batch=rows_per_batch, precise=True, name="moe_router")
        idx, gate = route(logits, b, n, cap)
        rows = jnp.swapaxes(idx + bi * n, 0, 1).reshape(-1)
        gate_e = jnp.swapaxes(gate, 0, 1).reshape(N_EXPERTS, b * cap, 1)
        ys = expert_ffn_gather(x2d, rows, g, shift, scale, w1, w3, w2, gate_e, rows_per_batch=cap)
        moe = combine(ys, idx, n)
    else:
        logits, h = fused_matmul(x2d, router_w, prologue="norm_mod", g=g, shift=shift, scale=scale,
                                 rows_per_batch=rows_per_batch, precise=True, emit_h=BF16, name="moe_router_ctx")
        aff = jax.nn.softmax(logits[:, :N_EXPERTS].reshape(b, n, N_EXPERTS), axis=-1)
        gate, idx = lax.top_k(jnp.swapaxes(aff, 1, 2), cap)
        xs = jnp.swapaxes(h.reshape(b, n, d)[bi, idx], 0, 1).reshape(N_EXPERTS, b * cap, d)
        gate_e = jnp.swapaxes(gate, 0, 1).reshape(N_EXPERTS, b * cap, 1)
        ys = expert_ffn(xs, w1, w3, w2, gate_e)
        ys = jnp.swapaxes(ys.reshape(N_EXPERTS, b, cap, d), 0, 1)
        moe = jnp.zeros((b, n, d), F32).at[bi, idx].add(ys)
    x3 = x2d.reshape(b, n, d)
    gm = gate_mod if gate_mod.shape[0] == b else jnp.broadcast_to(gate_mod, (b, 1, d))
    return (x3 + gm * moe).reshape(b * n, d)


def _dft_tables(l, ch):
    def cs(nn):
        idx = jnp.arange(nn)
        ang = ((idx[:, None] * idx[None, :]) % nn).astype(F32) * (2.0 * math.pi / nn)
        s = 1.0 / math.sqrt(nn)
        return jnp.cos(ang) * s, jnp.sin(ang) * s
    cl, sl = cs(l)
    cc, sc = cs(ch)
    return cl, sl, cc, sc


def _fourier(h2d, b, l, out_w, out_b, gate, res, rows_per_batch):
    d = h2d.shape[1]
    ch = d // FNET_GROUPS
    cl, sl, cc, sc = _dft_tables(l, ch)
    eye = jnp.eye(FNET_GROUPS, dtype=F32)
    w_ch = jnp.concatenate([jnp.kron(eye, cc), jnp.kron(eye, sc)], axis=1).astype(BF16)
    pq = fused_matmul(h2d, w_ch, out_dtype=BF16, name="fnet_channel_dft")
    pq = pq.reshape(b, l, 2, d)
    rhs = jnp.transpose(pq, (2, 1, 0, 3)).reshape(2 * l, b * d)
    lhs = jnp.concatenate([cl, -sl], axis=1).astype(BF16)
    mixed = fused_matmul(lhs, rhs, out_dtype=BF16, tk=2048, name="fnet_position_dft")
    mixed = jnp.transpose(mixed.reshape(l, b, d), (1, 0, 2)).reshape(b * l, d)
    return fused_matmul(mixed, out_w.astype(BF16), bias=out_b, gate=gate, res=res,
                        rows_per_batch=rows_per_batch, name="fnet_out_proj")


FFT_N1 = 32
FFT_ROWS = 8
FFT_CH = 128


def _fnet_kernel(x_ref, cc_ref, sc_ref, kc_ref, ks_ref, mc_ref, ms_ref, o_ref, zr_s, zq_s, ar_s, ai_s):
    n1, nj, rows, ch = zr_s.shape
    n2 = nj * rows
    x = x_ref[...]
    zr_s[...] = _dot(x, cc_ref[...]).reshape(zr_s.shape)
    zq_s[...] = _dot(x, sc_ref[...]).reshape(zq_s.shape)
    kc, ks = kc_ref[...], ks_ref[...]

    def stage1(j, carry):
        p = zr_s[:, j].reshape(n1 * rows, ch).astype(BF16)
        q = zq_s[:, j].reshape(n1 * rows, ch).astype(BF16)
        ar_s[:, j] = (_dot(kc, p) - _dot(ks, q)).reshape(n1, rows, ch)
        ai_s[:, j] = (_dot(kc, q) + _dot(ks, p)).reshape(n1, rows, ch)
        return carry

    lax.fori_loop(0, nj, stage1, 0)

    def stage2(k1, carry):
        ar = ar_s[k1].reshape(n2, ch).astype(BF16)
        ai = ai_s[k1].reshape(n2, ch).astype(BF16)
        o_ref[k1] = (_dot(mc_ref[k1], ar) - _dot(ms_ref[k1], ai)).astype(o_ref.dtype)
        return carry

    lax.fori_loop(0, n1, stage2, 0)


def _fnet_tables(l, ch):
    n1, n2 = FFT_N1, l // FFT_N1
    def cs(num, den, scale):
        ang = (num % den).astype(F32) * (2.0 * math.pi / den)
        return jnp.cos(ang) * scale, jnp.sin(ang) * scale
    ic = jnp.arange(ch)
    cc, sc = cs(ic[:, None] * ic[None, :], ch, 1.0 / math.sqrt(ch))
    i1 = jnp.arange(n1)
    c1, s1 = cs(i1[:, None] * i1[None, :], n1, 1.0)
    eye = jnp.eye(FFT_ROWS, dtype=F32)
    kc, ks = jnp.kron(c1, eye), jnp.kron(s1, eye)
    k = i1[:, None, None] + n1 * jnp.arange(n2)[None, :, None]
    mc, ms = cs(k * jnp.arange(n2)[None, None, :], l, 1.0 / math.sqrt(l))
    return [t.astype(BF16) for t in (cc, sc, kc, ks, mc, ms)]


def fourier_mix(h, name="fnet_mix"):
    b, l, d = h.shape
    ch = d // FNET_GROUPS
    n1, n2 = FFT_N1, l // FFT_N1
    halves = ch // FFT_CH
    cc, sc, kc, ks, mc, ms = _fnet_tables(l, ch)
    once = pl.Buffered(1)
    const2 = lambda i, g, s: (0, 0)
    return pl.pallas_call(
        _fnet_kernel, grid=(b, FNET_GROUPS, halves),
        in_specs=[pl.BlockSpec((None, l, ch), lambda i, g, s: (i, 0, g)),
                  pl.BlockSpec((ch, FFT_CH), lambda i, g, s: (0, s)),
                  pl.BlockSpec((ch, FFT_CH), lambda i, g, s: (0, s)),
                  pl.BlockSpec((n1 * FFT_ROWS, n1 * FFT_ROWS), const2, pipeline_mode=once),
                  pl.BlockSpec((n1 * FFT_ROWS, n1 * FFT_ROWS), const2, pipeline_mode=once),
                  pl.BlockSpec((n1, n2, n2), lambda i, g, s: (0, 0, 0), pipeline_mode=once),
                  pl.BlockSpec((n1, n2, n2), lambda i, g, s: (0, 0, 0), pipeline_mode=once)],
        out_specs=pl.BlockSpec((None, n1, n2, FFT_CH), lambda i, g, s: (i, 0, 0, g * halves + s)),
        out_shape=jax.ShapeDtypeStruct((b, n1, n2, d), BF16),
        scratch_shapes=[pltpu.VMEM((n1, n2 // FFT_ROWS, FFT_ROWS, FFT_CH), F32) for _ in range(4)],
        name=name,
        compiler_params=_compiler_params(("parallel", "arbitrary", "arbitrary")))(h, cc, sc, kc, ks, mc, ms)


def fnet_out_proj(mixed, w, bias, gate, res, name="fnet_out_proj"):
    b, n1, n2, d = mixed.shape
    n = w.shape[1]
    res3 = res.reshape(b, n2, n1 * n)
    kern = functools.partial(_mm_kernel, prologue=None, precise=False, has_bias=True, has_gate_res=True,
                             emit_h=False, nk=1)
    rows = lambda i, j, k: (i // n1, 0, i % n1)
    out = pl.pallas_call(
        kern, grid=(b * n1, 1, 1),
        in_specs=[pl.BlockSpec((None, None, n2, d), lambda i, j, k: (i // n1, i % n1, 0, 0)),
                  pl.BlockSpec((d, n), lambda i, j, k: (0, 0)),
                  pl.BlockSpec((1, n), lambda i, j, k: (0, 0)),
                  pl.BlockSpec((None, 1, n), lambda i, j, k: (i // n1, 0, 0)),
                  pl.BlockSpec((None, n2, n), rows)],
        out_specs=pl.BlockSpec((None, n2, n), rows),
        out_shape=jax.ShapeDtypeStruct((b, n2, n1 * n), F32), name=name,
        compiler_params=_compiler_params(("parallel", "arbitrary", "arbitrary")))(
            mixed, w, bias.reshape(1, n), gate, res3)
    return out.reshape(b * n2 * n1, n)


CONV_HALO = 16


def _conv_kernel(prev_ref, x_ref, next_ref, w_ref, b_ref, o_ref, *, taps):
    i = pl.program_id(1)
    last = pl.num_programs(1) - 1
    tl = x_ref.shape[0]
    prev = jnp.where(i > 0, prev_ref[...].astype(F32), 0.0)
    nxt = jnp.where(i < last, next_ref[...].astype(F32), 0.0)
    ext = jnp.concatenate([prev, x_ref[...].astype(F32), nxt], axis=0)
    acc = b_ref[...]
    for k in range(taps):
        lo = CONV_HALO - taps // 2 + k
        acc = acc + ext[lo:lo + tl, :] * w_ref[k:k + 1, :]
    o_ref[...] = acc * jax.nn.sigmoid(acc)


def conv_silu(main, col0, conv_w, conv_b, *, tl=512, tc=1024, name="ssd_conv"):
    b, l, _ = main.shape
    taps, c = conv_w.shape
    tl = min(tl, l)
    assert l % tl == 0 and c % tc == 0 and col0 % tc == 0 and tl % CONV_HALO == 0
    cb = col0 // tc
    hb = tl // CONV_HALO
    nh = l // CONV_HALO
    kern = functools.partial(_conv_kernel, taps=taps)
    return pl.pallas_call(
        kern, grid=(b, l // tl, c // tc),
        in_specs=[pl.BlockSpec((None, CONV_HALO, tc), lambda n, i, j: (n, jnp.maximum(i * hb - 1, 0), cb + j)),
                  pl.BlockSpec((None, tl, tc), lambda n, i, j: (n, i, cb + j)),
                  pl.BlockSpec((None, CONV_HALO, tc), lambda n, i, j: (n, jnp.minimum((i + 1) * hb, nh - 1), cb + j)),
                  pl.BlockSpec((taps, tc), lambda n, i, j: (0, j)),
                  pl.BlockSpec((1, tc), lambda n, i, j: (0, j))],
        out_specs=pl.BlockSpec((None, tl, tc), lambda n, i, j: (n, i, j)),
        out_shape=jax.ShapeDtypeStruct((b, l, c), F32), name=name,
        compiler_params=_compiler_params(("parallel", "parallel", "parallel")))(
            main, main, main, conv_w, conv_b.reshape(1, c))


def kernel(x, c, ctx, c_ctx, mod_w, mod_b, norm1_g, norm2_g, ssd_in_w, ssd_conv_w, ssd_conv_b, ssd_dt_bias,
           ssd_a_log, ssd_d, ssd_norm_g, ssd_out_w, fnet_out_w, fnet_out_b, diff_qkv_w, diff_out_w, diff_lambda,
           diff_subln_g, router_w, moe_w1, moe_w3, moe_w2, final_g):
    b, l, d = x.shape
    lc = ctx.shape[1]
    d_inner = SSM_HEADS * SSM_HEAD_DIM

    cond = jnp.concatenate([c, c_ctx[None], jnp.zeros(((8 - (b + 1) % 8) % 8, d), F32)], axis=0)
    cond = jax.nn.silu(cond)
    xl = x.reshape(b * l, d)
    xc = ctx.reshape(b * lc, d)

    for i in range(DEPTH):
        need_ctx = i < DEPTH - 1
        kind, j = i % N_MIXERS, i // N_MIXERS
        mod = fused_matmul(cond, mod_w[i], bias=mod_b[i], tn=1024, name="modulation")
        m_l = [mod[:b, t * d:(t + 1) * d].reshape(b, 1, d) for t in range(6)]
        m_c = [mod[b:b + 1, t * d:(t + 1) * d].reshape(1, 1, d) for t in range(6)]

        if kind == 0:
            w_main = ssd_in_w[j][:, :d_inner + d_inner + 2 * SSM_GROUPS * SSM_STATE].astype(BF16)
            w_dt = _pad_cols(ssd_in_w[j][:, w_main.shape[1]:], LANES)
            a = -jnp.exp(ssd_a_log[j].astype(F32))
            h0 = jnp.zeros((b, SSM_GROUPS, SSM_STATE, GROUP_WIDTH), F32)
            z_c, xbc_c, dt_c = _ssd_project(xc, norm1_g[i], m_c[0], m_c[1], None, w_main, w_dt,
                                            ssd_conv_w[j], ssd_conv_b[j], ssd_dt_bias[j], b, lc)
            y_c, hf, hb = _ssd_bidirectional(xbc_c, dt_c, a, ssd_d[j], h0, h0)
            z_l, xbc_l, dt_l = _ssd_project(xl, norm1_g[i], m_l[0], m_l[1], l, w_main, w_dt,
                                            ssd_conv_w[j], ssd_conv_b[j], ssd_dt_bias[j], b, l)
            y_l, _, _ = _ssd_bidirectional(xbc_l, dt_l, a, ssd_d[j], hf, hb)
            w_out = ssd_out_w[j].astype(BF16)
            xl = fused_matmul(y_l.reshape(b * l, d_inner), w_out, prologue="gated_norm",
                              z=z_l, g=ssd_norm_g[j], gate=m_l[2], res=xl,
                              rows_per_batch=l, name="ssd_out_proj")
            if need_ctx:
                xc = fused_matmul(y_c.reshape(b * lc, d_inner), w_out, prologue="gated_norm",
                                  z=z_c, g=ssd_norm_g[j], gate=m_c[2], res=xc,
                                  name="ssd_out_proj_ctx")
        elif kind == 1:
            ident = jnp.eye(d, dtype=BF16)
            _, h_l = fused_matmul(xl, ident[:, :LANES], prologue="norm_mod", g=norm1_g[i], shift=m_l[0],
                                  scale=m_l[1], rows_per_batch=l, emit_h=BF16, name="fnet_norm")
            mixed = fourier_mix(h_l.reshape(b, l, d))
            xl = fnet_out_proj(mixed, fnet_out_w[j].astype(BF16), fnet_out_b[j], m_l[2], xl)
            if need_ctx:
                _, h_c = fused_matmul(xc, ident[:, :LANES], prologue="norm_mod", g=norm1_g[i], shift=m_c[0],
                                      scale=m_c[1], emit_h=BF16, name="fnet_norm_ctx")
                xc = _fourier(h_c, b, lc, fnet_out_w[j], fnet_out_b[j], m_c[2], xc, None)
        else:
            lambda_init = 0.8 - 0.6 * math.exp(-0.3 * i)
            lp = diff_lambda[j].astype(F32)
            lam = jnp.exp(jnp.sum(lp[0] * lp[1])) - jnp.exp(jnp.sum(lp[2] * lp[3])) + lambda_init
            w_qkv = diff_qkv_w[j].astype(BF16)
            w_out = diff_out_w[j].astype(BF16)
            qkv_c = fused_matmul(xc, w_qkv, prologue="norm_mod", g=norm1_g[i], shift=m_c[0], scale=m_c[1],
                                 name="diff_qkv_ctx").reshape(b, lc, 3 * d)
            qkv_l = fused_matmul(xl, w_qkv, prologue="norm_mod", g=norm1_g[i], shift=m_l[0], scale=m_l[1],
                                 rows_per_batch=l, name="diff_qkv").reshape(b, l, 3 * d)
            o_l = diff_attention(qkv_l, qkv_l, qkv_c, lam, diff_subln_g[j], 1.0 - lambda_init, _rope_tables(l),
                                 name="diff_attention", **ATTN_TILES)
            xl = fused_matmul(o_l.reshape(b * l, d), w_out, gate=m_l[2], res=xl, rows_per_batch=l,
                              name="diff_out_proj")
            if need_ctx:
                o_c = diff_attention(qkv_c, None, qkv_c, lam, diff_subln_g[j], 1.0 - lambda_init, None,
                                     name="diff_attention_ctx", **ATTN_TILES_CTX)
                xc = fused_matmul(o_c.reshape(b * lc, d), w_out, gate=m_c[2], res=xc, name="diff_out_proj_ctx")

        w1, w3, w2 = (moe_w1, i), (moe_w3, i), (moe_w2, i)
        xl = _moe(xl, norm2_g[i], m_l[3], m_l[4], m_l[5], l, b, l, router_w[i], w1, w3, w2)
        if need_ctx:
            xc = _moe(xc, norm2_g[i], m_c[3], m_c[4], m_c[5], None, b, lc, router_w[i], w1, w3, w2)

    zero = jnp.zeros((1, 1, d), F32)
    _, out = fused_matmul(xl, jnp.eye(d, dtype=BF16)[:, :LANES], prologue="norm_mod", g=final_g, shift=zero,
                          scale=zero, emit_h=F32, name="final_norm")
    return out.reshape(b, l, d)
```

```python
import functools
import math

import jax
import jax.numpy as jnp
from jax import lax
from jax.experimental import pallas as pl
from jax.experimental.pallas import tpu as pltpu

F32 = jnp.float32
BF16 = jnp.bfloat16

EPS = 1e-6
DEPTH = 4
N_MIXERS = 3
GRID_W = 64
ROPE_THETA = 10000.0

SSM_HEAD_DIM = 64
SSM_HEADS = 32
SSM_GROUPS = 8
SSM_STATE = 128
SSM_CHUNK = 128
SSM_CONV = 5
HEADS_PER_GROUP = SSM_HEADS // SSM_GROUPS
GROUP_WIDTH = HEADS_PER_GROUP * SSM_HEAD_DIM

FNET_GROUPS = 4

DIFF_HEADS = 8
DIFF_HEAD_DIM = 64
DIFF_V_DIM = 2 * DIFF_HEAD_DIM
DIFF_SCALE = DIFF_HEAD_DIM ** -0.5
ROPE_PAIRS = DIFF_HEAD_DIM // 4

N_EXPERTS = 16
EC_CAPACITY_FACTOR = 2

LANES = 128
VMEM_LIMIT_BYTES = 56 * 1024 * 1024


def _compiler_params(semantics):
    return pltpu.CompilerParams(dimension_semantics=semantics, vmem_limit_bytes=VMEM_LIMIT_BYTES)


def _split_bf16(x, parts):
    out = []
    r = x
    for _ in range(parts):
        p = r.astype(BF16)
        out.append(p)
        r = r - p.astype(F32)
    return out


def _dot(a, b):
    return jnp.dot(a, b, preferred_element_type=F32)


def _dot_nt(a, b):
    return lax.dot_general(a, b, (((1,), (1,)), ((), ())), preferred_element_type=F32)


def _dot_exact_rhs(x, m_bf16, parts=3, nt=False):
    f = _dot_nt if nt else _dot
    acc = None
    for p in _split_bf16(x, parts):
        t = f(p, m_bf16)
        acc = t if acc is None else acc + t
    return acc


def _dot_exact_lhs(m_bf16, x, parts=3):
    acc = None
    for p in _split_bf16(x, parts):
        t = _dot(m_bf16, p)
        acc = t if acc is None else acc + t
    return acc


def _mm_kernel(*refs, prologue, precise, has_bias, has_gate_res, emit_h, nk):
    it = iter(refs)
    x_ref = next(it)
    z_ref = g_ref = sh_ref = sc_ref = None
    if prologue == "norm_mod":
        g_ref, sh_ref, sc_ref = next(it), next(it), next(it)
    elif prologue == "gated_norm":
        z_ref, g_ref = next(it), next(it)
    w_ref = next(it)
    b_ref = next(it) if has_bias else None
    gate_ref = res_ref = None
    if has_gate_res:
        gate_ref, res_ref = next(it), next(it)
    o_ref = next(it)
    hout_ref = next(it) if emit_h else None
    h_ref = next(it) if prologue else None
    acc_ref = next(it) if nk > 1 else None

    j = pl.program_id(1)
    k = pl.program_id(2)

    if prologue:
        @pl.when(j == 0)
        def _():
            x = x_ref[...].astype(F32)
            if prologue == "gated_norm":
                z = z_ref[...].astype(F32)
                x = x * (z * jax.nn.sigmoid(z))
            ms = jnp.mean(x * x, axis=-1, keepdims=True)
            y = x * lax.rsqrt(ms + EPS) * g_ref[...]
            if prologue == "norm_mod":
                y = y * (1.0 + sc_ref[...]) + sh_ref[...]
            h_ref[...] = y.astype(h_ref.dtype)
            if emit_h:
                hout_ref[...] = y.astype(hout_ref.dtype)
        lhs = h_ref[...]
    else:
        lhs = x_ref[...]

    w = w_ref[...]
    if precise:
        l_hi, l_lo = _split_bf16(lhs.astype(F32), 2)
        w_hi, w_lo = _split_bf16(w.astype(F32), 2)
        part = _dot(l_hi, w_hi) + (_dot(l_lo, w_hi) + _dot(l_hi, w_lo))
    else:
        part = _dot(lhs.astype(BF16), w.astype(BF16))

    def finish(acc):
        if has_bias:
            acc = acc + b_ref[...]
        if has_gate_res:
            acc = res_ref[...].astype(F32) + gate_ref[...] * acc
        o_ref[...] = acc.astype(o_ref.dtype)

    if nk == 1:
        finish(part)
    else:
        @pl.when(k == 0)
        def _():
            acc_ref[...] = part

        @pl.when(k > 0)
        def _():
            acc_ref[...] += part

        @pl.when(k == nk - 1)
        def _():
            finish(acc_ref[...])


def fused_matmul(x, w, *, prologue=None, g=None, shift=None, scale=None, z=None, bias=None,
                 gate=None, res=None, rows_per_batch=None, precise=False, out_dtype=F32,
                 emit_h=None, tm=512, tn=1024, tk=None, name="fused_matmul"):
    m, kdim = x.shape
    n = w.shape[1]
    rows_per_batch = m if rows_per_batch is None else rows_per_batch
    tm = min(tm, m, rows_per_batch)
    tn = min(tn, n)
    tk = kdim if tk is None else min(tk, kdim)
    assert m % tm == 0 and n % tn == 0 and kdim % tk == 0, (x.shape, w.shape, tm, tn, tk)
    nk = kdim // tk
    assert not (prologue and nk > 1)
    assert rows_per_batch % tm == 0
    tiles_per_batch = rows_per_batch // tm

    def bidx(i):
        return i // tiles_per_batch

    args = [x]
    in_specs = [pl.BlockSpec((tm, tk), lambda i, j, k: (i, k))]
    if prologue == "norm_mod":
        args += [g.reshape(1, kdim), shift, scale]
        in_specs += [pl.BlockSpec((1, kdim), lambda i, j, k: (0, 0)),
                     pl.BlockSpec((None, 1, kdim), lambda i, j, k: (bidx(i), 0, 0)),
                     pl.BlockSpec((None, 1, kdim), lambda i, j, k: (bidx(i), 0, 0))]
    elif prologue == "gated_norm":
        args += [z, g.reshape(1, kdim)]
        in_specs += [pl.BlockSpec((tm, kdim), lambda i, j, k: (i, 0)),
                     pl.BlockSpec((1, kdim), lambda i, j, k: (0, 0))]
    args.append(w)
    in_specs.append(pl.BlockSpec((tk, tn), lambda i, j, k: (k, j)))
    if bias is not None:
        args.append(bias.reshape(1, n))
        in_specs.append(pl.BlockSpec((1, tn), lambda i, j, k: (0, j)))
    if gate is not None:
        args += [gate, res]
        in_specs += [pl.BlockSpec((None, 1, tn), lambda i, j, k: (bidx(i), 0, j)),
                     pl.BlockSpec((tm, tn), lambda i, j, k: (i, j))]

    out_shape = [jax.ShapeDtypeStruct((m, n), out_dtype)]
    out_specs = [pl.BlockSpec((tm, tn), lambda i, j, k: (i, j))]
    if emit_h is not None:
        out_shape.append(jax.ShapeDtypeStruct((m, kdim), emit_h))
        out_specs.append(pl.BlockSpec((tm, kdim), lambda i, j, k: (i, 0)))

    scratch = []
    if prologue:
        scratch.append(pltpu.VMEM((tm, kdim), F32 if precise else BF16))
    if nk > 1:
        scratch.append(pltpu.VMEM((tm, tn), F32))

    kern = functools.partial(_mm_kernel, prologue=prologue, precise=precise, has_bias=bias is not None,
                             has_gate_res=gate is not None, emit_h=emit_h is not None, nk=nk)
    outs = pl.pallas_call(
        kern, grid=(m // tm, n // tn, nk), in_specs=in_specs, out_specs=out_specs, out_shape=out_shape,
        scratch_shapes=scratch, name=name,
        compiler_params=_compiler_params(("parallel", "arbitrary", "arbitrary")))(*args)
    return outs if emit_h is not None else outs[0]


def _norm_kernel(x_ref, g_ref, sh_ref, sc_ref, o_ref, *, modulated):
    x = x_ref[...].astype(F32)
    y = x * lax.rsqrt(jnp.mean(x * x, axis=-1, keepdims=True) + EPS) * g_ref[...]
    if modulated:
        y = y * (1.0 + sc_ref[...]) + sh_ref[...]
    o_ref[...] = y.astype(o_ref.dtype)


def norm_modulate(x, g, shift=None, scale=None, *, rows_per_batch=None, out_dtype=F32, tm=512, name="norm"):
    m, d = x.shape
    modulated = shift is not None
    if not modulated:
        shift = scale = jnp.zeros((1, 1, d), F32)
    rows_per_batch = m if rows_per_batch is None else rows_per_batch
    tm = min(tm, m, rows_per_batch)
    assert m % tm == 0 and rows_per_batch % tm == 0
    tiles_per_batch = rows_per_batch // tm
    per_batch = pl.BlockSpec((None, 1, d), lambda i: (i // tiles_per_batch, 0, 0))
    return pl.pallas_call(
        functools.partial(_norm_kernel, modulated=modulated), grid=(m // tm,),
        in_specs=[pl.BlockSpec((tm, d), lambda i: (i, 0)), pl.BlockSpec((1, d), lambda i: (0, 0)),
                  per_batch, per_batch],
        out_specs=pl.BlockSpec((tm, d), lambda i: (i, 0)),
        out_shape=jax.ShapeDtypeStruct((m, d), out_dtype), name=name,
        compiler_params=_compiler_params(("parallel",)))(x, g.reshape(1, d), shift, scale)


def _ssd_kernel(*refs, reverse, combine, q):
    it = iter(refs)
    xs_ref, bm_ref, cm_ref, dt_ref, dtt_ref, a_ref, at_ref, h0_ref = (next(it) for _ in range(8))
    yprev_ref = dskip_ref = None
    if combine:
        yprev_ref, dskip_ref = next(it), next(it)
    y_ref, hfin_ref, state_ref = next(it), next(it), next(it)

    c = pl.program_id(1)

    @pl.when(c == 0)
    def _():
        state_ref[...] = h0_ref[...]

    ii = lax.broadcasted_iota(jnp.int32, (q, q), 0)
    jj = lax.broadcasted_iota(jnp.int32, (q, q), 1)
    mask = (ii <= jj) if reverse else (ii >= jj)
    tri = jnp.where(mask, 1.0, 0.0).astype(BF16)

    dt = dt_ref[...]
    dta = dt * a_ref[...]
    dta_t = dtt_ref[...] * at_ref[...]
    a_cs = _dot_exact_lhs(tri, dta)
    a_cs_t = _dot_exact_rhs(dta_t, tri, nt=True)
    edge = 0 if reverse else q - 1
    tot = a_cs[edge:edge + 1, :]
    to_end = jnp.exp(tot - a_cs)
    into = jnp.exp(a_cs)
    chunk_decay = jnp.exp(tot)

    hh = lax.broadcasted_iota(jnp.int32, (SSM_HEADS, SSM_HEADS * SSM_HEAD_DIM), 0)
    cc = lax.broadcasted_iota(jnp.int32, (SSM_HEADS, SSM_HEADS * SSM_HEAD_DIM), 1)
    expand = jnp.where(cc // SSM_HEAD_DIM == hh, 1.0, 0.0).astype(BF16)
    dt_x = _dot_exact_rhs(dt, expand, parts=1)
    to_end_x = _dot_exact_rhs(to_end, expand, parts=1)
    into_x = _dot_exact_rhs(into, expand, parts=1)
    decay_x = _dot_exact_rhs(jnp.broadcast_to(chunk_decay, (8, SSM_HEADS)), expand, parts=3)[0:1, :]

    for g in range(SSM_GROUPS):
        c0 = g * GROUP_WIDTH
        xs_g = xs_ref[:, c0:c0 + GROUP_WIDTH]
        bm_g = bm_ref[:, g * SSM_STATE:(g + 1) * SSM_STATE]
        cm_g = cm_ref[:, g * SSM_STATE:(g + 1) * SSM_STATE].astype(BF16)
        xdt_g = xs_g * dt_x[:, c0:c0 + GROUP_WIDTH]
        xdt_b = xdt_g.astype(BF16)
        cb = _dot_nt(cm_g, bm_g.astype(BF16))
        st = state_ref[g]
        y_g = _dot(cm_g, st.astype(BF16)) * into_x[:, c0:c0 + GROUP_WIDTH]
        if combine:
            y_g = y_g + yprev_ref[:, c0:c0 + GROUP_WIDTH] + dskip_ref[:, c0:c0 + GROUP_WIDTH] * xs_g
        y_ref[:, c0:c0 + GROUP_WIDTH] = y_g
        for r in range(HEADS_PER_GROUP):
            h = g * HEADS_PER_GROUP + r
            seg = a_cs[:, h:h + 1] - a_cs_t[h:h + 1, :]
            lmat = (jnp.exp(jnp.where(mask, seg, -1e30)) * cb).astype(BF16)
            lo = c0 + r * SSM_HEAD_DIM
            y_ref[:, lo:lo + SSM_HEAD_DIM] += _dot(lmat, xdt_b[:, r * SSM_HEAD_DIM:(r + 1) * SSM_HEAD_DIM])
        xdtw = (xdt_g * to_end_x[:, c0:c0 + GROUP_WIDTH]).astype(BF16)
        new = _dot(bm_g.T.astype(BF16), xdtw)
        state_ref[g] = st * decay_x[:, c0:c0 + GROUP_WIDTH] + new

    @pl.when(c == pl.num_programs(1) - 1)
    def _():
        hfin_ref[...] = state_ref[...]


def ssd_scan(xbc, dt, dt_t, a, h0, *, reverse, y_prev=None, d_skip=None, name="ssd_scan"):
    b, l, _ = xbc.shape
    q = SSM_CHUNK
    nc = l // q
    d_inner = SSM_HEADS * SSM_HEAD_DIM
    nbc = SSM_GROUPS * SSM_STATE
    combine = y_prev is not None

    def cidx(c):
        return (nc - 1 - c) if reverse else c

    xs_blocks = d_inner // d_inner
    in_specs = [
        pl.BlockSpec((None, q, d_inner), lambda i, c: (i, cidx(c), 0)),
        pl.BlockSpec((None, q, nbc), lambda i, c: (i, cidx(c), d_inner // nbc)),
        pl.BlockSpec((None, q, nbc), lambda i, c: (i, cidx(c), d_inner // nbc + 1)),
        pl.BlockSpec((None, q, SSM_HEADS), lambda i, c: (i, cidx(c), 0)),
        pl.BlockSpec((None, SSM_HEADS, q), lambda i, c: (i, 0, cidx(c))),
        pl.BlockSpec((1, SSM_HEADS), lambda i, c: (0, 0)),
        pl.BlockSpec((SSM_HEADS, 1), lambda i, c: (0, 0)),
        pl.BlockSpec((None, SSM_GROUPS, SSM_STATE, GROUP_WIDTH), lambda i, c: (i, 0, 0, 0)),
    ]
    del xs_blocks
    args = [xbc, xbc, xbc, dt, dt_t, a.reshape(1, SSM_HEADS), a.reshape(SSM_HEADS, 1), h0]
    if combine:
        in_specs += [pl.BlockSpec((None, q, d_inner), lambda i, c: (i, cidx(c), 0)),
                     pl.BlockSpec((1, d_inner), lambda i, c: (0, 0))]
        args += [y_prev, jnp.repeat(d_skip, SSM_HEAD_DIM).reshape(1, d_inner)]
    out_shape = [jax.ShapeDtypeStruct((b, l, d_inner), F32),
                 jax.ShapeDtypeStruct((b, SSM_GROUPS, SSM_STATE, GROUP_WIDTH), F32)]
    out_specs = [pl.BlockSpec((None, q, d_inner), lambda i, c: (i, cidx(c), 0)),
                 pl.BlockSpec((None, SSM_GROUPS, SSM_STATE, GROUP_WIDTH), lambda i, c: (i, 0, 0, 0))]
    kern = functools.partial(_ssd_kernel, reverse=reverse, combine=combine, q=q)
    return pl.pallas_call(
        kern, grid=(b, nc), in_specs=in_specs, out_specs=out_specs, out_shape=out_shape,
        scratch_shapes=[pltpu.VMEM((SSM_GROUPS, SSM_STATE, GROUP_WIDTH), F32)], name=name,
        compiler_params=_compiler_params(("parallel", "arbitrary")))(*args)


def _rope(x, cos, sin_up, sin_dn):
    return (x * cos + pltpu.roll(x, ROPE_PAIRS, 1) * sin_up
            + pltpu.roll(x, LANES - ROPE_PAIRS, 1) * sin_dn)


LOG2E = 1.4426950408889634


def _attn_kernel(*refs, n_lat, n_ctx, tk, rope, out_scale):
    it = iter(refs)
    q_ref = next(it)
    kl_ref = vl_ref = None
    if n_lat:
        kl_ref, vl_ref = next(it), next(it)
    kc_ref, vc_ref = next(it), next(it)
    if rope:
        cq_ref, suq_ref, sdq_ref, ck_ref, suk_ref, sdk_ref = (next(it) for _ in range(6))
    lam_ref, g_ref, o_ref, k_s, vt_s, m_s, l_s, acc_s, s_s = (next(it) for _ in range(9))

    qi = pl.program_id(2)
    tq = q_ref.shape[0]
    n_tiles = (n_lat + n_ctx) // tk

    @pl.when(qi == 0)
    def _():
        if n_lat:
            k = kl_ref[...]
            if rope:
                k = _rope(k, ck_ref[...], suk_ref[...], sdk_ref[...])
            k_s[0:n_lat, :] = k.astype(BF16)
        k_s[n_lat:n_lat + n_ctx, :] = kc_ref[...].astype(BF16)
        for t in range(n_tiles):
            lo, hi = t * tk, (t + 1) * tk
            pieces = []
            if lo < n_lat:
                pieces.append(vl_ref[lo:min(hi, n_lat), :])
            if hi > n_lat:
                pieces.append(vc_ref[max(lo, n_lat) - n_lat:hi - n_lat, :])
            v = pieces[0] if len(pieces) == 1 else jnp.concatenate(pieces, axis=0)
            vt_s[t] = v.T.astype(BF16)

    q = q_ref[...] * (DIFF_SCALE * LOG2E)
    if rope:
        q = _rope(q, cq_ref[...], suq_ref[...], sdq_ref[...])
    qt = q.T
    row = lax.broadcasted_iota(jnp.int32, qt.shape, 0)
    wq = jnp.concatenate([jnp.where(row < DIFF_HEAD_DIM, qt, 0.0),
                          jnp.where(row >= DIFF_HEAD_DIM, qt, 0.0)], axis=1).astype(BF16)

    m_s[...] = jnp.full(m_s.shape, -1e30, F32)
    l_s[...] = jnp.zeros(l_s.shape, F32)
    acc_s[...] = jnp.zeros(acc_s.shape, F32)

    def scores(t):
        off = pl.multiple_of(t * tk, tk)
        return _dot(k_s[pl.ds(off, tk), :], wq)

    def absorb(buf, t):
        s = s_s[buf]
        m_old = m_s[...]
        m_new = jnp.maximum(m_old, jnp.max(s, axis=0, keepdims=True))
        alpha = jnp.exp2(m_old - m_new)
        p = jnp.exp2(s - m_new)
        l_s[...] = alpha * l_s[...] + jnp.sum(p, axis=0, keepdims=True)
        acc_s[...] = alpha * acc_s[...] + _dot(vt_s[t], p.astype(BF16))
        m_s[...] = m_new

    s_s[0] = scores(0)

    def body(i, carry):
        t0 = 2 * i
        s_s[1] = scores(t0 + 1)
        absorb(0, t0)
        s_s[0] = scores(t0 + 2)
        absorb(1, t0 + 1)
        return carry

    lax.fori_loop(0, n_tiles // 2 - 1, body, 0)
    s_s[1] = scores(n_tiles - 1)
    absorb(0, n_tiles - 2)
    absorb(1, n_tiles - 1)

    o_t = acc_s[...] / l_s[...]
    o = (o_t[:, :tq] - lam_ref[0:1, 0:1] * o_t[:, tq:]).T
    ms = jnp.mean(o * o, axis=-1, keepdims=True)
    o_ref[...] = o * lax.rsqrt(ms + EPS) * (g_ref[...] * out_scale)


ATTN_TILES = dict(tq=512, tk=1408)
ATTN_TILES_CTX = dict(tq=256, tk=128)


def diff_attention(qkv_q, qkv_lat, qkv_ctx, lam, subln_g, out_scale, rope_tabs, *, tq, tk, name):
    b, lq, d3 = qkv_q.shape
    d = d3 // 3
    nh = d // DIFF_V_DIM
    n_lat = 0 if qkv_lat is None else qkv_lat.shape[1]
    n_ctx = qkv_ctx.shape[1]
    rope = rope_tabs is not None
    tq = min(tq, lq)
    tk = min(tk, n_lat + n_ctx)
    assert lq % tq == 0 and (n_lat + n_ctx) % (2 * tk) == 0

    args = [qkv_q]
    in_specs = [pl.BlockSpec((None, tq, DIFF_V_DIM), lambda i, h, t: (i, t, h))]
    if n_lat:
        args += [qkv_lat, qkv_lat]
        in_specs += [pl.BlockSpec((None, n_lat, DIFF_V_DIM), lambda i, h, t: (i, 0, nh + h)),
                     pl.BlockSpec((None, n_lat, DIFF_V_DIM), lambda i, h, t: (i, 0, 2 * nh + h))]
    args += [qkv_ctx, qkv_ctx]
    in_specs += [pl.BlockSpec((None, n_ctx, DIFF_V_DIM), lambda i, h, t: (i, 0, nh + h)),
                 pl.BlockSpec((None, n_ctx, DIFF_V_DIM), lambda i, h, t: (i, 0, 2 * nh + h))]
    if rope:
        args += list(rope_tabs) + list(rope_tabs)
        in_specs += [pl.BlockSpec((tq, LANES), lambda i, h, t: (t, 0))] * 3
        in_specs += [pl.BlockSpec((n_lat, LANES), lambda i, h, t: (0, 0), pipeline_mode=pl.Buffered(1))] * 3
    args += [jnp.broadcast_to(lam.astype(F32).reshape(1, 1), (8, LANES)), subln_g.reshape(1, DIFF_V_DIM)]
    in_specs += [pl.BlockSpec((8, LANES), lambda i, h, t: (0, 0)),
                 pl.BlockSpec((1, DIFF_V_DIM), lambda i, h, t: (0, 0))]

    kern = functools.partial(_attn_kernel, n_lat=n_lat, n_ctx=n_ctx, tk=tk, rope=rope, out_scale=out_scale)
    return pl.pallas_call(
        kern, grid=(b, nh, lq // tq), in_specs=in_specs,
        out_specs=pl.BlockSpec((None, tq, DIFF_V_DIM), lambda i, h, t: (i, t, h)),
        out_shape=jax.ShapeDtypeStruct((b, lq, d), F32),
        scratch_shapes=[pltpu.VMEM((n_lat + n_ctx, DIFF_V_DIM), BF16),
                        pltpu.VMEM(((n_lat + n_ctx) // tk, DIFF_V_DIM, tk), BF16),
                        pltpu.VMEM((1, 2 * tq), F32), pltpu.VMEM((1, 2 * tq), F32),
                        pltpu.VMEM((DIFF_V_DIM, 2 * tq), F32),
                        pltpu.VMEM((2, tk, 2 * tq), F32)],
        name=name,
        compiler_params=_compiler_params(("parallel", "parallel", "arbitrary")))(*args)


def _rope_tables(l):
    rows = l // GRID_W
    row = jnp.repeat(jnp.arange(rows), GRID_W).astype(F32)
    col = jnp.tile(jnp.arange(GRID_W), rows).astype(F32)
    inv = ROPE_THETA ** (-jnp.arange(ROPE_PAIRS, dtype=F32) / ROPE_PAIRS)
    lane = jnp.arange(LANES)
    within = lane % DIFF_HEAD_DIM
    axis = within // (2 * ROPE_PAIRS)
    second = (within // ROPE_PAIRS) % 2
    pos = jnp.where(axis[None, :] == 0, row[:, None], col[:, None])
    ang = pos * inv[within % ROPE_PAIRS][None, :]
    cos, sin = jnp.cos(ang), jnp.sin(ang)
    sin_up = jnp.where(second[None, :] == 1, sin, 0.0)
    sin_dn = jnp.where(second[None, :] == 0, -sin, 0.0)
    return cos, sin_up, sin_dn


FFN_F_SPLIT = 2


def _ffn_kernel(x_ref, w1_ref, w3_ref, w2_ref, gate_ref, o_ref, w1_s, w3_s, w2_s):
    @pl.when(pl.program_id(1) == 0)
    def _():
        w1_s[...] = w1_ref[...].astype(BF16)
        w3_s[...] = w3_ref[...].astype(BF16)
        w2_s[...] = w2_ref[...].astype(BF16)

    x = x_ref[...]
    fs = w1_s.shape[1] // FFN_F_SPLIT
    acc = None
    for s in range(FFN_F_SPLIT):
        h1 = _dot(x, w1_s[:, s * fs:(s + 1) * fs])
        h3 = _dot(x, w3_s[:, s * fs:(s + 1) * fs])
        hid = (h1 * jax.nn.sigmoid(h1) * h3).astype(BF16)
        part = _dot(hid, w2_s[s * fs:(s + 1) * fs, :])
        acc = part if acc is None else acc + part
    o_ref[...] = acc * gate_ref[...]


def _ffn_gather_kernel(rows_ref, x_hbm, g_ref, sh_ref, sc_ref, w1_ref, w3_ref, w2_ref, gate_ref, o_ref,
                       xbuf, sem, w1_s, w3_s, w2_s):
    e, t = pl.program_id(0), pl.program_id(1)
    tiles = pl.num_programs(1)
    tm = xbuf.shape[1]
    step = e * tiles + t
    last = pl.num_programs(0) * tiles - 1
    slot = lax.rem(step, 2)

    def gather(step_id, slot_id):
        for i in range(tm):
            row = rows_ref[step_id * tm + i]
            pltpu.make_async_copy(x_hbm.at[pl.ds(row, 1), :], xbuf.at[slot_id, pl.ds(i, 1), :],
                                  sem.at[slot_id]).start()

    @pl.when(step == 0)
    def _():
        gather(0, 0)

    @pl.when(t == 0)
    def _():
        w1_s[...] = w1_ref[...].astype(BF16)
        w3_s[...] = w3_ref[...].astype(BF16)
        w2_s[...] = w2_ref[...].astype(BF16)

    pltpu.make_async_copy(x_hbm.at[pl.ds(0, tm), :], xbuf.at[slot], sem.at[slot]).wait()
    xr = xbuf[slot]
    ms = jnp.mean(xr * xr, axis=-1, keepdims=True)
    x = (xr * lax.rsqrt(ms + EPS) * g_ref[...] * (1.0 + sc_ref[...]) + sh_ref[...]).astype(BF16)

    gather(jnp.where(step < last, step + 1, 0), 1 - slot)

    fs = w1_s.shape[1] // FFN_F_SPLIT
    acc = None
    for s in range(FFN_F_SPLIT):
        h1 = _dot(x, w1_s[:, s * fs:(s + 1) * fs])
        h3 = _dot(x, w3_s[:, s * fs:(s + 1) * fs])
        hid = (h1 * jax.nn.sigmoid(h1) * h3).astype(BF16)
        part = _dot(hid, w2_s[s * fs:(s + 1) * fs, :])
        acc = part if acc is None else acc + part
    o_ref[...] = acc * gate_ref[...]

    @pl.when(step == last)
    def _():
        pltpu.make_async_copy(x_hbm.at[pl.ds(0, tm), :], xbuf.at[1 - slot], sem.at[1 - slot]).wait()


def _with_layer(w):
    return w if isinstance(w, tuple) else (w[None], 0)


def expert_ffn_gather(x2d, rows, g, shift, scale, w1, w3, w2, gate, *, rows_per_batch, tm=512,
                      name="expert_ffn_gather"):
    d = x2d.shape[1]
    (w1, l1), (w3, l3), (w2, l2) = _with_layer(w1), _with_layer(w3), _with_layer(w2)
    _, e, _, f = w1.shape
    r = rows.shape[0] // e
    tm = min(tm, r, rows_per_batch)
    assert r % tm == 0 and rows_per_batch % tm == 0 and f % FFN_F_SPLIT == 0
    tiles = r // tm
    tiles_per_batch = rows_per_batch // tm
    nb = shift.shape[0]
    once = pl.Buffered(1)
    bsel = lambda i, t, rows_ref: ((t // tiles_per_batch) % nb, 0, 0)
    grid_spec = pltpu.PrefetchScalarGridSpec(
        num_scalar_prefetch=1, grid=(e, tiles),
        in_specs=[pl.BlockSpec(memory_space=pl.ANY),
                  pl.BlockSpec((1, d), lambda i, t, rows_ref: (0, 0)),
                  pl.BlockSpec((None, 1, d), bsel),
                  pl.BlockSpec((None, 1, d), bsel),
                  pl.BlockSpec((None, None, d, f), lambda i, t, rows_ref: (l1, i, 0, 0), pipeline_mode=once),
                  pl.BlockSpec((None, None, d, f), lambda i, t, rows_ref: (l3, i, 0, 0), pipeline_mode=once),
                  pl.BlockSpec((None, None, f, d), lambda i, t, rows_ref: (l2, i, 0, 0), pipeline_mode=once),
                  pl.BlockSpec((None, tm, 1), lambda i, t, rows_ref: (i, t, 0))],
        out_specs=pl.BlockSpec((None, tm, d), lambda i, t, rows_ref: (i, t, 0)),
        scratch_shapes=[pltpu.VMEM((2, tm, d), F32), pltpu.SemaphoreType.DMA((2,)),
                        pltpu.VMEM((d, f), BF16), pltpu.VMEM((d, f), BF16), pltpu.VMEM((f, d), BF16)])
    return pl.pallas_call(
        _ffn_gather_kernel, grid_spec=grid_spec,
        out_shape=jax.ShapeDtypeStruct((e, r, d), F32), name=name,
        compiler_params=_compiler_params(("arbitrary", "arbitrary")))(
            rows, x2d, g.reshape(1, d), shift, scale, w1, w3, w2, gate)


def expert_ffn(xs, w1, w3, w2, gate, *, tm=512, name="expert_ffn"):
    e, r, d = xs.shape
    (w1, l1), (w3, l3), (w2, l2) = _with_layer(w1), _with_layer(w3), _with_layer(w2)
    f = w1.shape[3]
    tm = min(tm, r)
    assert r % tm == 0 and f % FFN_F_SPLIT == 0
    once = pl.Buffered(1)
    return pl.pallas_call(
        _ffn_kernel, grid=(e, r // tm),
        in_specs=[pl.BlockSpec((None, tm, d), lambda i, t: (i, t, 0)),
                  pl.BlockSpec((None, None, d, f), lambda i, t: (l1, i, 0, 0), pipeline_mode=once),
                  pl.BlockSpec((None, None, d, f), lambda i, t: (l3, i, 0, 0), pipeline_mode=once),
                  pl.BlockSpec((None, None, f, d), lambda i, t: (l2, i, 0, 0), pipeline_mode=once),
                  pl.BlockSpec((None, tm, 1), lambda i, t: (i, t, 0))],
        out_specs=pl.BlockSpec((None, tm, d), lambda i, t: (i, t, 0)),
        out_shape=jax.ShapeDtypeStruct((e, r, d), F32),
        scratch_shapes=[pltpu.VMEM((d, f), BF16), pltpu.VMEM((d, f), BF16), pltpu.VMEM((f, d), BF16)],
        name=name,
        compiler_params=_compiler_params(("arbitrary", "arbitrary")))(xs, w1, w3, w2, gate)


ROUTE_CHUNK = 1024


def _lane_prefix_sum(x):
    n = x.shape[-1]
    lane = lax.broadcasted_iota(jnp.int32, x.shape, x.ndim - 1)
    shift = 1
    while shift < n:
        x = x + jnp.where(lane >= shift, pltpu.roll(x, shift, x.ndim - 1), 0)
        shift *= 2
    return x


def _route_kernel(logit_ref, idx_ref, gate_ref, key_s, aff_s, cnt_s, acc_s, *, cap):
    n = logit_ref.shape[0]
    lt = logit_ref[...].T[:N_EXPERTS, :]
    ex = jnp.exp(lt - jnp.max(lt, axis=0, keepdims=True))
    aff = ex / jnp.sum(ex, axis=0, keepdims=True)
    bits = pltpu.bitcast(aff, jnp.int32)

    def bit_step(i, thr):
        trial = thr | jnp.left_shift(jnp.int32(1), 30 - i)
        cnt = jnp.sum((bits >= trial).astype(jnp.int32), axis=1, keepdims=True)
        return jnp.where(cnt >= cap, trial, thr)

    thr = lax.fori_loop(0, 31, bit_step, jnp.zeros((N_EXPERTS, 1), jnp.int32))
    gt = bits > thr
    eq = (bits == thr).astype(jnp.int32)
    need = cap - jnp.sum(gt.astype(jnp.int32), axis=1, keepdims=True)
    sel = jnp.where(gt, 1, jnp.where((_lane_prefix_sum(eq) - eq) < need, eq, 0))
    key = jnp.where(sel > 0, _lane_prefix_sum(sel) - sel, -1)

    chunk = key_s.shape[2]
    n_chunks = n // chunk
    st = min(ROUTE_SLOT_TILE, cap)
    count = _lane_prefix_sum(sel)
    for c in range(n_chunks):
        lanes = slice(c * chunk, (c + 1) * chunk)
        key_s[c] = key[:, lanes]
        aff_s[c] = aff[:, lanes]
        cnt_s[c] = count[:, (c + 1) * chunk - LANES:(c + 1) * chunk]
    acc_s[...] = jnp.zeros(acc_s.shape, F32)
    exact = lambda v: v.astype(BF16).astype(F32)
    slot0 = lax.broadcasted_iota(jnp.int32, (st, chunk), 0)
    lane = lax.broadcasted_iota(jnp.int32, (1, chunk), 1)

    def per_expert(e, carry):
        def per_chunk(c, lo):
            hi = cnt_s[c, pl.ds(e, 1), LANES - 1:LANES][0, 0]
            a = aff_s[c, pl.ds(e, 1), :]
            a_hi = exact(a)
            a_mid = exact(a - a_hi)
            a_lo = exact(a - a_hi - a_mid)
            tok = lane + c * chunk
            rows = jnp.concatenate([(tok // LANES).astype(F32), (tok % LANES).astype(F32), a_hi, a_mid, a_lo,
                                    jnp.zeros((11, chunk), F32)], axis=0).astype(BF16)
            k = key_s[c, pl.ds(e, 1), :]

            def per_tile(t, carry2):
                r0 = pl.multiple_of(t * st, st)
                onehot = jnp.where(k == slot0 + r0, 1.0, 0.0).astype(BF16)
                acc_s[e, pl.ds(r0, st), 0:16] += _dot_nt(onehot, rows)
                return carry2

            lax.fori_loop(lo // st, jnp.where(hi > lo, (hi - 1) // st + 1, lo // st), per_tile, 0)
            return hi

        lax.fori_loop(0, n_chunks, per_chunk, jnp.int32(0))
        acc_t = acc_s[e].T
        idx_ref[pl.ds(e, 1), :] = (acc_t[0:1] * LANES + acc_t[1:2]).astype(jnp.int32)
        gate_ref[pl.ds(e, 1), :] = acc_t[2:3] + acc_t[3:4] + acc_t[4:5]
        return carry

    lax.fori_loop(0, N_EXPERTS, per_expert, 0)


ROUTE_SLOT_TILE = 256


def route(logits, b, n, cap, name="moe_route"):
    kern = functools.partial(_route_kernel, cap=cap)
    chunk = min(ROUTE_CHUNK, n)
    assert n % chunk == 0 and cap % min(ROUTE_SLOT_TILE, cap) == 0
    return pl.pallas_call(
        kern, grid=(b,),
        in_specs=[pl.BlockSpec((n, LANES), lambda i: (i, 0))],
        out_specs=[pl.BlockSpec((None, N_EXPERTS, cap), lambda i: (i, 0, 0)),
                   pl.BlockSpec((None, N_EXPERTS, cap), lambda i: (i, 0, 0))],
        out_shape=[jax.ShapeDtypeStruct((b, N_EXPERTS, cap), jnp.int32),
                   jax.ShapeDtypeStruct((b, N_EXPERTS, cap), F32)],
        scratch_shapes=[pltpu.VMEM((n // chunk, N_EXPERTS, chunk), jnp.int32),
                        pltpu.VMEM((n // chunk, N_EXPERTS, chunk), F32),
                        pltpu.VMEM((n // chunk, N_EXPERTS, LANES), jnp.int32),
                        pltpu.VMEM((N_EXPERTS, cap, LANES), F32)],
        name=name, compiler_params=_compiler_params(("parallel",)))(logits)


COMBINE_ROWS = 256
COMBINE_GROUP = 8


def _combine_kernel(idx_ref, ys_ref, o_ref, *, cap):
    bb, e, j = pl.program_id(0), pl.program_id(1), pl.program_id(2)
    rows = ys_ref.shape[0]

    @pl.when((e == 0) & (j == 0))
    def _():
        o_ref[...] = jnp.zeros(o_ref.shape, F32)

    base = (bb * pl.num_programs(1) + e) * cap + j * rows

    def group(g, carry):
        r = g * COMBINE_GROUP
        ids = [idx_ref[base + r + u] for u in range(COMBINE_GROUP)]
        old = [o_ref[ids[u]] for u in range(COMBINE_GROUP)]
        for u in range(COMBINE_GROUP):
            o_ref[ids[u]] = old[u] + ys_ref[r + u]
        return carry

    lax.fori_loop(0, rows // COMBINE_GROUP, group, 0)


def combine(ys, idx, n, name="moe_combine"):
    e, r, d = ys.shape
    b, _, cap = idx.shape
    sub = d // LANES
    rows = min(COMBINE_ROWS, cap)
    assert cap % rows == 0 and rows % COMBINE_GROUP == 0
    tiles = cap // rows
    grid_spec = pltpu.PrefetchScalarGridSpec(
        num_scalar_prefetch=1, grid=(b, e, tiles),
        in_specs=[pl.BlockSpec((None, rows, sub, LANES), lambda i, k, j, idx_ref: (k, i * tiles + j, 0, 0))],
        out_specs=pl.BlockSpec((None, n, sub, LANES), lambda i, k, j, idx_ref: (i, 0, 0, 0),
                               pipeline_mode=pl.Buffered(1)))
    out = pl.pallas_call(
        functools.partial(_combine_kernel, cap=cap), grid_spec=grid_spec,
        out_shape=jax.ShapeDtypeStruct((b, n, sub, LANES), F32), name=name,
        compiler_params=_compiler_params(("arbitrary", "arbitrary", "arbitrary")))(
            idx.reshape(-1), ys.reshape(e, r, sub, LANES))
    return out.reshape(b, n, d)


def _pad_cols(w, n):
    return jnp.pad(w, ((0, 0), (0, n - w.shape[1])))


def _ssd_project(x2d, g, shift, scale, rows_per_batch, w_main, w_dt, conv_w, conv_b, dt_bias, b, l):
    d_inner = SSM_HEADS * SSM_HEAD_DIM
    main = fused_matmul(x2d, w_main, prologue="norm_mod", g=g, shift=shift, scale=scale,
                        rows_per_batch=rows_per_batch, out_dtype=BF16, tm=1024, name="ssd_in_proj")
    dt_raw = fused_matmul(x2d, w_dt, prologue="norm_mod", g=g, shift=shift, scale=scale,
                          rows_per_batch=rows_per_batch, precise=True, name="ssd_dt_proj")
    xbc = conv_silu(main.reshape(b, l, -1), d_inner, conv_w, conv_b)
    dt = jax.nn.softplus(dt_raw[:, :2 * SSM_HEADS].reshape(b, l, 2, SSM_HEADS) + dt_bias)
    return main, xbc, dt


def _ssd_bidirectional(xbc, dt, a, d_skip, h0_f, h0_b):
    dt_f, dt_b = dt[:, :, 0], dt[:, :, 1]
    y_f, hf = ssd_scan(xbc, dt_f, jnp.swapaxes(dt_f, 1, 2), a[0], h0_f, reverse=False, name="ssd_scan_fwd")
    y, hb = ssd_scan(xbc, dt_b, jnp.swapaxes(dt_b, 1, 2), a[1], h0_b, reverse=True, y_prev=y_f, d_skip=d_skip,
                     name="ssd_scan_bwd")
    return y, hf, hb


def _moe(x2d, g, shift, scale, gate_mod, rows_per_batch, b, n, router_w, w1, w3, w2):
    d = x2d.shape[1]
    cap = EC_CAPACITY_FACTOR * n // N_EXPERTS
    in_pallas = n % ROUTE_CHUNK == 0
    router_w = _pad_cols(router_w, LANES)
    bi = jnp.arange(b)[:, None, None]
    if in_pallas:
        logits = fused_matmul(x2d, router_w, prologue="norm_mod", g=g, shift=shift, scale=scale,
                              rows_per_batch=rows_per_batch, precise=True, name="moe_router")
        idx, gate = route(logits, b, n, cap)
        rows = jnp.swapaxes(idx + bi * n, 0, 1).reshape(-1)
        gate_e = jnp.swapaxes(gate, 0, 1).reshape(N_EXPERTS, b * cap, 1)
        ys = expert_ffn_gather(x2d, rows, g, shift, scale, w1, w3, w2, gate_e, rows_per_batch=cap)
        moe = combine(ys, idx, n)
    else:
        logits, h = fused_matmul(x2d, router_w, prologue="norm_mod", g=g, shift=shift, scale=scale,
                                 rows_per_batch=rows_per_batch, precise=True, emit_h=BF16, name="moe_router_ctx")
        aff = jax.nn.softmax(logits[:, :N_EXPERTS].reshape(b, n, N_EXPERTS), axis=-1)
        gate, idx = lax.top_k(jnp.swapaxes(aff, 1, 2), cap)
        xs = jnp.swapaxes(h.reshape(b, n, d)[bi, idx], 0, 1).reshape(N_EXPERTS, b * cap, d)
        gate_e = jnp.swapaxes(gate, 0, 1).reshape(N_EXPERTS, b * cap, 1)
        ys = expert_ffn(xs, w1, w3, w2, gate_e)
        ys = jnp.swapaxes(ys.reshape(N_EXPERTS, b, cap, d), 0, 1)
        moe = jnp.zeros((b, n, d), F32).at[bi, idx].add(ys)
    x3 = x2d.reshape(b, n, d)
    gm = gate_mod if gate_mod.shape[0] == b else jnp.broadcast_to(gate_mod, (b, 1, d))
    return (x3 + gm * moe).reshape(b * n, d)


def _dft_tables(l, ch):
    def cs(nn):
        idx = jnp.arange(nn)
        ang = ((idx[:, None] * idx[None, :]) % nn).astype(F32) * (2.0 * math.pi / nn)
        s = 1.0 / math.sqrt(nn)
        return jnp.cos(ang) * s, jnp.sin(ang) * s
    cl, sl = cs(l)
    cc, sc = cs(ch)
    return cl, sl, cc, sc


def _fourier(h2d, b, l, out_w, out_b, gate, res, rows_per_batch):
    d = h2d.shape[1]
    ch = d // FNET_GROUPS
    cl, sl, cc, sc = _dft_tables(l, ch)
    eye = jnp.eye(FNET_GROUPS, dtype=F32)
    w_ch = jnp.concatenate([jnp.kron(eye, cc), jnp.kron(eye, sc)], axis=1).astype(BF16)
    pq = fused_matmul(h2d, w_ch, out_dtype=BF16, name="fnet_channel_dft")
    pq = pq.reshape(b, l, 2, d)
    rhs = jnp.transpose(pq, (2, 1, 0, 3)).reshape(2 * l, b * d)
    lhs = jnp.concatenate([cl, -sl], axis=1).astype(BF16)
    mixed = fused_matmul(lhs, rhs, out_dtype=BF16, tk=2048, name="fnet_position_dft")
    mixed = jnp.transpose(mixed.reshape(l, b, d), (1, 0, 2)).reshape(b * l, d)
    return fused_matmul(mixed, out_w.astype(BF16), bias=out_b, gate=gate, res=res,
                        rows_per_batch=rows_per_batch, name="fnet_out_proj")


FFT_N1 = 32
FFT_ROWS = 8
FFT_CH = 128
FFT_UNROLL = 4


def _fnet_kernel(x_ref, cc_ref, sc_ref, kc_ref, ks_ref, mc_ref, ms_ref, o_ref, zr_s, zq_s, ar_s, ai_s):
    n1, nj, rows, ch = zr_s.shape
    n2 = nj * rows
    x = x_ref[...]
    zr_s[...] = _dot(x, cc_ref[...]).reshape(zr_s.shape)
    zq_s[...] = _dot(x, sc_ref[...]).reshape(zq_s.shape)
    kc, ks = kc_ref[...], ks_ref[...]

    def stage1(j, carry):
        p = zr_s[:, j].reshape(n1 * rows, ch).astype(BF16)
        q = zq_s[:, j].reshape(n1 * rows, ch).astype(BF16)
        ar_s[:, j] = (_dot(kc, p) - _dot(ks, q)).reshape(n1, rows, ch)
        ai_s[:, j] = (_dot(kc, q) + _dot(ks, p)).reshape(n1, rows, ch)
        return carry

    lax.fori_loop(0, nj, stage1, 0, unroll=FFT_UNROLL)

    def stage2(k1, carry):
        ar = ar_s[k1].reshape(n2, ch).astype(BF16)
        ai = ai_s[k1].reshape(n2, ch).astype(BF16)
        o_ref[k1] = (_dot(mc_ref[k1], ar) - _dot(ms_ref[k1], ai)).astype(o_ref.dtype)
        return carry

    lax.fori_loop(0, n1, stage2, 0, unroll=FFT_UNROLL)


def _fnet_tables(l, ch):
    n1, n2 = FFT_N1, l // FFT_N1
    def cs(num, den, scale):
        ang = (num % den).astype(F32) * (2.0 * math.pi / den)
        return jnp.cos(ang) * scale, jnp.sin(ang) * scale
    ic = jnp.arange(ch)
    cc, sc = cs(ic[:, None] * ic[None, :], ch, 1.0 / math.sqrt(ch))
    i1 = jnp.arange(n1)
    c1, s1 = cs(i1[:, None] * i1[None, :], n1, 1.0)
    eye = jnp.eye(FFT_ROWS, dtype=F32)
    kc, ks = jnp.kron(c1, eye), jnp.kron(s1, eye)
    k = i1[:, None, None] + n1 * jnp.arange(n2)[None, :, None]
    mc, ms = cs(k * jnp.arange(n2)[None, None, :], l, 1.0 / math.sqrt(l))
    return [t.astype(BF16) for t in (cc, sc, kc, ks, mc, ms)]


def fourier_mix(h, name="fnet_mix"):
    b, l, d = h.shape
    ch = d // FNET_GROUPS
    n1, n2 = FFT_N1, l // FFT_N1
    halves = ch // FFT_CH
    cc, sc, kc, ks, mc, ms = _fnet_tables(l, ch)
    once = pl.Buffered(1)
    const2 = lambda i, g, s: (0, 0)
    return pl.pallas_call(
        _fnet_kernel, grid=(b, FNET_GROUPS, halves),
        in_specs=[pl.BlockSpec((None, l, ch), lambda i, g, s: (i, 0, g)),
                  pl.BlockSpec((ch, FFT_CH), lambda i, g, s: (0, s)),
                  pl.BlockSpec((ch, FFT_CH), lambda i, g, s: (0, s)),
                  pl.BlockSpec((n1 * FFT_ROWS, n1 * FFT_ROWS), const2, pipeline_mode=once),
                  pl.BlockSpec((n1 * FFT_ROWS, n1 * FFT_ROWS), const2, pipeline_mode=once),
                  pl.BlockSpec((n1, n2, n2), lambda i, g, s: (0, 0, 0), pipeline_mode=once),
                  pl.BlockSpec((n1, n2, n2), lambda i, g, s: (0, 0, 0), pipeline_mode=once)],
        out_specs=pl.BlockSpec((None, n1, n2, FFT_CH), lambda i, g, s: (i, 0, 0, g * halves + s)),
        out_shape=jax.ShapeDtypeStruct((b, n1, n2, d), BF16),
        scratch_shapes=[pltpu.VMEM((n1, n2 // FFT_ROWS, FFT_ROWS, FFT_CH), F32) for _ in range(4)],
        name=name,
        compiler_params=_compiler_params(("parallel", "arbitrary", "arbitrary")))(h, cc, sc, kc, ks, mc, ms)


def fnet_out_proj(mixed, w, bias, gate, res, name="fnet_out_proj"):
    b, n1, n2, d = mixed.shape
    n = w.shape[1]
    res3 = res.reshape(b, n2, n1 * n)
    kern = functools.partial(_mm_kernel, prologue=None, precise=False, has_bias=True, has_gate_res=True,
                             emit_h=False, nk=1)
    rows = lambda i, j, k: (i // n1, 0, i % n1)
    out = pl.pallas_call(
        kern, grid=(b * n1, 1, 1),
        in_specs=[pl.BlockSpec((None, None, n2, d), lambda i, j, k: (i // n1, i % n1, 0, 0)),
                  pl.BlockSpec((d, n), lambda i, j, k: (0, 0)),
                  pl.BlockSpec((1, n), lambda i, j, k: (0, 0)),
                  pl.BlockSpec((None, 1, n), lambda i, j, k: (i // n1, 0, 0)),
                  pl.BlockSpec((None, n2, n), rows)],
        out_specs=pl.BlockSpec((None, n2, n), rows),
        out_shape=jax.ShapeDtypeStruct((b, n2, n1 * n), F32), name=name,
        compiler_params=_compiler_params(("parallel", "arbitrary", "arbitrary")))(
            mixed, w, bias.reshape(1, n), gate, res3)
    return out.reshape(b * n2 * n1, n)


CONV_HALO = 16


def _conv_kernel(prev_ref, x_ref, next_ref, w_ref, b_ref, o_ref, *, taps):
    i = pl.program_id(1)
    last = pl.num_programs(1) - 1
    tl = x_ref.shape[0]
    prev = jnp.where(i > 0, prev_ref[...].astype(F32), 0.0)
    nxt = jnp.where(i < last, next_ref[...].astype(F32), 0.0)
    ext = jnp.concatenate([prev, x_ref[...].astype(F32), nxt], axis=0)
    acc = b_ref[...]
    for k in range(taps):
        lo = CONV_HALO - taps // 2 + k
        acc = acc + ext[lo:lo + tl, :] * w_ref[k:k + 1, :]
    o_ref[...] = acc * jax.nn.sigmoid(acc)


def conv_silu(main, col0, conv_w, conv_b, *, tl=512, tc=1024, name="ssd_conv"):
    b, l, _ = main.shape
    taps, c = conv_w.shape
    tl = min(tl, l)
    assert l % tl == 0 and c % tc == 0 and col0 % tc == 0 and tl % CONV_HALO == 0
    cb = col0 // tc
    hb = tl // CONV_HALO
    nh = l // CONV_HALO
    kern = functools.partial(_conv_kernel, taps=taps)
    return pl.pallas_call(
        kern, grid=(b, l // tl, c // tc),
        in_specs=[pl.BlockSpec((None, CONV_HALO, tc), lambda n, i, j: (n, jnp.maximum(i * hb - 1, 0), cb + j)),
                  pl.BlockSpec((None, tl, tc), lambda n, i, j: (n, i, cb + j)),
                  pl.BlockSpec((None, CONV_HALO, tc), lambda n, i, j: (n, jnp.minimum((i + 1) * hb, nh - 1), cb + j)),
                  pl.BlockSpec((taps, tc), lambda n, i, j: (0, j)),
                  pl.BlockSpec((1, tc), lambda n, i, j: (0, j))],
        out_specs=pl.BlockSpec((None, tl, tc), lambda n, i, j: (n, i, j)),
        out_shape=jax.ShapeDtypeStruct((b, l, c), F32), name=name,
        compiler_params=_compiler_params(("parallel", "parallel", "parallel")))(
            main, main, main, conv_w, conv_b.reshape(1, c))


def kernel(x, c, ctx, c_ctx, mod_w, mod_b, norm1_g, norm2_g, ssd_in_w, ssd_conv_w, ssd_conv_b, ssd_dt_bias,
           ssd_a_log, ssd_d, ssd_norm_g, ssd_out_w, fnet_out_w, fnet_out_b, diff_qkv_w, diff_out_w, diff_lambda,
           diff_subln_g, router_w, moe_w1, moe_w3, moe_w2, final_g):
    b, l, d = x.shape
    lc = ctx.shape[1]
    d_inner = SSM_HEADS * SSM_HEAD_DIM

    cond = jnp.concatenate([c, c_ctx[None], jnp.zeros(((8 - (b + 1) % 8) % 8, d), F32)], axis=0)
    cond = jax.nn.silu(cond)
    xl = x.reshape(b * l, d)
    xc = ctx.reshape(b * lc, d)

    for i in range(DEPTH):
        need_ctx = i < DEPTH - 1
        kind, j = i % N_MIXERS, i // N_MIXERS
        mod = fused_matmul(cond, mod_w[i], bias=mod_b[i], tn=1024, name="modulation")
        m_l = [mod[:b, t * d:(t + 1) * d].reshape(b, 1, d) for t in range(6)]
        m_c = [mod[b:b + 1, t * d:(t + 1) * d].reshape(1, 1, d) for t in range(6)]

        if kind == 0:
            w_main = ssd_in_w[j][:, :d_inner + d_inner + 2 * SSM_GROUPS * SSM_STATE].astype(BF16)
            w_dt = _pad_cols(ssd_in_w[j][:, w_main.shape[1]:], LANES)
            a = -jnp.exp(ssd_a_log[j].astype(F32))
            h0 = jnp.zeros((b, SSM_GROUPS, SSM_STATE, GROUP_WIDTH), F32)
            z_c, xbc_c, dt_c = _ssd_project(xc, norm1_g[i], m_c[0], m_c[1], None, w_main, w_dt,
                                            ssd_conv_w[j], ssd_conv_b[j], ssd_dt_bias[j], b, lc)
            y_c, hf, hb = _ssd_bidirectional(xbc_c, dt_c, a, ssd_d[j], h0, h0)
            z_l, xbc_l, dt_l = _ssd_project(xl, norm1_g[i], m_l[0], m_l[1], l, w_main, w_dt,
                                            ssd_conv_w[j], ssd_conv_b[j], ssd_dt_bias[j], b, l)
            y_l, _, _ = _ssd_bidirectional(xbc_l, dt_l, a, ssd_d[j], hf, hb)
            w_out = ssd_out_w[j].astype(BF16)
            xl = fused_matmul(y_l.reshape(b * l, d_inner), w_out, prologue="gated_norm",
                              z=z_l, g=ssd_norm_g[j], gate=m_l[2], res=xl,
                              rows_per_batch=l, name="ssd_out_proj")
            if need_ctx:
                xc = fused_matmul(y_c.reshape(b * lc, d_inner), w_out, prologue="gated_norm",
                                  z=z_c, g=ssd_norm_g[j], gate=m_c[2], res=xc,
                                  name="ssd_out_proj_ctx")
        elif kind == 1:
            h_l = norm_modulate(xl, norm1_g[i], m_l[0], m_l[1], rows_per_batch=l, out_dtype=BF16, name="fnet_norm")
            mixed = fourier_mix(h_l.reshape(b, l, d))
            xl = fnet_out_proj(mixed, fnet_out_w[j].astype(BF16), fnet_out_b[j], m_l[2], xl)
            if need_ctx:
                h_c = norm_modulate(xc, norm1_g[i], m_c[0], m_c[1], out_dtype=BF16, name="fnet_norm_ctx")
                xc = _fourier(h_c, b, lc, fnet_out_w[j], fnet_out_b[j], m_c[2], xc, None)
        else:
            lambda_init = 0.8 - 0.6 * math.exp(-0.3 * i)
            lp = diff_lambda[j].astype(F32)
            lam = jnp.exp(jnp.sum(lp[0] * lp[1])) - jnp.exp(jnp.sum(lp[2] * lp[3])) + lambda_init
            w_qkv = diff_qkv_w[j].astype(BF16)
            w_out = diff_out_w[j].astype(BF16)
            qkv_c = fused_matmul(xc, w_qkv, prologue="norm_mod", g=norm1_g[i], shift=m_c[0], scale=m_c[1],
                                 name="diff_qkv_ctx").reshape(b, lc, 3 * d)
            qkv_l = fused_matmul(xl, w_qkv, prologue="norm_mod", g=norm1_g[i], shift=m_l[0], scale=m_l[1],
                                 rows_per_batch=l, tm=1024, name="diff_qkv").reshape(b, l, 3 * d)
            o_l = diff_attention(qkv_l, qkv_l, qkv_c, lam, diff_subln_g[j], 1.0 - lambda_init, _rope_tables(l),
                                 name="diff_attention", **ATTN_TILES)
            xl = fused_matmul(o_l.reshape(b * l, d), w_out, gate=m_l[2], res=xl, rows_per_batch=l,
                              name="diff_out_proj")
            if need_ctx:
                o_c = diff_attention(qkv_c, None, qkv_c, lam, diff_subln_g[j], 1.0 - lambda_init, None,
                                     name="diff_attention_ctx", **ATTN_TILES_CTX)
                xc = fused_matmul(o_c.reshape(b * lc, d), w_out, gate=m_c[2], res=xc, name="diff_out_proj_ctx")

        w1, w3, w2 = (moe_w1, i), (moe_w3, i), (moe_w2, i)
        xl = _moe(xl, norm2_g[i], m_l[3], m_l[4], m_l[5], l, b, l, router_w[i], w1, w3, w2)
        if need_ctx:
            xc = _moe(xc, norm2_g[i], m_c[3], m_c[4], m_c[5], None, b, lc, router_w[i], w1, w3, w2)

    return norm_modulate(xl, final_g, name="final_norm").reshape(b, l, d)
```

```python
import functools
import math

import jax
import jax.numpy as jnp
from jax import lax
from jax.experimental import pallas as pl
from jax.experimental.pallas import tpu as pltpu

F32 = jnp.float32
BF16 = jnp.bfloat16

EPS = 1e-6
DEPTH = 4
N_MIXERS = 3
GRID_W = 64
ROPE_THETA = 10000.0

SSM_HEAD_DIM = 64
SSM_HEADS = 32
SSM_GROUPS = 8
SSM_STATE = 128
SSM_CHUNK = 128
SSM_CONV = 5
HEADS_PER_GROUP = SSM_HEADS // SSM_GROUPS
GROUP_WIDTH = HEADS_PER_GROUP * SSM_HEAD_DIM

FNET_GROUPS = 4

DIFF_HEADS = 8
DIFF_HEAD_DIM = 64
DIFF_V_DIM = 2 * DIFF_HEAD_DIM
DIFF_SCALE = DIFF_HEAD_DIM ** -0.5
ROPE_PAIRS = DIFF_HEAD_DIM // 4

N_EXPERTS = 16
EC_CAPACITY_FACTOR = 2

LANES = 128
VMEM_LIMIT_BYTES = 56 * 1024 * 1024


def _compiler_params(semantics):
    return pltpu.CompilerParams(dimension_semantics=semantics, vmem_limit_bytes=VMEM_LIMIT_BYTES)


def _split_bf16(x, parts):
    out = []
    r = x
    for _ in range(parts):
        p = r.astype(BF16)
        out.append(p)
        r = r - p.astype(F32)
    return out


def _dot(a, b):
    return jnp.dot(a, b, preferred_element_type=F32)


def _dot_nt(a, b):
    return lax.dot_general(a, b, (((1,), (1,)), ((), ())), preferred_element_type=F32)


def _dot_exact_rhs(x, m_bf16, parts=3, nt=False):
    f = _dot_nt if nt else _dot
    acc = None
    for p in _split_bf16(x, parts):
        t = f(p, m_bf16)
        acc = t if acc is None else acc + t
    return acc


def _dot_exact_lhs(m_bf16, x, parts=3):
    acc = None
    for p in _split_bf16(x, parts):
        t = _dot(m_bf16, p)
        acc = t if acc is None else acc + t
    return acc


def _mm_kernel(*refs, prologue, precise, has_bias, has_gate_res, emit_h, nk):
    it = iter(refs)
    x_ref = next(it)
    z_ref = g_ref = sh_ref = sc_ref = None
    if prologue == "norm_mod":
        g_ref, sh_ref, sc_ref = next(it), next(it), next(it)
    elif prologue == "gated_norm":
        z_ref, g_ref = next(it), next(it)
    w_ref = next(it)
    b_ref = next(it) if has_bias else None
    gate_ref = res_ref = None
    if has_gate_res:
        gate_ref, res_ref = next(it), next(it)
    o_ref = next(it)
    hout_ref = next(it) if emit_h else None
    h_ref = next(it) if prologue else None
    acc_ref = next(it) if nk > 1 else None

    j = pl.program_id(1)
    k = pl.program_id(2)

    if prologue:
        @pl.when(j == 0)
        def _():
            x = x_ref[...].astype(F32)
            if prologue == "gated_norm":
                z = z_ref[...].astype(F32)
                x = x * (z * jax.nn.sigmoid(z))
            ms = jnp.mean(x * x, axis=-1, keepdims=True)
            y = x * lax.rsqrt(ms + EPS) * g_ref[...]
            if prologue == "norm_mod":
                y = y * (1.0 + sc_ref[...]) + sh_ref[...]
            h_ref[...] = y.astype(h_ref.dtype)
            if emit_h:
                hout_ref[...] = y.astype(hout_ref.dtype)
        lhs = h_ref[...]
    else:
        lhs = x_ref[...]

    w = w_ref[...]
    if precise:
        l_hi, l_lo = _split_bf16(lhs.astype(F32), 2)
        w_hi, w_lo = _split_bf16(w.astype(F32), 2)
        part = _dot(l_hi, w_hi) + (_dot(l_lo, w_hi) + _dot(l_hi, w_lo))
    else:
        part = _dot(lhs.astype(BF16), w.astype(BF16))

    def finish(acc):
        if has_bias:
            acc = acc + b_ref[...]
        if has_gate_res:
            acc = res_ref[...].astype(F32) + gate_ref[...] * acc
        o_ref[...] = acc.astype(o_ref.dtype)

    if nk == 1:
        finish(part)
    else:
        @pl.when(k == 0)
        def _():
            acc_ref[...] = part

        @pl.when(k > 0)
        def _():
            acc_ref[...] += part

        @pl.when(k == nk - 1)
        def _():
            finish(acc_ref[...])


def fused_matmul(x, w, *, prologue=None, g=None, shift=None, scale=None, z=None, bias=None,
                 gate=None, res=None, rows_per_batch=None, precise=False, out_dtype=F32,
                 emit_h=None, tm=512, tn=1024, tk=None, name="fused_matmul"):
    m, kdim = x.shape
    n = w.shape[1]
    rows_per_batch = m if rows_per_batch is None else rows_per_batch
    tm = min(tm, m, rows_per_batch)
    tn = min(tn, n)
    tk = kdim if tk is None else min(tk, kdim)
    assert m % tm == 0 and n % tn == 0 and kdim % tk == 0, (x.shape, w.shape, tm, tn, tk)
    nk = kdim // tk
    assert not (prologue and nk > 1)
    assert rows_per_batch % tm == 0
    tiles_per_batch = rows_per_batch // tm

    def bidx(i):
        return i // tiles_per_batch

    args = [x]
    in_specs = [pl.BlockSpec((tm, tk), lambda i, j, k: (i, k))]
    if prologue == "norm_mod":
        args += [g.reshape(1, kdim), shift, scale]
        in_specs += [pl.BlockSpec((1, kdim), lambda i, j, k: (0, 0)),
                     pl.BlockSpec((None, 1, kdim), lambda i, j, k: (bidx(i), 0, 0)),
                     pl.BlockSpec((None, 1, kdim), lambda i, j, k: (bidx(i), 0, 0))]
    elif prologue == "gated_norm":
        args += [z, g.reshape(1, kdim)]
        in_specs += [pl.BlockSpec((tm, kdim), lambda i, j, k: (i, 0)),
                     pl.BlockSpec((1, kdim), lambda i, j, k: (0, 0))]
    args.append(w)
    in_specs.append(pl.BlockSpec((tk, tn), lambda i, j, k: (k, j)))
    if bias is not None:
        args.append(bias.reshape(1, n))
        in_specs.append(pl.BlockSpec((1, tn), lambda i, j, k: (0, j)))
    if gate is not None:
        args += [gate, res]
        in_specs += [pl.BlockSpec((None, 1, tn), lambda i, j, k: (bidx(i), 0, j)),
                     pl.BlockSpec((tm, tn), lambda i, j, k: (i, j))]

    out_shape = [jax.ShapeDtypeStruct((m, n), out_dtype)]
    out_specs = [pl.BlockSpec((tm, tn), lambda i, j, k: (i, j))]
    if emit_h is not None:
        out_shape.append(jax.ShapeDtypeStruct((m, kdim), emit_h))
        out_specs.append(pl.BlockSpec((tm, kdim), lambda i, j, k: (i, 0)))

    scratch = []
    if prologue:
        scratch.append(pltpu.VMEM((tm, kdim), F32 if precise else BF16))
    if nk > 1:
        scratch.append(pltpu.VMEM((tm, tn), F32))

    kern = functools.partial(_mm_kernel, prologue=prologue, precise=precise, has_bias=bias is not None,
                             has_gate_res=gate is not None, emit_h=emit_h is not None, nk=nk)
    outs = pl.pallas_call(
        kern, grid=(m // tm, n // tn, nk), in_specs=in_specs, out_specs=out_specs, out_shape=out_shape,
        scratch_shapes=scratch, name=name,
        compiler_params=_compiler_params(("parallel", "arbitrary", "arbitrary")))(*args)
    return outs if emit_h is not None else outs[0]


def _norm_kernel(x_ref, g_ref, sh_ref, sc_ref, o_ref, *, modulated):
    x = x_ref[...].astype(F32)
    y = x * lax.rsqrt(jnp.mean(x * x, axis=-1, keepdims=True) + EPS) * g_ref[...]
    if modulated:
        y = y * (1.0 + sc_ref[...]) + sh_ref[...]
    o_ref[...] = y.astype(o_ref.dtype)


def norm_modulate(x, g, shift=None, scale=None, *, rows_per_batch=None, out_dtype=F32, tm=512, name="norm"):
    m, d = x.shape
    modulated = shift is not None
    if not modulated:
        shift = scale = jnp.zeros((1, 1, d), F32)
    rows_per_batch = m if rows_per_batch is None else rows_per_batch
    tm = min(tm, m, rows_per_batch)
    assert m % tm == 0 and rows_per_batch % tm == 0
    tiles_per_batch = rows_per_batch // tm
    per_batch = pl.BlockSpec((None, 1, d), lambda i: (i // tiles_per_batch, 0, 0))
    return pl.pallas_call(
        functools.partial(_norm_kernel, modulated=modulated), grid=(m // tm,),
        in_specs=[pl.BlockSpec((tm, d), lambda i: (i, 0)), pl.BlockSpec((1, d), lambda i: (0, 0)),
                  per_batch, per_batch],
        out_specs=pl.BlockSpec((tm, d), lambda i: (i, 0)),
        out_shape=jax.ShapeDtypeStruct((m, d), out_dtype), name=name,
        compiler_params=_compiler_params(("parallel",)))(x, g.reshape(1, d), shift, scale)


def _ssd_kernel(*refs, reverse, combine, q):
    it = iter(refs)
    xs_ref, bm_ref, cm_ref, dt_ref, dtt_ref, a_ref, at_ref, h0_ref = (next(it) for _ in range(8))
    yprev_ref = dskip_ref = None
    if combine:
        yprev_ref, dskip_ref = next(it), next(it)
    y_ref, hfin_ref, state_ref = next(it), next(it), next(it)

    c = pl.program_id(1)

    @pl.when(c == 0)
    def _():
        state_ref[...] = h0_ref[...]

    ii = lax.broadcasted_iota(jnp.int32, (q, q), 0)
    jj = lax.broadcasted_iota(jnp.int32, (q, q), 1)
    mask = (ii <= jj) if reverse else (ii >= jj)
    tri = jnp.where(mask, 1.0, 0.0).astype(BF16)

    dt = dt_ref[...]
    dta = dt * a_ref[...]
    dta_t = dtt_ref[...] * at_ref[...]
    a_cs = _dot_exact_lhs(tri, dta)
    a_cs_t = _dot_exact_rhs(dta_t, tri, nt=True)
    edge = 0 if reverse else q - 1
    tot = a_cs[edge:edge + 1, :]
    to_end = jnp.exp(tot - a_cs)
    into = jnp.exp(a_cs)
    chunk_decay = jnp.exp(tot)

    hh = lax.broadcasted_iota(jnp.int32, (SSM_HEADS, SSM_HEADS * SSM_HEAD_DIM), 0)
    cc = lax.broadcasted_iota(jnp.int32, (SSM_HEADS, SSM_HEADS * SSM_HEAD_DIM), 1)
    expand = jnp.where(cc // SSM_HEAD_DIM == hh, 1.0, 0.0).astype(BF16)
    dt_x = _dot_exact_rhs(dt, expand, parts=1)
    to_end_x = _dot_exact_rhs(to_end, expand, parts=1)
    into_x = _dot_exact_rhs(into, expand, parts=1)
    decay_x = _dot_exact_rhs(jnp.broadcast_to(chunk_decay, (8, SSM_HEADS)), expand, parts=3)[0:1, :]

    for g in range(SSM_GROUPS):
        c0 = g * GROUP_WIDTH
        xs_g = xs_ref[:, c0:c0 + GROUP_WIDTH]
        bm_g = bm_ref[:, g * SSM_STATE:(g + 1) * SSM_STATE]
        cm_g = cm_ref[:, g * SSM_STATE:(g + 1) * SSM_STATE].astype(BF16)
        xdt_g = xs_g * dt_x[:, c0:c0 + GROUP_WIDTH]
        xdt_b = xdt_g.astype(BF16)
        cb = _dot_nt(cm_g, bm_g.astype(BF16))
        st = state_ref[g]
        y_g = _dot(cm_g, st.astype(BF16)) * into_x[:, c0:c0 + GROUP_WIDTH]
        if combine:
            y_g = y_g + yprev_ref[:, c0:c0 + GROUP_WIDTH] + dskip_ref[:, c0:c0 + GROUP_WIDTH] * xs_g
        y_ref[:, c0:c0 + GROUP_WIDTH] = y_g
        for r in range(HEADS_PER_GROUP):
            h = g * HEADS_PER_GROUP + r
            seg = a_cs[:, h:h + 1] - a_cs_t[h:h + 1, :]
            lmat = (jnp.exp(jnp.where(mask, seg, -1e30)) * cb).astype(BF16)
            lo = c0 + r * SSM_HEAD_DIM
            y_ref[:, lo:lo + SSM_HEAD_DIM] += _dot(lmat, xdt_b[:, r * SSM_HEAD_DIM:(r + 1) * SSM_HEAD_DIM])
        xdtw = (xdt_g * to_end_x[:, c0:c0 + GROUP_WIDTH]).astype(BF16)
        new = _dot(bm_g.T.astype(BF16), xdtw)
        state_ref[g] = st * decay_x[:, c0:c0 + GROUP_WIDTH] + new

    @pl.when(c == pl.num_programs(1) - 1)
    def _():
        hfin_ref[...] = state_ref[...]


def ssd_scan(xbc, dt, dt_t, a, h0, *, reverse, y_prev=None, d_skip=None, name="ssd_scan"):
    b, l, _ = xbc.shape
    q = SSM_CHUNK
    nc = l // q
    d_inner = SSM_HEADS * SSM_HEAD_DIM
    nbc = SSM_GROUPS * SSM_STATE
    combine = y_prev is not None

    def cidx(c):
        return (nc - 1 - c) if reverse else c

    xs_blocks = d_inner // d_inner
    in_specs = [
        pl.BlockSpec((None, q, d_inner), lambda i, c: (i, cidx(c), 0)),
        pl.BlockSpec((None, q, nbc), lambda i, c: (i, cidx(c), d_inner // nbc)),
        pl.BlockSpec((None, q, nbc), lambda i, c: (i, cidx(c), d_inner // nbc + 1)),
        pl.BlockSpec((None, q, SSM_HEADS), lambda i, c: (i, cidx(c), 0)),
        pl.BlockSpec((None, SSM_HEADS, q), lambda i, c: (i, 0, cidx(c))),
        pl.BlockSpec((1, SSM_HEADS), lambda i, c: (0, 0)),
        pl.BlockSpec((SSM_HEADS, 1), lambda i, c: (0, 0)),
        pl.BlockSpec((None, SSM_GROUPS, SSM_STATE, GROUP_WIDTH), lambda i, c: (i, 0, 0, 0)),
    ]
    del xs_blocks
    args = [xbc, xbc, xbc, dt, dt_t, a.reshape(1, SSM_HEADS), a.reshape(SSM_HEADS, 1), h0]
    if combine:
        in_specs += [pl.BlockSpec((None, q, d_inner), lambda i, c: (i, cidx(c), 0)),
                     pl.BlockSpec((1, d_inner), lambda i, c: (0, 0))]
        args += [y_prev, jnp.repeat(d_skip, SSM_HEAD_DIM).reshape(1, d_inner)]
    out_shape = [jax.ShapeDtypeStruct((b, l, d_inner), F32),
                 jax.ShapeDtypeStruct((b, SSM_GROUPS, SSM_STATE, GROUP_WIDTH), F32)]
    out_specs = [pl.BlockSpec((None, q, d_inner), lambda i, c: (i, cidx(c), 0)),
                 pl.BlockSpec((None, SSM_GROUPS, SSM_STATE, GROUP_WIDTH), lambda i, c: (i, 0, 0, 0))]
    kern = functools.partial(_ssd_kernel, reverse=reverse, combine=combine, q=q)
    return pl.pallas_call(
        kern, grid=(b, nc), in_specs=in_specs, out_specs=out_specs, out_shape=out_shape,
        scratch_shapes=[pltpu.VMEM((SSM_GROUPS, SSM_STATE, GROUP_WIDTH), F32)], name=name,
        compiler_params=_compiler_params(("parallel", "arbitrary")))(*args)


def _rope(x, cos, sin_up, sin_dn):
    return (x * cos + pltpu.roll(x, ROPE_PAIRS, 1) * sin_up
            + pltpu.roll(x, LANES - ROPE_PAIRS, 1) * sin_dn)


LOG2E = 1.4426950408889634


def _attn_kernel(*refs, n_lat, n_ctx, tk, rope, out_scale):
    it = iter(refs)
    q_ref = next(it)
    kl_ref = vl_ref = None
    if n_lat:
        kl_ref, vl_ref = next(it), next(it)
    kc_ref, vc_ref = next(it), next(it)
    if rope:
        cq_ref, suq_ref, sdq_ref, ck_ref, suk_ref, sdk_ref = (next(it) for _ in range(6))
    lam_ref, g_ref, o_ref, k_s, vt_s, m_s, l_s, acc_s, s_s = (next(it) for _ in range(9))

    qi = pl.program_id(2)
    tq = q_ref.shape[0]
    n_tiles = (n_lat + n_ctx) // tk

    @pl.when(qi == 0)
    def _():
        if n_lat:
            k = kl_ref[...]
            if rope:
                k = _rope(k, ck_ref[...], suk_ref[...], sdk_ref[...])
            k_s[0:n_lat, :] = k.astype(BF16)
        k_s[n_lat:n_lat + n_ctx, :] = kc_ref[...].astype(BF16)
        for t in range(n_tiles):
            lo, hi = t * tk, (t + 1) * tk
            pieces = []
            if lo < n_lat:
                pieces.append(vl_ref[lo:min(hi, n_lat), :])
            if hi > n_lat:
                pieces.append(vc_ref[max(lo, n_lat) - n_lat:hi - n_lat, :])
            v = pieces[0] if len(pieces) == 1 else jnp.concatenate(pieces, axis=0)
            vt_s[t] = v.T.astype(BF16)

    q = q_ref[...] * (DIFF_SCALE * LOG2E)
    if rope:
        q = _rope(q, cq_ref[...], suq_ref[...], sdq_ref[...])
    qt = q.T
    row = lax.broadcasted_iota(jnp.int32, qt.shape, 0)
    wq = jnp.concatenate([jnp.where(row < DIFF_HEAD_DIM, qt, 0.0),
                          jnp.where(row >= DIFF_HEAD_DIM, qt, 0.0)], axis=1).astype(BF16)

    m_s[...] = jnp.full(m_s.shape, -1e30, F32)
    l_s[...] = jnp.zeros(l_s.shape, F32)
    acc_s[...] = jnp.zeros(acc_s.shape, F32)

    def scores(t):
        off = pl.multiple_of(t * tk, tk)
        return _dot(k_s[pl.ds(off, tk), :], wq)

    def absorb(buf, t):
        s = s_s[buf]
        m_old = m_s[...]
        m_new = jnp.maximum(m_old, jnp.max(s, axis=0, keepdims=True))
        alpha = jnp.exp2(m_old - m_new)
        p = jnp.exp2(s - m_new)
        l_s[...] = alpha * l_s[...] + jnp.sum(p, axis=0, keepdims=True)
        acc_s[...] = alpha * acc_s[...] + _dot(vt_s[t], p.astype(BF16))
        m_s[...] = m_new

    s_s[0] = scores(0)

    def body(i, carry):
        t0 = 2 * i
        s_s[1] = scores(t0 + 1)
        absorb(0, t0)
        s_s[0] = scores(t0 + 2)
        absorb(1, t0 + 1)
        return carry

    lax.fori_loop(0, n_tiles // 2 - 1, body, 0)
    s_s[1] = scores(n_tiles - 1)
    absorb(0, n_tiles - 2)
    absorb(1, n_tiles - 1)

    o_t = acc_s[...] / l_s[...]
    o = (o_t[:, :tq] - lam_ref[0:1, 0:1] * o_t[:, tq:]).T
    ms = jnp.mean(o * o, axis=-1, keepdims=True)
    o_ref[...] = o * lax.rsqrt(ms + EPS) * (g_ref[...] * out_scale)


ATTN_TILES = dict(tq=512, tk=1408)
ATTN_TILES_CTX = dict(tq=256, tk=128)


def diff_attention(qkv_q, qkv_lat, qkv_ctx, lam, subln_g, out_scale, rope_tabs, *, tq, tk, name):
    b, lq, d3 = qkv_q.shape
    d = d3 // 3
    nh = d // DIFF_V_DIM
    n_lat = 0 if qkv_lat is None else qkv_lat.shape[1]
    n_ctx = qkv_ctx.shape[1]
    rope = rope_tabs is not None
    tq = min(tq, lq)
    tk = min(tk, n_lat + n_ctx)
    assert lq % tq == 0 and (n_lat + n_ctx) % (2 * tk) == 0

    args = [qkv_q]
    in_specs = [pl.BlockSpec((None, tq, DIFF_V_DIM), lambda i, h, t: (i, t, h))]
    if n_lat:
        args += [qkv_lat, qkv_lat]
        in_specs += [pl.BlockSpec((None, n_lat, DIFF_V_DIM), lambda i, h, t: (i, 0, nh + h)),
                     pl.BlockSpec((None, n_lat, DIFF_V_DIM), lambda i, h, t: (i, 0, 2 * nh + h))]
    args += [qkv_ctx, qkv_ctx]
    in_specs += [pl.BlockSpec((None, n_ctx, DIFF_V_DIM), lambda i, h, t: (i, 0, nh + h)),
                 pl.BlockSpec((None, n_ctx, DIFF_V_DIM), lambda i, h, t: (i, 0, 2 * nh + h))]
    if rope:
        args += list(rope_tabs) + list(rope_tabs)
        in_specs += [pl.BlockSpec((tq, LANES), lambda i, h, t: (t, 0))] * 3
        in_specs += [pl.BlockSpec((n_lat, LANES), lambda i, h, t: (0, 0), pipeline_mode=pl.Buffered(1))] * 3
    args += [jnp.broadcast_to(lam.astype(F32).reshape(1, 1), (8, LANES)), subln_g.reshape(1, DIFF_V_DIM)]
    in_specs += [pl.BlockSpec((8, LANES), lambda i, h, t: (0, 0)),
                 pl.BlockSpec((1, DIFF_V_DIM), lambda i, h, t: (0, 0))]

    kern = functools.partial(_attn_kernel, n_lat=n_lat, n_ctx=n_ctx, tk=tk, rope=rope, out_scale=out_scale)
    return pl.pallas_call(
        kern, grid=(b, nh, lq // tq), in_specs=in_specs,
        out_specs=pl.BlockSpec((None, tq, DIFF_V_DIM), lambda i, h, t: (i, t, h)),
        out_shape=jax.ShapeDtypeStruct((b, lq, d), F32),
        scratch_shapes=[pltpu.VMEM((n_lat + n_ctx, DIFF_V_DIM), BF16),
                        pltpu.VMEM(((n_lat + n_ctx) // tk, DIFF_V_DIM, tk), BF16),
                        pltpu.VMEM((1, 2 * tq), F32), pltpu.VMEM((1, 2 * tq), F32),
                        pltpu.VMEM((DIFF_V_DIM, 2 * tq), F32),
                        pltpu.VMEM((2, tk, 2 * tq), F32)],
        name=name,
        compiler_params=_compiler_params(("parallel", "parallel", "arbitrary")))(*args)


def _rope_tables(l):
    rows = l // GRID_W
    row = jnp.repeat(jnp.arange(rows), GRID_W).astype(F32)
    col = jnp.tile(jnp.arange(GRID_W), rows).astype(F32)
    inv = ROPE_THETA ** (-jnp.arange(ROPE_PAIRS, dtype=F32) / ROPE_PAIRS)
    lane = jnp.arange(LANES)
    within = lane % DIFF_HEAD_DIM
    axis = within // (2 * ROPE_PAIRS)
    second = (within // ROPE_PAIRS) % 2
    pos = jnp.where(axis[None, :] == 0, row[:, None], col[:, None])
    ang = pos * inv[within % ROPE_PAIRS][None, :]
    cos, sin = jnp.cos(ang), jnp.sin(ang)
    sin_up = jnp.where(second[None, :] == 1, sin, 0.0)
    sin_dn = jnp.where(second[None, :] == 0, -sin, 0.0)
    return cos, sin_up, sin_dn


FFN_F_SPLIT = 2


def _ffn_kernel(x_ref, w1_ref, w3_ref, w2_ref, gate_ref, o_ref, w1_s, w3_s, w2_s):
    @pl.when(pl.program_id(1) == 0)
    def _():
        w1_s[...] = w1_ref[...].astype(BF16)
        w3_s[...] = w3_ref[...].astype(BF16)
        w2_s[...] = w2_ref[...].astype(BF16)

    x = x_ref[...]
    fs = w1_s.shape[1] // FFN_F_SPLIT
    acc = None
    for s in range(FFN_F_SPLIT):
        h1 = _dot(x, w1_s[:, s * fs:(s + 1) * fs])
        h3 = _dot(x, w3_s[:, s * fs:(s + 1) * fs])
        hid = (h1 * jax.nn.sigmoid(h1) * h3).astype(BF16)
        part = _dot(hid, w2_s[s * fs:(s + 1) * fs, :])
        acc = part if acc is None else acc + part
    o_ref[...] = acc * gate_ref[...]


def _ffn_gather_kernel(rows_ref, x_hbm, g_ref, sh_ref, sc_ref, w1_ref, w3_ref, w2_ref, gate_ref, o_ref,
                       xbuf, sem, w1_s, w3_s, w2_s):
    e, t = pl.program_id(0), pl.program_id(1)
    tiles = pl.num_programs(1)
    tm = xbuf.shape[1]
    step = e * tiles + t
    last = pl.num_programs(0) * tiles - 1
    slot = lax.rem(step, 2)

    def gather(step_id, slot_id):
        for i in range(tm):
            row = rows_ref[step_id * tm + i]
            pltpu.make_async_copy(x_hbm.at[pl.ds(row, 1), :], xbuf.at[slot_id, pl.ds(i, 1), :],
                                  sem.at[slot_id]).start()

    @pl.when(step == 0)
    def _():
        gather(0, 0)

    @pl.when(t == 0)
    def _():
        w1_s[...] = w1_ref[...].astype(BF16)
        w3_s[...] = w3_ref[...].astype(BF16)
        w2_s[...] = w2_ref[...].astype(BF16)

    pltpu.make_async_copy(x_hbm.at[pl.ds(0, tm), :], xbuf.at[slot], sem.at[slot]).wait()
    xr = xbuf[slot]
    ms = jnp.mean(xr * xr, axis=-1, keepdims=True)
    x = (xr * lax.rsqrt(ms + EPS) * g_ref[...] * (1.0 + sc_ref[...]) + sh_ref[...]).astype(BF16)

    gather(jnp.where(step < last, step + 1, 0), 1 - slot)

    fs = w1_s.shape[1] // FFN_F_SPLIT
    acc = None
    for s in range(FFN_F_SPLIT):
        h1 = _dot(x, w1_s[:, s * fs:(s + 1) * fs])
        h3 = _dot(x, w3_s[:, s * fs:(s + 1) * fs])
        hid = (h1 * jax.nn.sigmoid(h1) * h3).astype(BF16)
        part = _dot(hid, w2_s[s * fs:(s + 1) * fs, :])
        acc = part if acc is None else acc + part
    o_ref[...] = acc * gate_ref[...]

    @pl.when(step == last)
    def _():
        pltpu.make_async_copy(x_hbm.at[pl.ds(0, tm), :], xbuf.at[1 - slot], sem.at[1 - slot]).wait()


def _with_layer(w):
    return w if isinstance(w, tuple) else (w[None], 0)


def expert_ffn_gather(x2d, rows, g, shift, scale, w1, w3, w2, gate, *, rows_per_batch, tm=512,
                      name="expert_ffn_gather"):
    d = x2d.shape[1]
    (w1, l1), (w3, l3), (w2, l2) = _with_layer(w1), _with_layer(w3), _with_layer(w2)
    _, e, _, f = w1.shape
    r = rows.shape[0] // e
    tm = min(tm, r, rows_per_batch)
    assert r % tm == 0 and rows_per_batch % tm == 0 and f % FFN_F_SPLIT == 0
    tiles = r // tm
    tiles_per_batch = rows_per_batch // tm
    nb = shift.shape[0]
    once = pl.Buffered(1)
    bsel = lambda i, t, rows_ref: ((t // tiles_per_batch) % nb, 0, 0)
    grid_spec = pltpu.PrefetchScalarGridSpec(
        num_scalar_prefetch=1, grid=(e, tiles),
        in_specs=[pl.BlockSpec(memory_space=pl.ANY),
                  pl.BlockSpec((1, d), lambda i, t, rows_ref: (0, 0)),
                  pl.BlockSpec((None, 1, d), bsel),
                  pl.BlockSpec((None, 1, d), bsel),
                  pl.BlockSpec((None, None, d, f), lambda i, t, rows_ref: (l1, i, 0, 0), pipeline_mode=once),
                  pl.BlockSpec((None, None, d, f), lambda i, t, rows_ref: (l3, i, 0, 0), pipeline_mode=once),
                  pl.BlockSpec((None, None, f, d), lambda i, t, rows_ref: (l2, i, 0, 0), pipeline_mode=once),
                  pl.BlockSpec((None, tm, 1), lambda i, t, rows_ref: (i, t, 0))],
        out_specs=pl.BlockSpec((None, tm, d), lambda i, t, rows_ref: (i, t, 0)),
        scratch_shapes=[pltpu.VMEM((2, tm, d), F32), pltpu.SemaphoreType.DMA((2,)),
                        pltpu.VMEM((d, f), BF16), pltpu.VMEM((d, f), BF16), pltpu.VMEM((f, d), BF16)])
    return pl.pallas_call(
        _ffn_gather_kernel, grid_spec=grid_spec,
        out_shape=jax.ShapeDtypeStruct((e, r, d), F32), name=name,
        compiler_params=_compiler_params(("arbitrary", "arbitrary")))(
            rows, x2d, g.reshape(1, d), shift, scale, w1, w3, w2, gate)


def expert_ffn(xs, w1, w3, w2, gate, *, tm=512, name="expert_ffn"):
    e, r, d = xs.shape
    (w1, l1), (w3, l3), (w2, l2) = _with_layer(w1), _with_layer(w3), _with_layer(w2)
    f = w1.shape[3]
    tm = min(tm, r)
    assert r % tm == 0 and f % FFN_F_SPLIT == 0
    once = pl.Buffered(1)
    return pl.pallas_call(
        _ffn_kernel, grid=(e, r // tm),
        in_specs=[pl.BlockSpec((None, tm, d), lambda i, t: (i, t, 0)),
                  pl.BlockSpec((None, None, d, f), lambda i, t: (l1, i, 0, 0), pipeline_mode=once),
                  pl.BlockSpec((None, None, d, f), lambda i, t: (l3, i, 0, 0), pipeline_mode=once),
                  pl.BlockSpec((None, None, f, d), lambda i, t: (l2, i, 0, 0), pipeline_mode=once),
                  pl.BlockSpec((None, tm, 1), lambda i, t: (i, t, 0))],
        out_specs=pl.BlockSpec((None, tm, d), lambda i, t: (i, t, 0)),
        out_shape=jax.ShapeDtypeStruct((e, r, d), F32),
        scratch_shapes=[pltpu.VMEM((d, f), BF16), pltpu.VMEM((d, f), BF16), pltpu.VMEM((f, d), BF16)],
        name=name,
        compiler_params=_compiler_params(("arbitrary", "arbitrary")))(xs, w1, w3, w2, gate)


ROUTE_CHUNK = 1024


def _lane_prefix_sum(x):
    n = x.shape[-1]
    lane = lax.broadcasted_iota(jnp.int32, x.shape, x.ndim - 1)
    shift = 1
    while shift < n:
        x = x + jnp.where(lane >= shift, pltpu.roll(x, shift, x.ndim - 1), 0)
        shift *= 2
    return x


def _route_kernel(logit_ref, idx_ref, gate_ref, key_s, aff_s, cnt_s, acc_s, *, cap):
    n = logit_ref.shape[0]
    lt = logit_ref[...].T[:N_EXPERTS, :]
    ex = jnp.exp(lt - jnp.max(lt, axis=0, keepdims=True))
    aff = ex / jnp.sum(ex, axis=0, keepdims=True)
    bits = pltpu.bitcast(aff, jnp.int32)

    def bit_step(i, thr):
        trial = thr | jnp.left_shift(jnp.int32(1), 30 - i)
        cnt = jnp.sum((bits >= trial).astype(jnp.int32), axis=1, keepdims=True)
        return jnp.where(cnt >= cap, trial, thr)

    thr = lax.fori_loop(0, 31, bit_step, jnp.zeros((N_EXPERTS, 1), jnp.int32))
    gt = bits > thr
    eq = (bits == thr).astype(jnp.int32)
    need = cap - jnp.sum(gt.astype(jnp.int32), axis=1, keepdims=True)
    sel = jnp.where(gt, 1, jnp.where((_lane_prefix_sum(eq) - eq) < need, eq, 0))
    key = jnp.where(sel > 0, _lane_prefix_sum(sel) - sel, -1)

    chunk = key_s.shape[2]
    n_chunks = n // chunk
    st = min(ROUTE_SLOT_TILE, cap)
    count = _lane_prefix_sum(sel)
    for c in range(n_chunks):
        lanes = slice(c * chunk, (c + 1) * chunk)
        key_s[c] = key[:, lanes]
        aff_s[c] = aff[:, lanes]
        cnt_s[c] = count[:, (c + 1) * chunk - LANES:(c + 1) * chunk]
    acc_s[...] = jnp.zeros(acc_s.shape, F32)
    exact = lambda v: v.astype(BF16).astype(F32)
    slot0 = lax.broadcasted_iota(jnp.int32, (st, chunk), 0)
    lane = lax.broadcasted_iota(jnp.int32, (1, chunk), 1)

    def per_expert(e, carry):
        def per_chunk(c, lo):
            hi = cnt_s[c, pl.ds(e, 1), LANES - 1:LANES][0, 0]
            a = aff_s[c, pl.ds(e, 1), :]
            a_hi = exact(a)
            a_mid = exact(a - a_hi)
            a_lo = exact(a - a_hi - a_mid)
            tok = lane + c * chunk
            rows = jnp.concatenate([(tok // LANES).astype(F32), (tok % LANES).astype(F32), a_hi, a_mid, a_lo,
                                    jnp.zeros((11, chunk), F32)], axis=0).astype(BF16)
            k = key_s[c, pl.ds(e, 1), :]

            def per_tile(t, carry2):
                r0 = pl.multiple_of(t * st, st)
                onehot = jnp.where(k == slot0 + r0, 1.0, 0.0).astype(BF16)
                acc_s[e, pl.ds(r0, st), 0:16] += _dot_nt(onehot, rows)
                return carry2

            lax.fori_loop(lo // st, jnp.where(hi > lo, (hi - 1) // st + 1, lo // st), per_tile, 0)
            return hi

        lax.fori_loop(0, n_chunks, per_chunk, jnp.int32(0))
        acc_t = acc_s[e].T
        idx_ref[pl.ds(e, 1), :] = (acc_t[0:1] * LANES + acc_t[1:2]).astype(jnp.int32)
        gate_ref[pl.ds(e, 1), :] = acc_t[2:3] + acc_t[3:4] + acc_t[4:5]
        return carry

    lax.fori_loop(0, N_EXPERTS, per_expert, 0)


ROUTE_SLOT_TILE = 256


def route(logits, b, n, cap, name="moe_route"):
    kern = functools.partial(_route_kernel, cap=cap)
    chunk = min(ROUTE_CHUNK, n)
    assert n % chunk == 0 and cap % min(ROUTE_SLOT_TILE, cap) == 0
    return pl.pallas_call(
        kern, grid=(b,),
        in_specs=[pl.BlockSpec((n, LANES), lambda i: (i, 0))],
        out_specs=[pl.BlockSpec((None, N_EXPERTS, cap), lambda i: (i, 0, 0)),
                   pl.BlockSpec((None, N_EXPERTS, cap), lambda i: (i, 0, 0))],
        out_shape=[jax.ShapeDtypeStruct((b, N_EXPERTS, cap), jnp.int32),
                   jax.ShapeDtypeStruct((b, N_EXPERTS, cap), F32)],
        scratch_shapes=[pltpu.VMEM((n // chunk, N_EXPERTS, chunk), jnp.int32),
                        pltpu.VMEM((n // chunk, N_EXPERTS, chunk), F32),
                        pltpu.VMEM((n // chunk, N_EXPERTS, LANES), jnp.int32),
                        pltpu.VMEM((N_EXPERTS, cap, LANES), F32)],
        name=name, compiler_params=_compiler_params(("parallel",)))(logits)


COMBINE_ROWS = 256
COMBINE_GROUP = 8


def _combine_kernel(idx_ref, ys_ref, o_ref, *, cap):
    bb, e, j = pl.program_id(0), pl.program_id(1), pl.program_id(2)
    rows = ys_ref.shape[0]

    @pl.when((e == 0) & (j == 0))
    def _():
        o_ref[...] = jnp.zeros(o_ref.shape, F32)

    base = (bb * pl.num_programs(1) + e) * cap + j * rows

    def group(g, carry):
        r = g * COMBINE_GROUP
        ids = [idx_ref[base + r + u] for u in range(COMBINE_GROUP)]
        old = [o_ref[ids[u]] for u in range(COMBINE_GROUP)]
        ys = ys_ref[pl.ds(pl.multiple_of(r, COMBINE_GROUP), COMBINE_GROUP), :]
        ys = ys.reshape(COMBINE_GROUP, o_ref.shape[1], LANES)
        for u in range(COMBINE_GROUP):
            o_ref[ids[u]] = old[u] + ys[u]
        return carry

    lax.fori_loop(0, rows // COMBINE_GROUP, group, 0)


def combine(ys, idx, n, name="moe_combine"):
    e, r, d = ys.shape
    b, _, cap = idx.shape
    sub = d // LANES
    rows = min(COMBINE_ROWS, cap)
    assert cap % rows == 0 and rows % COMBINE_GROUP == 0
    tiles = cap // rows
    grid_spec = pltpu.PrefetchScalarGridSpec(
        num_scalar_prefetch=1, grid=(b, e, tiles),
        in_specs=[pl.BlockSpec((None, rows, d), lambda i, k, j, idx_ref: (k, i * tiles + j, 0))],
        out_specs=pl.BlockSpec((None, n, sub, LANES), lambda i, k, j, idx_ref: (i, 0, 0, 0),
                               pipeline_mode=pl.Buffered(1)))
    out = pl.pallas_call(
        functools.partial(_combine_kernel, cap=cap), grid_spec=grid_spec,
        out_shape=jax.ShapeDtypeStruct((b, n, sub, LANES), F32), name=name,
        compiler_params=_compiler_params(("arbitrary", "arbitrary", "arbitrary")))(
            idx.reshape(-1), ys)
    return out.reshape(b, n, d)


def _pad_cols(w, n):
    return jnp.pad(w, ((0, 0), (0, n - w.shape[1])))


def _ssd_project(x2d, g, shift, scale, rows_per_batch, w_main, w_dt, conv_w, conv_b, dt_bias, b, l):
    d_inner = SSM_HEADS * SSM_HEAD_DIM
    main = fused_matmul(x2d, w_main, prologue="norm_mod", g=g, shift=shift, scale=scale,
                        rows_per_batch=rows_per_batch, out_dtype=BF16, tm=1024, name="ssd_in_proj")
    dt_raw = fused_matmul(x2d, w_dt, prologue="norm_mod", g=g, shift=shift, scale=scale,
                          rows_per_batch=rows_per_batch, precise=True, name="ssd_dt_proj")
    xbc = conv_silu(main.reshape(b, l, -1), d_inner, conv_w, conv_b)
    dt = jax.nn.softplus(dt_raw[:, :2 * SSM_HEADS].reshape(b, l, 2, SSM_HEADS) + dt_bias)
    return main, xbc, dt


def _ssd_bidirectional(xbc, dt, a, d_skip, h0_f, h0_b):
    dt_f, dt_b = dt[:, :, 0], dt[:, :, 1]
    y_f, hf = ssd_scan(xbc, dt_f, jnp.swapaxes(dt_f, 1, 2), a[0], h0_f, reverse=False, name="ssd_scan_fwd")
    y, hb = ssd_scan(xbc, dt_b, jnp.swapaxes(dt_b, 1, 2), a[1], h0_b, reverse=True, y_prev=y_f, d_skip=d_skip,
                     name="ssd_scan_bwd")
    return y, hf, hb


def _moe(x2d, g, shift, scale, gate_mod, rows_per_batch, b, n, router_w, w1, w3, w2):
    d = x2d.shape[1]
    cap = EC_CAPACITY_FACTOR * n // N_EXPERTS
    in_pallas = n % ROUTE_CHUNK == 0
    router_w = _pad_cols(router_w, LANES)
    bi = jnp.arange(b)[:, None, None]
    if in_pallas:
        logits = fused_matmul(x2d, router_w, prologue="norm_mod", g=g, shift=shift, scale=scale,
                              rows_per_batch=rows_per_batch, precise=True, name="moe_router")
        idx, gate = route(logits, b, n, cap)
        rows = jnp.swapaxes(idx + bi * n, 0, 1).reshape(-1)
        gate_e = jnp.swapaxes(gate, 0, 1).reshape(N_EXPERTS, b * cap, 1)
        ys = expert_ffn_gather(x2d, rows, g, shift, scale, w1, w3, w2, gate_e, rows_per_batch=cap)
        moe = combine(ys, idx, n)
    else:
        logits, h = fused_matmul(x2d, router_w, prologue="norm_mod", g=g, shift=shift, scale=scale,
                                 rows_per_batch=rows_per_batch, precise=True, emit_h=BF16, name="moe_router_ctx")
        aff = jax.nn.softmax(logits[:, :N_EXPERTS].reshape(b, n, N_EXPERTS), axis=-1)
        gate, idx = lax.top_k(jnp.swapaxes(aff, 1, 2), cap)
        xs = jnp.swapaxes(h.reshape(b, n, d)[bi, idx], 0, 1).reshape(N_EXPERTS, b * cap, d)
        gate_e = jnp.swapaxes(gate, 0, 1).reshape(N_EXPERTS, b * cap, 1)
        ys = expert_ffn(xs, w1, w3, w2, gate_e)
        ys = jnp.swapaxes(ys.reshape(N_EXPERTS, b, cap, d), 0, 1)
        moe = jnp.zeros((b, n, d), F32).at[bi, idx].add(ys)
    x3 = x2d.reshape(b, n, d)
    gm = gate_mod if gate_mod.shape[0] == b else jnp.broadcast_to(gate_mod, (b, 1, d))
    return (x3 + gm * moe).reshape(b * n, d)


def _dft_tables(l, ch):
    def cs(nn):
        idx = jnp.arange(nn)
        ang = ((idx[:, None] * idx[None, :]) % nn).astype(F32) * (2.0 * math.pi / nn)
        s = 1.0 / math.sqrt(nn)
        return jnp.cos(ang) * s, jnp.sin(ang) * s
    cl, sl = cs(l)
    cc, sc = cs(ch)
    return cl, sl, cc, sc


def _fourier(h2d, b, l, out_w, out_b, gate, res, rows_per_batch):
    d = h2d.shape[1]
    ch = d // FNET_GROUPS
    cl, sl, cc, sc = _dft_tables(l, ch)
    eye = jnp.eye(FNET_GROUPS, dtype=F32)
    w_ch = jnp.concatenate([jnp.kron(eye, cc), jnp.kron(eye, sc)], axis=1).astype(BF16)
    pq = fused_matmul(h2d, w_ch, out_dtype=BF16, name="fnet_channel_dft")
    pq = pq.reshape(b, l, 2, d)
    rhs = jnp.transpose(pq, (2, 1, 0, 3)).reshape(2 * l, b * d)
    lhs = jnp.concatenate([cl, -sl], axis=1).astype(BF16)
    mixed = fused_matmul(lhs, rhs, out_dtype=BF16, tk=2048, name="fnet_position_dft")
    mixed = jnp.transpose(mixed.reshape(l, b, d), (1, 0, 2)).reshape(b * l, d)
    return fused_matmul(mixed, out_w.astype(BF16), bias=out_b, gate=gate, res=res,
                        rows_per_batch=rows_per_batch, name="fnet_out_proj")


FFT_N1 = 32
FFT_ROWS = 8
FFT_CH = 128
FFT_UNROLL = 4


def _fnet_kernel(x_ref, cc_ref, sc_ref, kc_ref, ks_ref, mc_ref, ms_ref, o_ref, zr_s, zq_s, ar_s, ai_s):
    n1, nj, rows, ch = zr_s.shape
    n2 = nj * rows
    x = x_ref[...]
    zr_s[...] = _dot(x, cc_ref[...]).reshape(zr_s.shape)
    zq_s[...] = _dot(x, sc_ref[...]).reshape(zq_s.shape)
    kc, ks = kc_ref[...], ks_ref[...]

    def stage1(j, carry):
        p = zr_s[:, j].reshape(n1 * rows, ch).astype(BF16)
        q = zq_s[:, j].reshape(n1 * rows, ch).astype(BF16)
        ar_s[:, j] = (_dot(kc, p) - _dot(ks, q)).reshape(n1, rows, ch)
        ai_s[:, j] = (_dot(kc, q) + _dot(ks, p)).reshape(n1, rows, ch)
        return carry

    lax.fori_loop(0, nj, stage1, 0, unroll=FFT_UNROLL)

    def stage2(k1, carry):
        ar = ar_s[k1].reshape(n2, ch).astype(BF16)
        ai = ai_s[k1].reshape(n2, ch).astype(BF16)
        o_ref[k1] = (_dot(mc_ref[k1], ar) - _dot(ms_ref[k1], ai)).astype(o_ref.dtype)
        return carry

    lax.fori_loop(0, n1, stage2, 0, unroll=FFT_UNROLL)


def _fnet_tables(l, ch):
    n1, n2 = FFT_N1, l // FFT_N1
    def cs(num, den, scale):
        ang = (num % den).astype(F32) * (2.0 * math.pi / den)
        return jnp.cos(ang) * scale, jnp.sin(ang) * scale
    ic = jnp.arange(ch)
    cc, sc = cs(ic[:, None] * ic[None, :], ch, 1.0 / math.sqrt(ch))
    i1 = jnp.arange(n1)
    c1, s1 = cs(i1[:, None] * i1[None, :], n1, 1.0)
    eye = jnp.eye(FFT_ROWS, dtype=F32)
    kc, ks = jnp.kron(c1, eye), jnp.kron(s1, eye)
    k = i1[:, None, None] + n1 * jnp.arange(n2)[None, :, None]
    mc, ms = cs(k * jnp.arange(n2)[None, None, :], l, 1.0 / math.sqrt(l))
    return [t.astype(BF16) for t in (cc, sc, kc, ks, mc, ms)]


def fourier_mix(h, name="fnet_mix"):
    b, l, d = h.shape
    ch = d // FNET_GROUPS
    n1, n2 = FFT_N1, l // FFT_N1
    halves = ch // FFT_CH
    cc, sc, kc, ks, mc, ms = _fnet_tables(l, ch)
    once = pl.Buffered(1)
    const2 = lambda i, g, s: (0, 0)
    return pl.pallas_call(
        _fnet_kernel, grid=(b, FNET_GROUPS, halves),
        in_specs=[pl.BlockSpec((None, l, ch), lambda i, g, s: (i, 0, g)),
                  pl.BlockSpec((ch, FFT_CH), lambda i, g, s: (0, s)),
                  pl.BlockSpec((ch, FFT_CH), lambda i, g, s: (0, s)),
                  pl.BlockSpec((n1 * FFT_ROWS, n1 * FFT_ROWS), const2, pipeline_mode=once),
                  pl.BlockSpec((n1 * FFT_ROWS, n1 * FFT_ROWS), const2, pipeline_mode=once),
                  pl.BlockSpec((n1, n2, n2), lambda i, g, s: (0, 0, 0), pipeline_mode=once),
                  pl.BlockSpec((n1, n2, n2), lambda i, g, s: (0, 0, 0), pipeline_mode=once)],
        out_specs=pl.BlockSpec((None, n1, n2, FFT_CH), lambda i, g, s: (i, 0, 0, g * halves + s)),
        out_shape=jax.ShapeDtypeStruct((b, n1, n2, d), BF16),
        scratch_shapes=[pltpu.VMEM((n1, n2 // FFT_ROWS, FFT_ROWS, FFT_CH), F32) for _ in range(4)],
        name=name,
        compiler_params=_compiler_params(("parallel", "arbitrary", "arbitrary")))(h, cc, sc, kc, ks, mc, ms)


def fnet_out_proj(mixed, w, bias, gate, res, name="fnet_out_proj"):
    b, n1, n2, d = mixed.shape
    n = w.shape[1]
    res3 = res.reshape(b, n2, n1 * n)
    kern = functools.partial(_mm_kernel, prologue=None, precise=False, has_bias=True, has_gate_res=True,
                             emit_h=False, nk=1)
    rows = lambda i, j, k: (i // n1, 0, i % n1)
    out = pl.pallas_call(
        kern, grid=(b * n1, 1, 1),
        in_specs=[pl.BlockSpec((None, None, n2, d), lambda i, j, k: (i // n1, i % n1, 0, 0)),
                  pl.BlockSpec((d, n), lambda i, j, k: (0, 0)),
                  pl.BlockSpec((1, n), lambda i, j, k: (0, 0)),
                  pl.BlockSpec((None, 1, n), lambda i, j, k: (i // n1, 0, 0)),
                  pl.BlockSpec((None, n2, n), rows)],
        out_specs=pl.BlockSpec((None, n2, n), rows),
        out_shape=jax.ShapeDtypeStruct((b, n2, n1 * n), F32), name=name,
        compiler_params=_compiler_params(("parallel", "arbitrary", "arbitrary")))(
            mixed, w, bias.reshape(1, n), gate, res3)
    return out.reshape(b * n2 * n1, n)


CONV_HALO = 16


CONV_BLOCK = 128


def _conv_kernel(prev_ref, x_ref, next_ref, w_ref, b_ref, o_ref, *, taps):
    i = pl.program_id(1)
    last = pl.num_programs(1) - 1
    tl = x_ref.shape[0]
    zero = jnp.zeros(prev_ref.shape, prev_ref.dtype)
    prev = jnp.where(i > 0, prev_ref[...], zero)
    nxt = jnp.where(i < last, next_ref[...], zero)
    ext = jnp.concatenate([prev, x_ref[...], nxt], axis=0)
    span = CONV_BLOCK + 2 * CONV_HALO
    r_out = lax.broadcasted_iota(jnp.int32, (CONV_BLOCK, span), 0)
    r_in = lax.broadcasted_iota(jnp.int32, (CONV_BLOCK, span), 1)
    shifts = [jnp.where(r_in == r_out + (CONV_HALO - taps // 2 + k), 1.0, 0.0).astype(BF16)
              for k in range(taps)]
    for blk in range(tl // CONV_BLOCK):
        src = ext[blk * CONV_BLOCK:blk * CONV_BLOCK + span, :]
        acc = b_ref[...]
        for k in range(taps):
            acc = acc + _dot(shifts[k], src) * w_ref[k:k + 1, :]
        o_ref[blk * CONV_BLOCK:(blk + 1) * CONV_BLOCK, :] = acc * jax.nn.sigmoid(acc)


def conv_silu(main, col0, conv_w, conv_b, *, tl=512, tc=1024, name="ssd_conv"):
    b, l, _ = main.shape
    taps, c = conv_w.shape
    tl = min(tl, l)
    assert l % tl == 0 and c % tc == 0 and col0 % tc == 0 and tl % CONV_BLOCK == 0 and main.dtype == BF16
    cb = col0 // tc
    hb = tl // CONV_HALO
    nh = l // CONV_HALO
    kern = functools.partial(_conv_kernel, taps=taps)
    return pl.pallas_call(
        kern, grid=(b, l // tl, c // tc),
        in_specs=[pl.BlockSpec((None, CONV_HALO, tc), lambda n, i, j: (n, jnp.maximum(i * hb - 1, 0), cb + j)),
                  pl.BlockSpec((None, tl, tc), lambda n, i, j: (n, i, cb + j)),
                  pl.BlockSpec((None, CONV_HALO, tc), lambda n, i, j: (n, jnp.minimum((i + 1) * hb, nh - 1), cb + j)),
                  pl.BlockSpec((taps, tc), lambda n, i, j: (0, j)),
                  pl.BlockSpec((1, tc), lambda n, i, j: (0, j))],
        out_specs=pl.BlockSpec((None, tl, tc), lambda n, i, j: (n, i, j)),
        out_shape=jax.ShapeDtypeStruct((b, l, c), F32), name=name,
        compiler_params=_compiler_params(("parallel", "parallel", "parallel")))(
            main, main, main, conv_w, conv_b.reshape(1, c))


def kernel(x, c, ctx, c_ctx, mod_w, mod_b, norm1_g, norm2_g, ssd_in_w, ssd_conv_w, ssd_conv_b, ssd_dt_bias,
           ssd_a_log, ssd_d, ssd_norm_g, ssd_out_w, fnet_out_w, fnet_out_b, diff_qkv_w, diff_out_w, diff_lambda,
           diff_subln_g, router_w, moe_w1, moe_w3, moe_w2, final_g):
    b, l, d = x.shape
    lc = ctx.shape[1]
    d_inner = SSM_HEADS * SSM_HEAD_DIM

    cond = jnp.concatenate([c, c_ctx[None], jnp.zeros(((8 - (b + 1) % 8) % 8, d), F32)], axis=0)
    cond = jax.nn.silu(cond)
    xl = x.reshape(b * l, d)
    xc = ctx.reshape(b * lc, d)

    for i in range(DEPTH):
        need_ctx = i < DEPTH - 1
        kind, j = i % N_MIXERS, i // N_MIXERS
        mod = fused_matmul(cond, mod_w[i], bias=mod_b[i], tn=1024, name="modulation")
        m_l = [mod[:b, t * d:(t + 1) * d].reshape(b, 1, d) for t in range(6)]
        m_c = [mod[b:b + 1, t * d:(t + 1) * d].reshape(1, 1, d) for t in range(6)]

        if kind == 0:
            w_main = ssd_in_w[j][:, :d_inner + d_inner + 2 * SSM_GROUPS * SSM_STATE].astype(BF16)
            w_dt = _pad_cols(ssd_in_w[j][:, w_main.shape[1]:], LANES)
            a = -jnp.exp(ssd_a_log[j].astype(F32))
            h0 = jnp.zeros((b, SSM_GROUPS, SSM_STATE, GROUP_WIDTH), F32)
            z_c, xbc_c, dt_c = _ssd_project(xc, norm1_g[i], m_c[0], m_c[1], None, w_main, w_dt,
                                            ssd_conv_w[j], ssd_conv_b[j], ssd_dt_bias[j], b, lc)
            y_c, hf, hb = _ssd_bidirectional(xbc_c, dt_c, a, ssd_d[j], h0, h0)
            z_l, xbc_l, dt_l = _ssd_project(xl, norm1_g[i], m_l[0], m_l[1], l, w_main, w_dt,
                                            ssd_conv_w[j], ssd_conv_b[j], ssd_dt_bias[j], b, l)
            y_l, _, _ = _ssd_bidirectional(xbc_l, dt_l, a, ssd_d[j], hf, hb)
            w_out = ssd_out_w[j].astype(BF16)
            xl = fused_matmul(y_l.reshape(b * l, d_inner), w_out, prologue="gated_norm",
                              z=z_l, g=ssd_norm_g[j], gate=m_l[2], res=xl,
                              rows_per_batch=l, name="ssd_out_proj")
            if need_ctx:
                xc = fused_matmul(y_c.reshape(b * lc, d_inner), w_out, prologue="gated_norm",
                                  z=z_c, g=ssd_norm_g[j], gate=m_c[2], res=xc,
                                  name="ssd_out_proj_ctx")
        elif kind == 1:
            h_l = norm_modulate(xl, norm1_g[i], m_l[0], m_l[1], rows_per_batch=l, out_dtype=BF16, name="fnet_norm")
            mixed = fourier_mix(h_l.reshape(b, l, d))
            xl = fnet_out_proj(mixed, fnet_out_w[j].astype(BF16), fnet_out_b[j], m_l[2], xl)
            if need_ctx:
                h_c = norm_modulate(xc, norm1_g[i], m_c[0], m_c[1], out_dtype=BF16, name="fnet_norm_ctx")
                xc = _fourier(h_c, b, lc, fnet_out_w[j], fnet_out_b[j], m_c[2], xc, None)
        else:
            lambda_init = 0.8 - 0.6 * math.exp(-0.3 * i)
            lp = diff_lambda[j].astype(F32)
            lam = jnp.exp(jnp.sum(lp[0] * lp[1])) - jnp.exp(jnp.sum(lp[2] * lp[3])) + lambda_init
            w_qkv = diff_qkv_w[j].astype(BF16)
            w_out = diff_out_w[j].astype(BF16)
            qkv_c = fused_matmul(xc, w_qkv, prologue="norm_mod", g=norm1_g[i], shift=m_c[0], scale=m_c[1],
                                 name="diff_qkv_ctx").reshape(b, lc, 3 * d)
            qkv_l = fused_matmul(xl, w_qkv, prologue="norm_mod", g=norm1_g[i], shift=m_l[0], scale=m_l[1],
                                 rows_per_batch=l, tm=1024, name="diff_qkv").reshape(b, l, 3 * d)
            o_l = diff_attention(qkv_l, qkv_l, qkv_c, lam, diff_subln_g[j], 1.0 - lambda_init, _rope_tables(l),
                                 name="diff_attention", **ATTN_TILES)
            xl = fused_matmul(o_l.reshape(b * l, d), w_out, gate=m_l[2], res=xl, rows_per_batch=l,
                              name="diff_out_proj")
            if need_ctx:
                o_c = diff_attention(qkv_c, None, qkv_c, lam, diff_subln_g[j], 1.0 - lambda_init, None,
                                     name="diff_attention_ctx", **ATTN_TILES_CTX)
                xc = fused_matmul(o_c.reshape(b * lc, d), w_out, gate=m_c[2], res=xc, name="diff_out_proj_ctx")

        w1, w3, w2 = (moe_w1, i), (moe_w3, i), (moe_w2, i)
        xl = _moe(xl, norm2_g[i], m_l[3], m_l[4], m_l[5], l, b, l, router_w[i], w1, w3, w2)
        if need_ctx:
            xc = _moe(xc, norm2_g[i], m_c[3], m_c[4], m_c[5], None, b, lc, router_w[i], w1, w3, w2)

    return norm_modulate(xl, final_g, name="final_norm").reshape(b, l, d)
```

```python
import functools
import math

import jax
import jax.numpy as jnp
from jax import lax
from jax.experimental import pallas as pl
from jax.experimental.pallas import tpu as pltpu

F32 = jnp.float32
BF16 = jnp.bfloat16

EPS = 1e-6
DEPTH = 4
N_MIXERS = 3
GRID_W = 64
ROPE_THETA = 10000.0

SSM_HEAD_DIM = 64
SSM_HEADS = 32
SSM_GROUPS = 8
SSM_STATE = 128
SSM_CHUNK = 128
SSM_CONV = 5
HEADS_PER_GROUP = SSM_HEADS // SSM_GROUPS
GROUP_WIDTH = HEADS_PER_GROUP * SSM_HEAD_DIM

FNET_GROUPS = 4

DIFF_HEADS = 8
DIFF_HEAD_DIM = 64
DIFF_V_DIM = 2 * DIFF_HEAD_DIM
DIFF_SCALE = DIFF_HEAD_DIM ** -0.5
ROPE_PAIRS = DIFF_HEAD_DIM // 4

N_EXPERTS = 16
EC_CAPACITY_FACTOR = 2

LANES = 128
VMEM_LIMIT_BYTES = 56 * 1024 * 1024


def _compiler_params(semantics):
    return pltpu.CompilerParams(dimension_semantics=semantics, vmem_limit_bytes=VMEM_LIMIT_BYTES)


def _split_bf16(x, parts):
    out = []
    r = x
    for _ in range(parts):
        p = r.astype(BF16)
        out.append(p)
        r = r - p.astype(F32)
    return out


def _dot(a, b):
    return jnp.dot(a, b, preferred_element_type=F32)


def _dot_nt(a, b):
    return lax.dot_general(a, b, (((1,), (1,)), ((), ())), preferred_element_type=F32)


def _dot_exact_rhs(x, m_bf16, parts=3, nt=False):
    f = _dot_nt if nt else _dot
    acc = None
    for p in _split_bf16(x, parts):
        t = f(p, m_bf16)
        acc = t if acc is None else acc + t
    return acc


def _dot_exact_lhs(m_bf16, x, parts=3):
    acc = None
    for p in _split_bf16(x, parts):
        t = _dot(m_bf16, p)
        acc = t if acc is None else acc + t
    return acc


def _mm_kernel(*refs, prologue, precise, has_bias, has_gate_res, emit_h, nk):
    it = iter(refs)
    x_ref = next(it)
    z_ref = g_ref = sh_ref = sc_ref = None
    if prologue == "norm_mod":
        g_ref, sh_ref, sc_ref = next(it), next(it), next(it)
    elif prologue == "gated_norm":
        z_ref, g_ref = next(it), next(it)
    w_ref = next(it)
    b_ref = next(it) if has_bias else None
    gate_ref = res_ref = None
    if has_gate_res:
        gate_ref, res_ref = next(it), next(it)
    o_ref = next(it)
    hout_ref = next(it) if emit_h else None
    h_ref = next(it) if prologue else None
    acc_ref = next(it) if nk > 1 else None

    j = pl.program_id(1)
    k = pl.program_id(2)

    if prologue:
        @pl.when(j == 0)
        def _():
            x = x_ref[...].astype(F32)
            if prologue == "gated_norm":
                z = z_ref[...].astype(F32)
                x = x * (z * jax.nn.sigmoid(z))
            ms = jnp.mean(x * x, axis=-1, keepdims=True)
            y = x * lax.rsqrt(ms + EPS) * g_ref[...]
            if prologue == "norm_mod":
                y = y * (1.0 + sc_ref[...]) + sh_ref[...]
            h_ref[...] = y.astype(h_ref.dtype)
            if emit_h:
                hout_ref[...] = y.astype(hout_ref.dtype)
        lhs = h_ref[...]
    else:
        lhs = x_ref[...]

    w = w_ref[...]
    if precise:
        l_hi, l_lo = _split_bf16(lhs.astype(F32), 2)
        w_hi, w_lo = _split_bf16(w.astype(F32), 2)
        part = _dot(l_hi, w_hi) + (_dot(l_lo, w_hi) + _dot(l_hi, w_lo))
    else:
        part = _dot(lhs.astype(BF16), w.astype(BF16))

    def finish(acc):
        if has_bias:
            acc = acc + b_ref[...]
        if has_gate_res:
            acc = res_ref[...].astype(F32) + gate_ref[...] * acc
        o_ref[...] = acc.astype(o_ref.dtype)

    if nk == 1:
        finish(part)
    else:
        @pl.when(k == 0)
        def _():
            acc_ref[...] = part

        @pl.when(k > 0)
        def _():
            acc_ref[...] += part

        @pl.when(k == nk - 1)
        def _():
            finish(acc_ref[...])


def fused_matmul(x, w, *, prologue=None, g=None, shift=None, scale=None, z=None, bias=None,
                 gate=None, res=None, rows_per_batch=None, precise=False, out_dtype=F32,
                 emit_h=None, tm=512, tn=1024, tk=None, name="fused_matmul"):
    m, kdim = x.shape
    n = w.shape[1]
    rows_per_batch = m if rows_per_batch is None else rows_per_batch
    tm = min(tm, m, rows_per_batch)
    tn = min(tn, n)
    tk = kdim if tk is None else min(tk, kdim)
    assert m % tm == 0 and n % tn == 0 and kdim % tk == 0, (x.shape, w.shape, tm, tn, tk)
    nk = kdim // tk
    assert not (prologue and nk > 1)
    assert rows_per_batch % tm == 0
    tiles_per_batch = rows_per_batch // tm

    def bidx(i):
        return i // tiles_per_batch

    args = [x]
    in_specs = [pl.BlockSpec((tm, tk), lambda i, j, k: (i, k))]
    if prologue == "norm_mod":
        args += [g.reshape(1, kdim), shift, scale]
        in_specs += [pl.BlockSpec((1, kdim), lambda i, j, k: (0, 0)),
                     pl.BlockSpec((None, 1, kdim), lambda i, j, k: (bidx(i), 0, 0)),
                     pl.BlockSpec((None, 1, kdim), lambda i, j, k: (bidx(i), 0, 0))]
    elif prologue == "gated_norm":
        args += [z, g.reshape(1, kdim)]
        in_specs += [pl.BlockSpec((tm, kdim), lambda i, j, k: (i, 0)),
                     pl.BlockSpec((1, kdim), lambda i, j, k: (0, 0))]
    args.append(w)
    in_specs.append(pl.BlockSpec((tk, tn), lambda i, j, k: (k, j)))
    if bias is not None:
        args.append(bias.reshape(1, n))
        in_specs.append(pl.BlockSpec((1, tn), lambda i, j, k: (0, j)))
    if gate is not None:
        args += [gate, res]
        in_specs += [pl.BlockSpec((None, 1, tn), lambda i, j, k: (bidx(i), 0, j)),
                     pl.BlockSpec((tm, tn), lambda i, j, k: (i, j))]

    out_shape = [jax.ShapeDtypeStruct((m, n), out_dtype)]
    out_specs = [pl.BlockSpec((tm, tn), lambda i, j, k: (i, j))]
    if emit_h is not None:
        out_shape.append(jax.ShapeDtypeStruct((m, kdim), emit_h))
        out_specs.append(pl.BlockSpec((tm, kdim), lambda i, j, k: (i, 0)))

    scratch = []
    if prologue:
        scratch.append(pltpu.VMEM((tm, kdim), F32 if precise else BF16))
    if nk > 1:
        scratch.append(pltpu.VMEM((tm, tn), F32))

    kern = functools.partial(_mm_kernel, prologue=prologue, precise=precise, has_bias=bias is not None,
                             has_gate_res=gate is not None, emit_h=emit_h is not None, nk=nk)
    outs = pl.pallas_call(
        kern, grid=(m // tm, n // tn, nk), in_specs=in_specs, out_specs=out_specs, out_shape=out_shape,
        scratch_shapes=scratch, name=name,
        compiler_params=_compiler_params(("parallel", "arbitrary", "arbitrary")))(*args)
    return outs if emit_h is not None else outs[0]


def _norm_kernel(x_ref, g_ref, sh_ref, sc_ref, o_ref, *, modulated):
    x = x_ref[...].astype(F32)
    y = x * lax.rsqrt(jnp.mean(x * x, axis=-1, keepdims=True) + EPS) * g_ref[...]
    if modulated:
        y = y * (1.0 + sc_ref[...]) + sh_ref[...]
    o_ref[...] = y.astype(o_ref.dtype)


def norm_modulate(x, g, shift=None, scale=None, *, rows_per_batch=None, out_dtype=F32, tm=512, name="norm"):
    m, d = x.shape
    modulated = shift is not None
    if not modulated:
        shift = scale = jnp.zeros((1, 1, d), F32)
    rows_per_batch = m if rows_per_batch is None else rows_per_batch
    tm = min(tm, m, rows_per_batch)
    assert m % tm == 0 and rows_per_batch % tm == 0
    tiles_per_batch = rows_per_batch // tm
    per_batch = pl.BlockSpec((None, 1, d), lambda i: (i // tiles_per_batch, 0, 0))
    return pl.pallas_call(
        functools.partial(_norm_kernel, modulated=modulated), grid=(m // tm,),
        in_specs=[pl.BlockSpec((tm, d), lambda i: (i, 0)), pl.BlockSpec((1, d), lambda i: (0, 0)),
                  per_batch, per_batch],
        out_specs=pl.BlockSpec((tm, d), lambda i: (i, 0)),
        out_shape=jax.ShapeDtypeStruct((m, d), out_dtype), name=name,
        compiler_params=_compiler_params(("parallel",)))(x, g.reshape(1, d), shift, scale)


def _ssd_kernel(*refs, reverse, combine, q):
    it = iter(refs)
    xs_ref, bm_ref, cm_ref, dt_ref, dtt_ref, a_ref, at_ref, h0_ref = (next(it) for _ in range(8))
    yprev_ref = dskip_ref = None
    if combine:
        yprev_ref, dskip_ref = next(it), next(it)
    y_ref, hfin_ref, state_ref = next(it), next(it), next(it)

    c = pl.program_id(1)

    @pl.when(c == 0)
    def _():
        state_ref[...] = h0_ref[...]

    ii = lax.broadcasted_iota(jnp.int32, (q, q), 0)
    jj = lax.broadcasted_iota(jnp.int32, (q, q), 1)
    mask = (ii <= jj) if reverse else (ii >= jj)
    tri = jnp.where(mask, 1.0, 0.0).astype(BF16)

    dt = dt_ref[...]
    dta = dt * a_ref[...]
    dta_t = dtt_ref[...] * at_ref[...]
    a_cs = _dot_exact_lhs(tri, dta)
    a_cs_t = _dot_exact_rhs(dta_t, tri, nt=True)
    edge = 0 if reverse else q - 1
    tot = a_cs[edge:edge + 1, :]
    to_end = jnp.exp(tot - a_cs)
    into = jnp.exp(a_cs)
    chunk_decay = jnp.exp(tot)

    hh = lax.broadcasted_iota(jnp.int32, (SSM_HEADS, SSM_HEADS * SSM_HEAD_DIM), 0)
    cc = lax.broadcasted_iota(jnp.int32, (SSM_HEADS, SSM_HEADS * SSM_HEAD_DIM), 1)
    expand = jnp.where(cc // SSM_HEAD_DIM == hh, 1.0, 0.0).astype(BF16)
    dt_x = _dot_exact_rhs(dt, expand, parts=1)
    to_end_x = _dot_exact_rhs(to_end, expand, parts=1)
    into_x = _dot_exact_rhs(into, expand, parts=1)
    decay_x = _dot_exact_rhs(jnp.broadcast_to(chunk_decay, (8, SSM_HEADS)), expand, parts=3)[0:1, :]

    for g in range(SSM_GROUPS):
        c0 = g * GROUP_WIDTH
        xs_g = xs_ref[:, c0:c0 + GROUP_WIDTH]
        bm_g = bm_ref[:, g * SSM_STATE:(g + 1) * SSM_STATE]
        cm_g = cm_ref[:, g * SSM_STATE:(g + 1) * SSM_STATE].astype(BF16)
        xdt_g = xs_g * dt_x[:, c0:c0 + GROUP_WIDTH]
        xdt_b = xdt_g.astype(BF16)
        cb = _dot_nt(cm_g, bm_g.astype(BF16))
        st = state_ref[g]
        y_g = _dot(cm_g, st.astype(BF16)) * into_x[:, c0:c0 + GROUP_WIDTH]
        if combine:
            y_g = y_g + yprev_ref[:, c0:c0 + GROUP_WIDTH] + dskip_ref[:, c0:c0 + GROUP_WIDTH] * xs_g
        y_ref[:, c0:c0 + GROUP_WIDTH] = y_g
        for r in range(HEADS_PER_GROUP):
            h = g * HEADS_PER_GROUP + r
            seg = a_cs[:, h:h + 1] - a_cs_t[h:h + 1, :]
            lmat = (jnp.exp(jnp.where(mask, seg, -1e30)) * cb).astype(BF16)
            lo = c0 + r * SSM_HEAD_DIM
            y_ref[:, lo:lo + SSM_HEAD_DIM] += _dot(lmat, xdt_b[:, r * SSM_HEAD_DIM:(r + 1) * SSM_HEAD_DIM])
        xdtw = (xdt_g * to_end_x[:, c0:c0 + GROUP_WIDTH]).astype(BF16)
        new = _dot(bm_g.T.astype(BF16), xdtw)
        state_ref[g] = st * decay_x[:, c0:c0 + GROUP_WIDTH] + new

    @pl.when(c == pl.num_programs(1) - 1)
    def _():
        hfin_ref[...] = state_ref[...]


def ssd_scan(xbc, dt, dt_t, a, h0, *, reverse, y_prev=None, d_skip=None, name="ssd_scan"):
    b, l, _ = xbc.shape
    q = SSM_CHUNK
    nc = l // q
    d_inner = SSM_HEADS * SSM_HEAD_DIM
    nbc = SSM_GROUPS * SSM_STATE
    combine = y_prev is not None

    def cidx(c):
        return (nc - 1 - c) if reverse else c

    xs_blocks = d_inner // d_inner
    in_specs = [
        pl.BlockSpec((None, q, d_inner), lambda i, c: (i, cidx(c), 0)),
        pl.BlockSpec((None, q, nbc), lambda i, c: (i, cidx(c), d_inner // nbc)),
        pl.BlockSpec((None, q, nbc), lambda i, c: (i, cidx(c), d_inner // nbc + 1)),
        pl.BlockSpec((None, q, SSM_HEADS), lambda i, c: (i, cidx(c), 0)),
        pl.BlockSpec((None, SSM_HEADS, q), lambda i, c: (i, 0, cidx(c))),
        pl.BlockSpec((1, SSM_HEADS), lambda i, c: (0, 0)),
        pl.BlockSpec((SSM_HEADS, 1), lambda i, c: (0, 0)),
        pl.BlockSpec((None, SSM_GROUPS, SSM_STATE, GROUP_WIDTH), lambda i, c: (i, 0, 0, 0)),
    ]
    del xs_blocks
    args = [xbc, xbc, xbc, dt, dt_t, a.reshape(1, SSM_HEADS), a.reshape(SSM_HEADS, 1), h0]
    if combine:
        in_specs += [pl.BlockSpec((None, q, d_inner), lambda i, c: (i, cidx(c), 0)),
                     pl.BlockSpec((1, d_inner), lambda i, c: (0, 0))]
        args += [y_prev, jnp.repeat(d_skip, SSM_HEAD_DIM).reshape(1, d_inner)]
    out_shape = [jax.ShapeDtypeStruct((b, l, d_inner), F32),
                 jax.ShapeDtypeStruct((b, SSM_GROUPS, SSM_STATE, GROUP_WIDTH), F32)]
    out_specs = [pl.BlockSpec((None, q, d_inner), lambda i, c: (i, cidx(c), 0)),
                 pl.BlockSpec((None, SSM_GROUPS, SSM_STATE, GROUP_WIDTH), lambda i, c: (i, 0, 0, 0))]
    kern = functools.partial(_ssd_kernel, reverse=reverse, combine=combine, q=q)
    return pl.pallas_call(
        kern, grid=(b, nc), in_specs=in_specs, out_specs=out_specs, out_shape=out_shape,
        scratch_shapes=[pltpu.VMEM((SSM_GROUPS, SSM_STATE, GROUP_WIDTH), F32)], name=name,
        compiler_params=_compiler_params(("parallel", "arbitrary")))(*args)


def _rope(x, cos, sin_up, sin_dn):
    return (x * cos + pltpu.roll(x, ROPE_PAIRS, 1) * sin_up
            + pltpu.roll(x, LANES - ROPE_PAIRS, 1) * sin_dn)


LOG2E = 1.4426950408889634


def _attn_kernel(*refs, n_lat, n_ctx, tk, rope, out_scale):
    it = iter(refs)
    q_ref, qn_ref = next(it), next(it)
    kl_ref = vl_ref = None
    if n_lat:
        kl_ref, vl_ref = next(it), next(it)
    kc_ref, vc_ref = next(it), next(it)
    q_tabs = qn_tabs = None
    if rope:
        q_tabs = [next(it) for _ in range(3)]
        qn_tabs = [next(it) for _ in range(3)]
        ck_ref, suk_ref, sdk_ref = (next(it) for _ in range(3))
    lam_ref, g_ref, o_ref, k_s, vt_s, m_s, l_s, acc_s, s_s, wq_s = (next(it) for _ in range(10))

    qi = pl.program_id(2)
    tq = q_ref.shape[0]
    n_tiles = (n_lat + n_ctx) // tk

    def query_weights(qr, tabs):
        q = qr[...] * (DIFF_SCALE * LOG2E)
        if rope:
            q = _rope(q, tabs[0][...], tabs[1][...], tabs[2][...])
        qt = q.T
        row = lax.broadcasted_iota(jnp.int32, qt.shape, 0)
        return jnp.concatenate([jnp.where(row < DIFF_HEAD_DIM, qt, 0.0),
                                jnp.where(row >= DIFF_HEAD_DIM, qt, 0.0)], axis=1).astype(BF16)

    @pl.when(qi == 0)
    def _():
        if n_lat:
            k = kl_ref[...]
            if rope:
                k = _rope(k, ck_ref[...], suk_ref[...], sdk_ref[...])
            k_s[0:n_lat, :] = k.astype(BF16)
        k_s[n_lat:n_lat + n_ctx, :] = kc_ref[...].astype(BF16)
        for t in range(n_tiles):
            lo, hi = t * tk, (t + 1) * tk
            pieces = []
            if lo < n_lat:
                pieces.append(vl_ref[lo:min(hi, n_lat), :])
            if hi > n_lat:
                pieces.append(vc_ref[max(lo, n_lat) - n_lat:hi - n_lat, :])
            v = pieces[0] if len(pieces) == 1 else jnp.concatenate(pieces, axis=0)
            vt_s[t] = v.T.astype(BF16)
        w_first = query_weights(q_ref, q_tabs)
        wq_s[...] = w_first
        s_s[0] = _dot(k_s[0:tk, :], w_first)

    wq = wq_s[...]

    m_s[...] = jnp.full(m_s.shape, -1e30, F32)
    l_s[...] = jnp.zeros(l_s.shape, F32)
    acc_s[...] = jnp.zeros(acc_s.shape, F32)

    def scores(t):
        off = pl.multiple_of(t * tk, tk)
        return _dot(k_s[pl.ds(off, tk), :], wq)

    def absorb(buf, t):
        s = s_s[buf]
        m_old = m_s[...]
        m_new = jnp.maximum(m_old, jnp.max(s, axis=0, keepdims=True))
        alpha = jnp.exp2(m_old - m_new)
        p = jnp.exp2(s - m_new)
        l_s[...] = alpha * l_s[...] + jnp.sum(p, axis=0, keepdims=True)
        acc_s[...] = alpha * acc_s[...] + _dot(vt_s[t], p.astype(BF16))
        m_s[...] = m_new


    def body(i, carry):
        t0 = 2 * i
        s_s[1] = scores(t0 + 1)
        absorb(0, t0)
        s_s[0] = scores(t0 + 2)
        absorb(1, t0 + 1)
        return carry

    lax.fori_loop(0, n_tiles // 2 - 1, body, 0)
    s_s[1] = scores(n_tiles - 1)
    absorb(0, n_tiles - 2)
    w_next = query_weights(qn_ref, qn_tabs)
    wq_s[...] = w_next
    s_s[0] = _dot(k_s[0:tk, :], w_next)
    absorb(1, n_tiles - 1)

    o_t = acc_s[...] / l_s[...]
    o = (o_t[:, :tq] - lam_ref[0:1, 0:1] * o_t[:, tq:]).T
    ms = jnp.mean(o * o, axis=-1, keepdims=True)
    o_ref[...] = o * lax.rsqrt(ms + EPS) * (g_ref[...] * out_scale)


ATTN_TILES = dict(tq=512, tk=1408)
ATTN_TILES_CTX = dict(tq=256, tk=128)


def diff_attention(qkv_q, qkv_lat, qkv_ctx, lam, subln_g, out_scale, rope_tabs, *, tq, tk, name):
    b, lq, d3 = qkv_q.shape
    d = d3 // 3
    nh = d // DIFF_V_DIM
    n_lat = 0 if qkv_lat is None else qkv_lat.shape[1]
    n_ctx = qkv_ctx.shape[1]
    rope = rope_tabs is not None
    tq = min(tq, lq)
    tk = min(tk, n_lat + n_ctx)
    assert lq % tq == 0 and (n_lat + n_ctx) % (2 * tk) == 0

    nq = lq // tq
    nxt = lambda t: jnp.minimum(t + 1, nq - 1)
    args = [qkv_q, qkv_q]
    in_specs = [pl.BlockSpec((None, tq, DIFF_V_DIM), lambda i, h, t: (i, t, h)),
                pl.BlockSpec((None, tq, DIFF_V_DIM), lambda i, h, t: (i, nxt(t), h))]
    if n_lat:
        args += [qkv_lat, qkv_lat]
        in_specs += [pl.BlockSpec((None, n_lat, DIFF_V_DIM), lambda i, h, t: (i, 0, nh + h)),
                     pl.BlockSpec((None, n_lat, DIFF_V_DIM), lambda i, h, t: (i, 0, 2 * nh + h))]
    args += [qkv_ctx, qkv_ctx]
    in_specs += [pl.BlockSpec((None, n_ctx, DIFF_V_DIM), lambda i, h, t: (i, 0, nh + h)),
                 pl.BlockSpec((None, n_ctx, DIFF_V_DIM), lambda i, h, t: (i, 0, 2 * nh + h))]
    if rope:
        args += list(rope_tabs) * 3
        in_specs += [pl.BlockSpec((tq, LANES), lambda i, h, t: (t, 0))] * 3
        in_specs += [pl.BlockSpec((tq, LANES), lambda i, h, t: (nxt(t), 0))] * 3
        in_specs += [pl.BlockSpec((n_lat, LANES), lambda i, h, t: (0, 0), pipeline_mode=pl.Buffered(1))] * 3
    args += [jnp.broadcast_to(lam.astype(F32).reshape(1, 1), (8, LANES)), subln_g.reshape(1, DIFF_V_DIM)]
    in_specs += [pl.BlockSpec((8, LANES), lambda i, h, t: (0, 0)),
                 pl.BlockSpec((1, DIFF_V_DIM), lambda i, h, t: (0, 0))]

    kern = functools.partial(_attn_kernel, n_lat=n_lat, n_ctx=n_ctx, tk=tk, rope=rope, out_scale=out_scale)
    return pl.pallas_call(
        kern, grid=(b, nh, lq // tq), in_specs=in_specs,
        out_specs=pl.BlockSpec((None, tq, DIFF_V_DIM), lambda i, h, t: (i, t, h)),
        out_shape=jax.ShapeDtypeStruct((b, lq, d), F32),
        scratch_shapes=[pltpu.VMEM((n_lat + n_ctx, DIFF_V_DIM), BF16),
                        pltpu.VMEM(((n_lat + n_ctx) // tk, DIFF_V_DIM, tk), BF16),
                        pltpu.VMEM((1, 2 * tq), F32), pltpu.VMEM((1, 2 * tq), F32),
                        pltpu.VMEM((DIFF_V_DIM, 2 * tq), F32),
                        pltpu.VMEM((2, tk, 2 * tq), F32),
                        pltpu.VMEM((DIFF_V_DIM, 2 * tq), BF16)],
        name=name,
        compiler_params=_compiler_params(("parallel", "parallel", "arbitrary")))(*args)


def _rope_tables(l):
    rows = l // GRID_W
    row = jnp.repeat(jnp.arange(rows), GRID_W).astype(F32)
    col = jnp.tile(jnp.arange(GRID_W), rows).astype(F32)
    inv = ROPE_THETA ** (-jnp.arange(ROPE_PAIRS, dtype=F32) / ROPE_PAIRS)
    lane = jnp.arange(LANES)
    within = lane % DIFF_HEAD_DIM
    axis = within // (2 * ROPE_PAIRS)
    second = (within // ROPE_PAIRS) % 2
    pos = jnp.where(axis[None, :] == 0, row[:, None], col[:, None])
    ang = pos * inv[within % ROPE_PAIRS][None, :]
    cos, sin = jnp.cos(ang), jnp.sin(ang)
    sin_up = jnp.where(second[None, :] == 1, sin, 0.0)
    sin_dn = jnp.where(second[None, :] == 0, -sin, 0.0)
    return cos, sin_up, sin_dn


FFN_F_SPLIT = 1


def _ffn_kernel(x_ref, w1_ref, w3_ref, w2_ref, gate_ref, o_ref, w1_s, w3_s, w2_s):
    @pl.when(pl.program_id(1) == 0)
    def _():
        w1_s[...] = w1_ref[...].astype(BF16)
        w3_s[...] = w3_ref[...].astype(BF16)
        w2_s[...] = w2_ref[...].astype(BF16)

    x = x_ref[...]
    fs = w1_s.shape[1] // FFN_F_SPLIT
    acc = None
    for s in range(FFN_F_SPLIT):
        h1 = _dot(x, w1_s[:, s * fs:(s + 1) * fs])
        h3 = _dot(x, w3_s[:, s * fs:(s + 1) * fs])
        hid = (h1 * jax.nn.sigmoid(h1) * h3).astype(BF16)
        part = _dot(hid, w2_s[s * fs:(s + 1) * fs, :])
        acc = part if acc is None else acc + part
    o_ref[...] = acc * gate_ref[...]


def _ffn_gather_kernel(rows_ref, x_hbm, g_ref, sh_ref, sc_ref, w1_ref, w3_ref, w2_ref, gate_ref, o_ref,
                       xbuf, sem, w1_s, w3_s, w2_s):
    e, t = pl.program_id(0), pl.program_id(1)
    tiles = pl.num_programs(1)
    tm = xbuf.shape[1]
    step = e * tiles + t
    last = pl.num_programs(0) * tiles - 1
    slot = lax.rem(step, 2)

    def gather(step_id, slot_id):
        for i in range(tm):
            row = rows_ref[step_id * tm + i]
            pltpu.make_async_copy(x_hbm.at[pl.ds(row, 1), :], xbuf.at[slot_id, pl.ds(i, 1), :],
                                  sem.at[slot_id]).start()

    @pl.when(step == 0)
    def _():
        gather(0, 0)

    @pl.when(t == 0)
    def _():
        w1_s[...] = w1_ref[...].astype(BF16)
        w3_s[...] = w3_ref[...].astype(BF16)
        w2_s[...] = w2_ref[...].astype(BF16)

    pltpu.make_async_copy(x_hbm.at[pl.ds(0, tm), :], xbuf.at[slot], sem.at[slot]).wait()
    xr = xbuf[slot]
    ms = jnp.mean(xr * xr, axis=-1, keepdims=True)
    x = (xr * lax.rsqrt(ms + EPS) * g_ref[...] * (1.0 + sc_ref[...]) + sh_ref[...]).astype(BF16)

    gather(jnp.where(step < last, step + 1, 0), 1 - slot)

    fs = w1_s.shape[1] // FFN_F_SPLIT
    acc = None
    for s in range(FFN_F_SPLIT):
        h1 = _dot(x, w1_s[:, s * fs:(s + 1) * fs])
        h3 = _dot(x, w3_s[:, s * fs:(s + 1) * fs])
        hid = (h1 * jax.nn.sigmoid(h1) * h3).astype(BF16)
        part = _dot(hid, w2_s[s * fs:(s + 1) * fs, :])
        acc = part if acc is None else acc + part
    o_ref[...] = acc * gate_ref[...]

    @pl.when(step == last)
    def _():
        pltpu.make_async_copy(x_hbm.at[pl.ds(0, tm), :], xbuf.at[1 - slot], sem.at[1 - slot]).wait()


def _with_layer(w):
    return w if isinstance(w, tuple) else (w[None], 0)


def expert_ffn_gather(x2d, rows, g, shift, scale, w1, w3, w2, gate, *, rows_per_batch, tm=512,
                      name="expert_ffn_gather"):
    d = x2d.shape[1]
    (w1, l1), (w3, l3), (w2, l2) = _with_layer(w1), _with_layer(w3), _with_layer(w2)
    _, e, _, f = w1.shape
    r = rows.shape[0] // e
    tm = min(tm, r, rows_per_batch)
    assert r % tm == 0 and rows_per_batch % tm == 0 and f % FFN_F_SPLIT == 0
    tiles = r // tm
    tiles_per_batch = rows_per_batch // tm
    nb = shift.shape[0]
    once = pl.Buffered(1)
    bsel = lambda i, t, rows_ref: ((t // tiles_per_batch) % nb, 0, 0)
    grid_spec = pltpu.PrefetchScalarGridSpec(
        num_scalar_prefetch=1, grid=(e, tiles),
        in_specs=[pl.BlockSpec(memory_space=pl.ANY),
                  pl.BlockSpec((1, d), lambda i, t, rows_ref: (0, 0)),
                  pl.BlockSpec((None, 1, d), bsel),
                  pl.BlockSpec((None, 1, d), bsel),
                  pl.BlockSpec((None, None, d, f), lambda i, t, rows_ref: (l1, i, 0, 0), pipeline_mode=once),
                  pl.BlockSpec((None, None, d, f), lambda i, t, rows_ref: (l3, i, 0, 0), pipeline_mode=once),
                  pl.BlockSpec((None, None, f, d), lambda i, t, rows_ref: (l2, i, 0, 0), pipeline_mode=once),
                  pl.BlockSpec((None, tm, 1), lambda i, t, rows_ref: (i, t, 0))],
        out_specs=pl.BlockSpec((None, tm, d), lambda i, t, rows_ref: (i, t, 0)),
        scratch_shapes=[pltpu.VMEM((2, tm, d), F32), pltpu.SemaphoreType.DMA((2,)),
                        pltpu.VMEM((d, f), BF16), pltpu.VMEM((d, f), BF16), pltpu.VMEM((f, d), BF16)])
    return pl.pallas_call(
        _ffn_gather_kernel, grid_spec=grid_spec,
        out_shape=jax.ShapeDtypeStruct((e, r, d), F32), name=name,
        compiler_params=_compiler_params(("arbitrary", "arbitrary")))(
            rows, x2d, g.reshape(1, d), shift, scale, w1, w3, w2, gate)


def expert_ffn(xs, w1, w3, w2, gate, *, tm=512, name="expert_ffn"):
    e, r, d = xs.shape
    (w1, l1), (w3, l3), (w2, l2) = _with_layer(w1), _with_layer(w3), _with_layer(w2)
    f = w1.shape[3]
    tm = min(tm, r)
    assert r % tm == 0 and f % FFN_F_SPLIT == 0
    once = pl.Buffered(1)
    return pl.pallas_call(
        _ffn_kernel, grid=(e, r // tm),
        in_specs=[pl.BlockSpec((None, tm, d), lambda i, t: (i, t, 0)),
                  pl.BlockSpec((None, None, d, f), lambda i, t: (l1, i, 0, 0), pipeline_mode=once),
                  pl.BlockSpec((None, None, d, f), lambda i, t: (l3, i, 0, 0), pipeline_mode=once),
                  pl.BlockSpec((None, None, f, d), lambda i, t: (l2, i, 0, 0), pipeline_mode=once),
                  pl.BlockSpec((None, tm, 1), lambda i, t: (i, t, 0))],
        out_specs=pl.BlockSpec((None, tm, d), lambda i, t: (i, t, 0)),
        out_shape=jax.ShapeDtypeStruct((e, r, d), F32),
        scratch_shapes=[pltpu.VMEM((d, f), BF16), pltpu.VMEM((d, f), BF16), pltpu.VMEM((f, d), BF16)],
        name=name,
        compiler_params=_compiler_params(("arbitrary", "arbitrary")))(xs, w1, w3, w2, gate)


ROUTE_CHUNK = 1024


def _lane_prefix_sum(x):
    n = x.shape[-1]
    lane = lax.broadcasted_iota(jnp.int32, x.shape, x.ndim - 1)
    shift = 1
    while shift < n:
        x = x + jnp.where(lane >= shift, pltpu.roll(x, shift, x.ndim - 1), 0)
        shift *= 2
    return x


def _route_kernel(logit_ref, idx_ref, gate_ref, key_s, aff_s, cnt_s, acc_s, *, cap):
    n = logit_ref.shape[0]
    lt = logit_ref[...].T[:N_EXPERTS, :]
    ex = jnp.exp(lt - jnp.max(lt, axis=0, keepdims=True))
    aff = ex / jnp.sum(ex, axis=0, keepdims=True)
    bits = pltpu.bitcast(aff, jnp.int32)

    def bit_step(i, thr):
        trial = thr | jnp.left_shift(jnp.int32(1), 30 - i)
        cnt = jnp.sum((bits >= trial).astype(jnp.int32), axis=1, keepdims=True)
        return jnp.where(cnt >= cap, trial, thr)

    thr = lax.fori_loop(0, 31, bit_step, jnp.zeros((N_EXPERTS, 1), jnp.int32))
    gt = bits > thr
    eq = (bits == thr).astype(jnp.int32)
    need = cap - jnp.sum(gt.astype(jnp.int32), axis=1, keepdims=True)
    sel = jnp.where(gt, 1, jnp.where((_lane_prefix_sum(eq) - eq) < need, eq, 0))
    key = jnp.where(sel > 0, _lane_prefix_sum(sel) - sel, -1)

    chunk = key_s.shape[2]
    n_chunks = n // chunk
    st = min(ROUTE_SLOT_TILE, cap)
    count = _lane_prefix_sum(sel)
    for c in range(n_chunks):
        lanes = slice(c * chunk, (c + 1) * chunk)
        key_s[c] = key[:, lanes]
        aff_s[c] = aff[:, lanes]
        cnt_s[c] = count[:, (c + 1) * chunk - LANES:(c + 1) * chunk]
    acc_s[...] = jnp.zeros(acc_s.shape, F32)
    exact = lambda v: v.astype(BF16).astype(F32)
    slot0 = lax.broadcasted_iota(jnp.int32, (st, chunk), 0)
    lane = lax.broadcasted_iota(jnp.int32, (1, chunk), 1)

    def per_expert(e, carry):
        def per_chunk(c, lo):
            hi = cnt_s[c, pl.ds(e, 1), LANES - 1:LANES][0, 0]
            a = aff_s[c, pl.ds(e, 1), :]
            a_hi = exact(a)
            a_mid = exact(a - a_hi)
            a_lo = exact(a - a_hi - a_mid)
            tok = lane + c * chunk
            rows = jnp.concatenate([(tok // LANES).astype(F32), (tok % LANES).astype(F32), a_hi, a_mid, a_lo,
                                    jnp.zeros((11, chunk), F32)], axis=0).astype(BF16)
            k = key_s[c, pl.ds(e, 1), :]

            def per_tile(t, carry2):
                r0 = pl.multiple_of(t * st, st)
                onehot = jnp.where(k == slot0 + r0, 1.0, 0.0).astype(BF16)
                acc_s[e, pl.ds(r0, st), 0:16] += _dot_nt(onehot, rows)
                return carry2

            lax.fori_loop(lo // st, jnp.where(hi > lo, (hi - 1) // st + 1, lo // st), per_tile, 0)
            return hi

        lax.fori_loop(0, n_chunks, per_chunk, jnp.int32(0))
        acc_t = acc_s[e].T
        idx_ref[pl.ds(e, 1), :] = (acc_t[0:1] * LANES + acc_t[1:2]).astype(jnp.int32)
        gate_ref[pl.ds(e, 1), :] = acc_t[2:3] + acc_t[3:4] + acc_t[4:5]
        return carry

    lax.fori_loop(0, N_EXPERTS, per_expert, 0)


ROUTE_SLOT_TILE = 256


def route(logits, b, n, cap, name="moe_route"):
    kern = functools.partial(_route_kernel, cap=cap)
    chunk = min(ROUTE_CHUNK, n)
    assert n % chunk == 0 and cap % min(ROUTE_SLOT_TILE, cap) == 0
    return pl.pallas_call(
        kern, grid=(b,),
        in_specs=[pl.BlockSpec((n, LANES), lambda i: (i, 0))],
        out_specs=[pl.BlockSpec((None, N_EXPERTS, cap), lambda i: (i, 0, 0)),
                   pl.BlockSpec((None, N_EXPERTS, cap), lambda i: (i, 0, 0))],
        out_shape=[jax.ShapeDtypeStruct((b, N_EXPERTS, cap), jnp.int32),
                   jax.ShapeDtypeStruct((b, N_EXPERTS, cap), F32)],
        scratch_shapes=[pltpu.VMEM((n // chunk, N_EXPERTS, chunk), jnp.int32),
                        pltpu.VMEM((n // chunk, N_EXPERTS, chunk), F32),
                        pltpu.VMEM((n // chunk, N_EXPERTS, LANES), jnp.int32),
                        pltpu.VMEM((N_EXPERTS, cap, LANES), F32)],
        name=name, compiler_params=_compiler_params(("parallel",)))(logits)


COMBINE_ROWS = 256
COMBINE_GROUP = 8


def _combine_kernel(idx_ref, ys_ref, o_ref, *, cap):
    bb, e, j = pl.program_id(0), pl.program_id(1), pl.program_id(2)
    rows = ys_ref.shape[0]

    @pl.when((e == 0) & (j == 0))
    def _():
        o_ref[...] = jnp.zeros(o_ref.shape, F32)

    base = (bb * pl.num_programs(1) + e) * cap + j * rows

    def group(g, carry):
        r = g * COMBINE_GROUP
        ids = [idx_ref[base + r + u] for u in range(COMBINE_GROUP)]
        old = [o_ref[ids[u]] for u in range(COMBINE_GROUP)]
        ys = ys_ref[pl.ds(pl.multiple_of(r, COMBINE_GROUP), COMBINE_GROUP), :]
        ys = ys.reshape(COMBINE_GROUP, o_ref.shape[1], LANES)
        for u in range(COMBINE_GROUP):
            o_ref[ids[u]] = old[u] + ys[u]
        return carry

    lax.fori_loop(0, rows // COMBINE_GROUP, group, 0)


def combine(ys, idx, n, name="moe_combine"):
    e, r, d = ys.shape
    b, _, cap = idx.shape
    sub = d // LANES
    rows = min(COMBINE_ROWS, cap)
    assert cap % rows == 0 and rows % COMBINE_GROUP == 0
    tiles = cap // rows
    grid_spec = pltpu.PrefetchScalarGridSpec(
        num_scalar_prefetch=1, grid=(b, e, tiles),
        in_specs=[pl.BlockSpec((None, rows, d), lambda i, k, j, idx_ref: (k, i * tiles + j, 0))],
        out_specs=pl.BlockSpec((None, n, sub, LANES), lambda i, k, j, idx_ref: (i, 0, 0, 0),
                               pipeline_mode=pl.Buffered(1)))
    out = pl.pallas_call(
        functools.partial(_combine_kernel, cap=cap), grid_spec=grid_spec,
        out_shape=jax.ShapeDtypeStruct((b, n, sub, LANES), F32), name=name,
        compiler_params=_compiler_params(("arbitrary", "arbitrary", "arbitrary")))(
            idx.reshape(-1), ys)
    return out.reshape(b, n, d)


def _pad_cols(w, n):
    return jnp.pad(w, ((0, 0), (0, n - w.shape[1])))


def _ssd_project(x2d, g, shift, scale, rows_per_batch, w_main, w_dt, conv_w, conv_b, dt_bias, b, l):
    d_inner = SSM_HEADS * SSM_HEAD_DIM
    main = fused_matmul(x2d, w_main, prologue="norm_mod", g=g, shift=shift, scale=scale,
                        rows_per_batch=rows_per_batch, out_dtype=BF16, tm=1024, name="ssd_in_proj")
    dt_raw = fused_matmul(x2d, w_dt, prologue="norm_mod", g=g, shift=shift, scale=scale,
                          rows_per_batch=rows_per_batch, precise=True, name="ssd_dt_proj")
    xbc = conv_silu(main.reshape(b, l, -1), d_inner, conv_w, conv_b)
    dt = jax.nn.softplus(dt_raw[:, :2 * SSM_HEADS].reshape(b, l, 2, SSM_HEADS) + dt_bias)
    return main, xbc, dt


def _ssd_bidirectional(xbc, dt, a, d_skip, h0_f, h0_b):
    dt_f, dt_b = dt[:, :, 0], dt[:, :, 1]
    y_f, hf = ssd_scan(xbc, dt_f, jnp.swapaxes(dt_f, 1, 2), a[0], h0_f, reverse=False, name="ssd_scan_fwd")
    y, hb = ssd_scan(xbc, dt_b, jnp.swapaxes(dt_b, 1, 2), a[1], h0_b, reverse=True, y_prev=y_f, d_skip=d_skip,
                     name="ssd_scan_bwd")
    return y, hf, hb


def _moe(x2d, g, shift, scale, gate_mod, rows_per_batch, b, n, router_w, w1, w3, w2):
    d = x2d.shape[1]
    cap = EC_CAPACITY_FACTOR * n // N_EXPERTS
    in_pallas = n % ROUTE_CHUNK == 0
    router_w = _pad_cols(router_w, LANES)
    bi = jnp.arange(b)[:, None, None]
    if in_pallas:
        logits = fused_matmul(x2d, router_w, prologue="norm_mod", g=g, shift=shift, scale=scale,
                              rows_per_batch=rows_per_batch, precise=True, name="moe_router")
        idx, gate = route(logits, b, n, cap)
        rows = jnp.swapaxes(idx + bi * n, 0, 1).reshape(-1)
        gate_e = jnp.swapaxes(gate, 0, 1).reshape(N_EXPERTS, b * cap, 1)
        ys = expert_ffn_gather(x2d, rows, g, shift, scale, w1, w3, w2, gate_e, rows_per_batch=cap)
        moe = combine(ys, idx, n)
    else:
        logits, h = fused_matmul(x2d, router_w, prologue="norm_mod", g=g, shift=shift, scale=scale,
                                 rows_per_batch=rows_per_batch, precise=True, emit_h=BF16, name="moe_router_ctx")
        aff = jax.nn.softmax(logits[:, :N_EXPERTS].reshape(b, n, N_EXPERTS), axis=-1)
        gate, idx = lax.top_k(jnp.swapaxes(aff, 1, 2), cap)
        xs = jnp.swapaxes(h.reshape(b, n, d)[bi, idx], 0, 1).reshape(N_EXPERTS, b * cap, d)
        gate_e = jnp.swapaxes(gate, 0, 1).reshape(N_EXPERTS, b * cap, 1)
        ys = expert_ffn(xs, w1, w3, w2, gate_e)
        ys = jnp.swapaxes(ys.reshape(N_EXPERTS, b, cap, d), 0, 1)
        moe = jnp.zeros((b, n, d), F32).at[bi, idx].add(ys)
    x3 = x2d.reshape(b, n, d)
    gm = gate_mod if gate_mod.shape[0] == b else jnp.broadcast_to(gate_mod, (b, 1, d))
    return (x3 + gm * moe).reshape(b * n, d)


def _dft_tables(l, ch):
    def cs(nn):
        idx = jnp.arange(nn)
        ang = ((idx[:, None] * idx[None, :]) % nn).astype(F32) * (2.0 * math.pi / nn)
        s = 1.0 / math.sqrt(nn)
        return jnp.cos(ang) * s, jnp.sin(ang) * s
    cl, sl = cs(l)
    cc, sc = cs(ch)
    return cl, sl, cc, sc


def _fourier(h2d, b, l, out_w, out_b, gate, res, rows_per_batch):
    d = h2d.shape[1]
    ch = d // FNET_GROUPS
    cl, sl, cc, sc = _dft_tables(l, ch)
    eye = jnp.eye(FNET_GROUPS, dtype=F32)
    w_ch = jnp.concatenate([jnp.kron(eye, cc), jnp.kron(eye, sc)], axis=1).astype(BF16)
    pq = fused_matmul(h2d, w_ch, out_dtype=BF16, name="fnet_channel_dft")
    pq = pq.reshape(b, l, 2, d)
    rhs = jnp.transpose(pq, (2, 1, 0, 3)).reshape(2 * l, b * d)
    lhs = jnp.concatenate([cl, -sl], axis=1).astype(BF16)
    mixed = fused_matmul(lhs, rhs, out_dtype=BF16, tk=2048, name="fnet_position_dft")
    mixed = jnp.transpose(mixed.reshape(l, b, d), (1, 0, 2)).reshape(b * l, d)
    return fused_matmul(mixed, out_w.astype(BF16), bias=out_b, gate=gate, res=res,
                        rows_per_batch=rows_per_batch, name="fnet_out_proj")


FFT_N1 = 32
FFT_ROWS = 8
FFT_CH = 128
FFT_UNROLL = 4


def _fnet_kernel(x_ref, cc_ref, sc_ref, kc_ref, ks_ref, mc_ref, ms_ref, o_ref, zr_s, zq_s, ar_s, ai_s):
    n1, nj, rows, ch = zr_s.shape
    n2 = nj * rows
    x = x_ref[...]
    zr_s[...] = _dot(x, cc_ref[...]).reshape(zr_s.shape)
    zq_s[...] = _dot(x, sc_ref[...]).reshape(zq_s.shape)
    kc, ks = kc_ref[...], ks_ref[...]

    def stage1(j, carry):
        p = zr_s[:, j].reshape(n1 * rows, ch).astype(BF16)
        q = zq_s[:, j].reshape(n1 * rows, ch).astype(BF16)
        ar_s[:, j] = (_dot(kc, p) - _dot(ks, q)).reshape(n1, rows, ch)
        ai_s[:, j] = (_dot(kc, q) + _dot(ks, p)).reshape(n1, rows, ch)
        return carry

    lax.fori_loop(0, nj, stage1, 0, unroll=FFT_UNROLL)

    def stage2(k1, carry):
        ar = ar_s[k1].reshape(n2, ch).astype(BF16)
        ai = ai_s[k1].reshape(n2, ch).astype(BF16)
        o_ref[k1] = (_dot(mc_ref[k1], ar) - _dot(ms_ref[k1], ai)).astype(o_ref.dtype)
        return carry

    lax.fori_loop(0, n1, stage2, 0, unroll=FFT_UNROLL)


def _fnet_tables(l, ch):
    n1, n2 = FFT_N1, l // FFT_N1
    def cs(num, den, scale):
        ang = (num % den).astype(F32) * (2.0 * math.pi / den)
        return jnp.cos(ang) * scale, jnp.sin(ang) * scale
    ic = jnp.arange(ch)
    cc, sc = cs(ic[:, None] * ic[None, :], ch, 1.0 / math.sqrt(ch))
    i1 = jnp.arange(n1)
    c1, s1 = cs(i1[:, None] * i1[None, :], n1, 1.0)
    eye = jnp.eye(FFT_ROWS, dtype=F32)
    kc, ks = jnp.kron(c1, eye), jnp.kron(s1, eye)
    k = i1[:, None, None] + n1 * jnp.arange(n2)[None, :, None]
    mc, ms = cs(k * jnp.arange(n2)[None, None, :], l, 1.0 / math.sqrt(l))
    return [t.astype(BF16) for t in (cc, sc, kc, ks, mc, ms)]


def fourier_mix(h, name="fnet_mix"):
    b, l, d = h.shape
    ch = d // FNET_GROUPS
    n1, n2 = FFT_N1, l // FFT_N1
    halves = ch // FFT_CH
    cc, sc, kc, ks, mc, ms = _fnet_tables(l, ch)
    once = pl.Buffered(1)
    const2 = lambda i, g, s: (0, 0)
    return pl.pallas_call(
        _fnet_kernel, grid=(b, FNET_GROUPS, halves),
        in_specs=[pl.BlockSpec((None, l, ch), lambda i, g, s: (i, 0, g)),
                  pl.BlockSpec((ch, FFT_CH), lambda i, g, s: (0, s)),
                  pl.BlockSpec((ch, FFT_CH), lambda i, g, s: (0, s)),
                  pl.BlockSpec((n1 * FFT_ROWS, n1 * FFT_ROWS), const2, pipeline_mode=once),
                  pl.BlockSpec((n1 * FFT_ROWS, n1 * FFT_ROWS), const2, pipeline_mode=once),
                  pl.BlockSpec((n1, n2, n2), lambda i, g, s: (0, 0, 0), pipeline_mode=once),
                  pl.BlockSpec((n1, n2, n2), lambda i, g, s: (0, 0, 0), pipeline_mode=once)],
        out_specs=pl.BlockSpec((None, n1, n2, FFT_CH), lambda i, g, s: (i, 0, 0, g * halves + s)),
        out_shape=jax.ShapeDtypeStruct((b, n1, n2, d), BF16),
        scratch_shapes=[pltpu.VMEM((n1, n2 // FFT_ROWS, FFT_ROWS, FFT_CH), F32) for _ in range(4)],
        name=name,
        compiler_params=_compiler_params(("parallel", "arbitrary", "arbitrary")))(h, cc, sc, kc, ks, mc, ms)


def fnet_out_proj(mixed, w, bias, gate, res, name="fnet_out_proj"):
    b, n1, n2, d = mixed.shape
    n = w.shape[1]
    res3 = res.reshape(b, n2, n1 * n)
    kern = functools.partial(_mm_kernel, prologue=None, precise=False, has_bias=True, has_gate_res=True,
                             emit_h=False, nk=1)
    rows = lambda i, j, k: (i // n1, 0, i % n1)
    out = pl.pallas_call(
        kern, grid=(b * n1, 1, 1),
        in_specs=[pl.BlockSpec((None, None, n2, d), lambda i, j, k: (i // n1, i % n1, 0, 0)),
                  pl.BlockSpec((d, n), lambda i, j, k: (0, 0)),
                  pl.BlockSpec((1, n), lambda i, j, k: (0, 0)),
                  pl.BlockSpec((None, 1, n), lambda i, j, k: (i // n1, 0, 0)),
                  pl.BlockSpec((None, n2, n), rows)],
        out_specs=pl.BlockSpec((None, n2, n), rows),
        out_shape=jax.ShapeDtypeStruct((b, n2, n1 * n), F32), name=name,
        compiler_params=_compiler_params(("parallel", "arbitrary", "arbitrary")))(
            mixed, w, bias.reshape(1, n), gate, res3)
    return out.reshape(b * n2 * n1, n)


CONV_HALO = 16


CONV_BLOCK = 128


def _conv_kernel(prev_ref, x_ref, next_ref, w_ref, b_ref, o_ref, *, taps):
    i = pl.program_id(1)
    last = pl.num_programs(1) - 1
    tl = x_ref.shape[0]
    zero = jnp.zeros(prev_ref.shape, prev_ref.dtype)
    prev = jnp.where(i > 0, prev_ref[...], zero)
    nxt = jnp.where(i < last, next_ref[...], zero)
    ext = jnp.concatenate([prev, x_ref[...], nxt], axis=0)
    span = CONV_BLOCK + 2 * CONV_HALO
    r_out = lax.broadcasted_iota(jnp.int32, (CONV_BLOCK, span), 0)
    r_in = lax.broadcasted_iota(jnp.int32, (CONV_BLOCK, span), 1)
    shifts = [jnp.where(r_in == r_out + (CONV_HALO - taps // 2 + k), 1.0, 0.0).astype(BF16)
              for k in range(taps)]
    for blk in range(tl // CONV_BLOCK):
        src = ext[blk * CONV_BLOCK:blk * CONV_BLOCK + span, :]
        acc = b_ref[...]
        for k in range(taps):
            acc = acc + _dot(shifts[k], src) * w_ref[k:k + 1, :]
        o_ref[blk * CONV_BLOCK:(blk + 1) * CONV_BLOCK, :] = acc * jax.nn.sigmoid(acc)


def conv_silu(main, col0, conv_w, conv_b, *, tl=512, tc=1024, name="ssd_conv"):
    b, l, _ = main.shape
    taps, c = conv_w.shape
    tl = min(tl, l)
    assert l % tl == 0 and c % tc == 0 and col0 % tc == 0 and tl % CONV_BLOCK == 0 and main.dtype == BF16
    cb = col0 // tc
    hb = tl // CONV_HALO
    nh = l // CONV_HALO
    kern = functools.partial(_conv_kernel, taps=taps)
    return pl.pallas_call(
        kern, grid=(b, l // tl, c // tc),
        in_specs=[pl.BlockSpec((None, CONV_HALO, tc), lambda n, i, j: (n, jnp.maximum(i * hb - 1, 0), cb + j)),
                  pl.BlockSpec((None, tl, tc), lambda n, i, j: (n, i, cb + j)),
                  pl.BlockSpec((None, CONV_HALO, tc), lambda n, i, j: (n, jnp.minimum((i + 1) * hb, nh - 1), cb + j)),
                  pl.BlockSpec((taps, tc), lambda n, i, j: (0, j)),
                  pl.BlockSpec((1, tc), lambda n, i, j: (0, j))],
        out_specs=pl.BlockSpec((None, tl, tc), lambda n, i, j: (n, i, j)),
        out_shape=jax.ShapeDtypeStruct((b, l, c), F32), name=name,
        compiler_params=_compiler_params(("parallel", "parallel", "parallel")))(
            main, main, main, conv_w, conv_b.reshape(1, c))


def kernel(x, c, ctx, c_ctx, mod_w, mod_b, norm1_g, norm2_g, ssd_in_w, ssd_conv_w, ssd_conv_b, ssd_dt_bias,
           ssd_a_log, ssd_d, ssd_norm_g, ssd_out_w, fnet_out_w, fnet_out_b, diff_qkv_w, diff_out_w, diff_lambda,
           diff_subln_g, router_w, moe_w1, moe_w3, moe_w2, final_g):
    b, l, d = x.shape
    lc = ctx.shape[1]
    d_inner = SSM_HEADS * SSM_HEAD_DIM

    cond = jnp.concatenate([c, c_ctx[None], jnp.zeros(((8 - (b + 1) % 8) % 8, d), F32)], axis=0)
    cond = jax.nn.silu(cond)
    xl = x.reshape(b * l, d)
    xc = ctx.reshape(b * lc, d)

    for i in range(DEPTH):
        need_ctx = i < DEPTH - 1
        kind, j = i % N_MIXERS, i // N_MIXERS
        mod = fused_matmul(cond, mod_w[i], bias=mod_b[i], tn=1024, name="modulation")
        m_l = [mod[:b, t * d:(t + 1) * d].reshape(b, 1, d) for t in range(6)]
        m_c = [mod[b:b + 1, t * d:(t + 1) * d].reshape(1, 1, d) for t in range(6)]

        if kind == 0:
            w_main = ssd_in_w[j][:, :d_inner + d_inner + 2 * SSM_GROUPS * SSM_STATE].astype(BF16)
            w_dt = _pad_cols(ssd_in_w[j][:, w_main.shape[1]:], LANES)
            a = -jnp.exp(ssd_a_log[j].astype(F32))
            h0 = jnp.zeros((b, SSM_GROUPS, SSM_STATE, GROUP_WIDTH), F32)
            z_c, xbc_c, dt_c = _ssd_project(xc, norm1_g[i], m_c[0], m_c[1], None, w_main, w_dt,
                                            ssd_conv_w[j], ssd_conv_b[j], ssd_dt_bias[j], b, lc)
            y_c, hf, hb = _ssd_bidirectional(xbc_c, dt_c, a, ssd_d[j], h0, h0)
            z_l, xbc_l, dt_l = _ssd_project(xl, norm1_g[i], m_l[0], m_l[1], l, w_main, w_dt,
                                            ssd_conv_w[j], ssd_conv_b[j], ssd_dt_bias[j], b, l)
            y_l, _, _ = _ssd_bidirectional(xbc_l, dt_l, a, ssd_d[j], hf, hb)
            w_out = ssd_out_w[j].astype(BF16)
            xl = fused_matmul(y_l.reshape(b * l, d_inner), w_out, prologue="gated_norm",
                              z=z_l, g=ssd_norm_g[j], gate=m_l[2], res=xl,
                              rows_per_batch=l, name="ssd_out_proj")
            if need_ctx:
                xc = fused_matmul(y_c.reshape(b * lc, d_inner), w_out, prologue="gated_norm",
                                  z=z_c, g=ssd_norm_g[j], gate=m_c[2], res=xc,
                                  name="ssd_out_proj_ctx")
        elif kind == 1:
            h_l = norm_modulate(xl, norm1_g[i], m_l[0], m_l[1], rows_per_batch=l, out_dtype=BF16, name="fnet_norm")
            mixed = fourier_mix(h_l.reshape(b, l, d))
            xl = fnet_out_proj(mixed, fnet_out_w[j].astype(BF16), fnet_out_b[j], m_l[2], xl)
            if need_ctx:
                h_c = norm_modulate(xc, norm1_g[i], m_c[0], m_c[1], out_dtype=BF16, name="fnet_norm_ctx")
                xc = _fourier(h_c, b, lc, fnet_out_w[j], fnet_out_b[j], m_c[2], xc, None)
        else:
            lambda_init = 0.8 - 0.6 * math.exp(-0.3 * i)
            lp = diff_lambda[j].astype(F32)
            lam = jnp.exp(jnp.sum(lp[0] * lp[1])) - jnp.exp(jnp.sum(lp[2] * lp[3])) + lambda_init
            w_qkv = diff_qkv_w[j].astype(BF16)
            w_out = diff_out_w[j].astype(BF16)
            qkv_c = fused_matmul(xc, w_qkv, prologue="norm_mod", g=norm1_g[i], shift=m_c[0], scale=m_c[1],
                                 name="diff_qkv_ctx").reshape(b, lc, 3 * d)
            qkv_l = fused_matmul(xl, w_qkv, prologue="norm_mod", g=norm1_g[i], shift=m_l[0], scale=m_l[1],
                                 rows_per_batch=l, tm=1024, name="diff_qkv").reshape(b, l, 3 * d)
            o_l = diff_attention(qkv_l, qkv_l, qkv_c, lam, diff_subln_g[j], 1.0 - lambda_init, _rope_tables(l),
                                 name="diff_attention", **ATTN_TILES)
            xl = fused_matmul(o_l.reshape(b * l, d), w_out, gate=m_l[2], res=xl, rows_per_batch=l,
                              name="diff_out_proj")
            if need_ctx:
                o_c = diff_attention(qkv_c, None, qkv_c, lam, diff_subln_g[j], 1.0 - lambda_init, None,
                                     name="diff_attention_ctx", **ATTN_TILES_CTX)
                xc = fused_matmul(o_c.reshape(b * lc, d), w_out, gate=m_c[2], res=xc, name="diff_out_proj_ctx")

        w1, w3, w2 = (moe_w1, i), (moe_w3, i), (moe_w2, i)
        xl = _moe(xl, norm2_g[i], m_l[3], m_l[4], m_l[5], l, b, l, router_w[i], w1, w3, w2)
        if need_ctx:
            xc = _moe(xc, norm2_g[i], m_c[3], m_c[4], m_c[5], None, b, lc, router_w[i], w1, w3, w2)

    return norm_modulate(xl, final_g, name="final_norm").reshape(b, l, d)
```

```python
import functools
import math

import jax
import jax.numpy as jnp
from jax import lax
from jax.experimental import pallas as pl
from jax.experimental.pallas import tpu as pltpu

F32 = jnp.float32
BF16 = jnp.bfloat16

EPS = 1e-6
DEPTH = 4
N_MIXERS = 3
GRID_W = 64
ROPE_THETA = 10000.0

SSM_HEAD_DIM = 64
SSM_HEADS = 32
SSM_GROUPS = 8
SSM_STATE = 128
SSM_CHUNK = 128
SSM_CONV = 5
HEADS_PER_GROUP = SSM_HEADS // SSM_GROUPS
GROUP_WIDTH = HEADS_PER_GROUP * SSM_HEAD_DIM

FNET_GROUPS = 4

DIFF_HEADS = 8
DIFF_HEAD_DIM = 64
DIFF_V_DIM = 2 * DIFF_HEAD_DIM
DIFF_SCALE = DIFF_HEAD_DIM ** -0.5
ROPE_PAIRS = DIFF_HEAD_DIM // 4

N_EXPERTS = 16
EC_CAPACITY_FACTOR = 2

LANES = 128
VMEM_LIMIT_BYTES = 56 * 1024 * 1024


def _compiler_params(semantics):
    return pltpu.CompilerParams(dimension_semantics=semantics, vmem_limit_bytes=VMEM_LIMIT_BYTES)


def _split_bf16(x, parts):
    out = []
    r = x
    for _ in range(parts):
        p = r.astype(BF16)
        out.append(p)
        r = r - p.astype(F32)
    return out


def _dot(a, b):
    return jnp.dot(a, b, preferred_element_type=F32)


def _dot_nt(a, b):
    return lax.dot_general(a, b, (((1,), (1,)), ((), ())), preferred_element_type=F32)


def _dot_exact_rhs(x, m_bf16, parts=3, nt=False):
    f = _dot_nt if nt else _dot
    acc = None
    for p in _split_bf16(x, parts):
        t = f(p, m_bf16)
        acc = t if acc is None else acc + t
    return acc


def _dot_exact_lhs(m_bf16, x, parts=3):
    acc = None
    for p in _split_bf16(x, parts):
        t = _dot(m_bf16, p)
        acc = t if acc is None else acc + t
    return acc


def _mm_kernel(*refs, prologue, precise, has_bias, has_gate_res, emit_h, nk):
    it = iter(refs)
    x_ref = next(it)
    z_ref = g_ref = sh_ref = sc_ref = None
    if prologue == "norm_mod":
        g_ref, sh_ref, sc_ref = next(it), next(it), next(it)
    elif prologue == "gated_norm":
        z_ref, g_ref = next(it), next(it)
    w_ref = next(it)
    b_ref = next(it) if has_bias else None
    gate_ref = res_ref = None
    if has_gate_res:
        gate_ref, res_ref = next(it), next(it)
    o_ref = next(it)
    hout_ref = next(it) if emit_h else None
    h_ref = next(it) if prologue else None
    acc_ref = next(it) if nk > 1 else None

    j = pl.program_id(1)
    k = pl.program_id(2)

    if prologue:
        @pl.when(j == 0)
        def _():
            x = x_ref[...].astype(F32)
            if prologue == "gated_norm":
                z = z_ref[...].astype(F32)
                x = x * (z * jax.nn.sigmoid(z))
            ms = jnp.mean(x * x, axis=-1, keepdims=True)
            y = x * lax.rsqrt(ms + EPS) * g_ref[...]
            if prologue == "norm_mod":
                y = y * (1.0 + sc_ref[...]) + sh_ref[...]
            h_ref[...] = y.astype(h_ref.dtype)
            if emit_h:
                hout_ref[...] = y.astype(hout_ref.dtype)
        lhs = h_ref[...]
    else:
        lhs = x_ref[...]

    w = w_ref[...]
    if precise:
        l_hi, l_lo = _split_bf16(lhs.astype(F32), 2)
        w_hi, w_lo = _split_bf16(w.astype(F32), 2)
        part = _dot(l_hi, w_hi) + (_dot(l_lo, w_hi) + _dot(l_hi, w_lo))
    else:
        part = _dot(lhs.astype(BF16), w.astype(BF16))

    def finish(acc):
        if has_bias:
            acc = acc + b_ref[...]
        if has_gate_res:
            acc = res_ref[...].astype(F32) + gate_ref[...] * acc
        o_ref[...] = acc.astype(o_ref.dtype)

    if nk == 1:
        finish(part)
    else:
        @pl.when(k == 0)
        def _():
            acc_ref[...] = part

        @pl.when(k > 0)
        def _():
            acc_ref[...] += part

        @pl.when(k == nk - 1)
        def _():
            finish(acc_ref[...])


def fused_matmul(x, w, *, prologue=None, g=None, shift=None, scale=None, z=None, bias=None,
                 gate=None, res=None, rows_per_batch=None, precise=False, out_dtype=F32,
                 emit_h=None, tm=512, tn=1024, tk=None, name="fused_matmul"):
    m, kdim = x.shape
    n = w.shape[1]
    rows_per_batch = m if rows_per_batch is None else rows_per_batch
    tm = min(tm, m, rows_per_batch)
    tn = min(tn, n)
    tk = kdim if tk is None else min(tk, kdim)
    assert m % tm == 0 and n % tn == 0 and kdim % tk == 0, (x.shape, w.shape, tm, tn, tk)
    nk = kdim // tk
    assert not (prologue and nk > 1)
    assert rows_per_batch % tm == 0
    tiles_per_batch = rows_per_batch // tm

    def bidx(i):
        return i // tiles_per_batch

    args = [x]
    in_specs = [pl.BlockSpec((tm, tk), lambda i, j, k: (i, k))]
    if prologue == "norm_mod":
        args += [g.reshape(1, kdim), shift, scale]
        in_specs += [pl.BlockSpec((1, kdim), lambda i, j, k: (0, 0)),
                     pl.BlockSpec((None, 1, kdim), lambda i, j, k: (bidx(i), 0, 0)),
                     pl.BlockSpec((None, 1, kdim), lambda i, j, k: (bidx(i), 0, 0))]
    elif prologue == "gated_norm":
        args += [z, g.reshape(1, kdim)]
        in_specs += [pl.BlockSpec((tm, kdim), lambda i, j, k: (i, 0)),
                     pl.BlockSpec((1, kdim), lambda i, j, k: (0, 0))]
    args.append(w)
    in_specs.append(pl.BlockSpec((tk, tn), lambda i, j, k: (k, j)))
    if bias is not None:
        args.append(bias.reshape(1, n))
        in_specs.append(pl.BlockSpec((1, tn), lambda i, j, k: (0, j)))
    if gate is not None:
        args += [gate, res]
        in_specs += [pl.BlockSpec((None, 1, tn), lambda i, j, k: (bidx(i), 0, j)),
                     pl.BlockSpec((tm, tn), lambda i, j, k: (i, j))]

    out_shape = [jax.ShapeDtypeStruct((m, n), out_dtype)]
    out_specs = [pl.BlockSpec((tm, tn), lambda i, j, k: (i, j))]
    if emit_h is not None:
        out_shape.append(jax.ShapeDtypeStruct((m, kdim), emit_h))
        out_specs.append(pl.BlockSpec((tm, kdim), lambda i, j, k: (i, 0)))

    scratch = []
    if prologue:
        scratch.append(pltpu.VMEM((tm, kdim), F32 if precise else BF16))
    if nk > 1:
        scratch.append(pltpu.VMEM((tm, tn), F32))

    kern = functools.partial(_mm_kernel, prologue=prologue, precise=precise, has_bias=bias is not None,
                             has_gate_res=gate is not None, emit_h=emit_h is not None, nk=nk)
    outs = pl.pallas_call(
        kern, grid=(m // tm, n // tn, nk), in_specs=in_specs, out_specs=out_specs, out_shape=out_shape,
        scratch_shapes=scratch, name=name,
        compiler_params=_compiler_params(("parallel", "arbitrary", "arbitrary")))(*args)
    return outs if emit_h is not None else outs[0]


def _norm_kernel(x_ref, g_ref, sh_ref, sc_ref, o_ref, *, modulated):
    x = x_ref[...].astype(F32)
    y = x * lax.rsqrt(jnp.mean(x * x, axis=-1, keepdims=True) + EPS) * g_ref[...]
    if modulated:
        y = y * (1.0 + sc_ref[...]) + sh_ref[...]
    o_ref[...] = y.astype(o_ref.dtype)


def norm_modulate(x, g, shift=None, scale=None, *, rows_per_batch=None, out_dtype=F32, tm=512, name="norm"):
    m, d = x.shape
    modulated = shift is not None
    if not modulated:
        shift = scale = jnp.zeros((1, 1, d), F32)
    rows_per_batch = m if rows_per_batch is None else rows_per_batch
    tm = min(tm, m, rows_per_batch)
    assert m % tm == 0 and rows_per_batch % tm == 0
    tiles_per_batch = rows_per_batch // tm
    per_batch = pl.BlockSpec((None, 1, d), lambda i: (i // tiles_per_batch, 0, 0))
    return pl.pallas_call(
        functools.partial(_norm_kernel, modulated=modulated), grid=(m // tm,),
        in_specs=[pl.BlockSpec((tm, d), lambda i: (i, 0)), pl.BlockSpec((1, d), lambda i: (0, 0)),
                  per_batch, per_batch],
        out_specs=pl.BlockSpec((tm, d), lambda i: (i, 0)),
        out_shape=jax.ShapeDtypeStruct((m, d), out_dtype), name=name,
        compiler_params=_compiler_params(("parallel",)))(x, g.reshape(1, d), shift, scale)


def _ssd_kernel(*refs, reverse, combine, q):
    it = iter(refs)
    xs_ref, bm_ref, cm_ref, dt_ref, dtt_ref, a_ref, at_ref, h0_ref = (next(it) for _ in range(8))
    yprev_ref = dskip_ref = None
    if combine:
        yprev_ref, dskip_ref = next(it), next(it)
    y_ref, hfin_ref, state_ref = next(it), next(it), next(it)

    c = pl.program_id(1)

    @pl.when(c == 0)
    def _():
        state_ref[...] = h0_ref[...]

    ii = lax.broadcasted_iota(jnp.int32, (q, q), 0)
    jj = lax.broadcasted_iota(jnp.int32, (q, q), 1)
    mask = (ii <= jj) if reverse else (ii >= jj)
    tri = jnp.where(mask, 1.0, 0.0).astype(BF16)

    dt = dt_ref[...]
    dta = dt * a_ref[...]
    dta_t = dtt_ref[...] * at_ref[...]
    a_cs = _dot_exact_lhs(tri, dta)
    a_cs_t = _dot_exact_rhs(dta_t, tri, nt=True)
    edge = 0 if reverse else q - 1
    tot = a_cs[edge:edge + 1, :]
    to_end = jnp.exp(tot - a_cs)
    into = jnp.exp(a_cs)
    chunk_decay = jnp.exp(tot)

    hh = lax.broadcasted_iota(jnp.int32, (SSM_HEADS, SSM_HEADS * SSM_HEAD_DIM), 0)
    cc = lax.broadcasted_iota(jnp.int32, (SSM_HEADS, SSM_HEADS * SSM_HEAD_DIM), 1)
    expand = jnp.where(cc // SSM_HEAD_DIM == hh, 1.0, 0.0).astype(BF16)
    dt_x = _dot_exact_rhs(dt, expand, parts=1)
    to_end_x = _dot_exact_rhs(to_end, expand, parts=1)
    into_x = _dot_exact_rhs(into, expand, parts=1)
    decay_x = _dot_exact_rhs(jnp.broadcast_to(chunk_decay, (8, SSM_HEADS)), expand, parts=3)[0:1, :]

    for g in range(SSM_GROUPS):
        c0 = g * GROUP_WIDTH
        xs_g = xs_ref[:, c0:c0 + GROUP_WIDTH]
        bm_g = bm_ref[:, g * SSM_STATE:(g + 1) * SSM_STATE]
        cm_g = cm_ref[:, g * SSM_STATE:(g + 1) * SSM_STATE].astype(BF16)
        xdt_g = xs_g * dt_x[:, c0:c0 + GROUP_WIDTH]
        xdt_b = xdt_g.astype(BF16)
        cb = _dot_nt(cm_g, bm_g.astype(BF16))
        st = state_ref[g]
        y_g = _dot(cm_g, st.astype(BF16)) * into_x[:, c0:c0 + GROUP_WIDTH]
        if combine:
            y_g = y_g + yprev_ref[:, c0:c0 + GROUP_WIDTH] + dskip_ref[:, c0:c0 + GROUP_WIDTH] * xs_g
        y_ref[:, c0:c0 + GROUP_WIDTH] = y_g
        for r in range(HEADS_PER_GROUP):
            h = g * HEADS_PER_GROUP + r
            seg = a_cs[:, h:h + 1] - a_cs_t[h:h + 1, :]
            lmat = (jnp.exp(jnp.where(mask, seg, -1e30)) * cb).astype(BF16)
            lo = c0 + r * SSM_HEAD_DIM
            y_ref[:, lo:lo + SSM_HEAD_DIM] += _dot(lmat, xdt_b[:, r * SSM_HEAD_DIM:(r + 1) * SSM_HEAD_DIM])
        xdtw = (xdt_g * to_end_x[:, c0:c0 + GROUP_WIDTH]).astype(BF16)
        new = _dot(bm_g.T.astype(BF16), xdtw)
        state_ref[g] = st * decay_x[:, c0:c0 + GROUP_WIDTH] + new

    @pl.when(c == pl.num_programs(1) - 1)
    def _():
        hfin_ref[...] = state_ref[...]


def ssd_scan(xbc, dt, dt_t, a, h0, *, reverse, y_prev=None, d_skip=None, name="ssd_scan"):
    b, l, _ = xbc.shape
    q = SSM_CHUNK
    nc = l // q
    d_inner = SSM_HEADS * SSM_HEAD_DIM
    nbc = SSM_GROUPS * SSM_STATE
    combine = y_prev is not None

    def cidx(c):
        return (nc - 1 - c) if reverse else c

    xs_blocks = d_inner // d_inner
    in_specs = [
        pl.BlockSpec((None, q, d_inner), lambda i, c: (i, cidx(c), 0)),
        pl.BlockSpec((None, q, nbc), lambda i, c: (i, cidx(c), d_inner // nbc)),
        pl.BlockSpec((None, q, nbc), lambda i, c: (i, cidx(c), d_inner // nbc + 1)),
        pl.BlockSpec((None, q, SSM_HEADS), lambda i, c: (i, cidx(c), 0)),
        pl.BlockSpec((None, SSM_HEADS, q), lambda i, c: (i, 0, cidx(c))),
        pl.BlockSpec((1, SSM_HEADS), lambda i, c: (0, 0)),
        pl.BlockSpec((SSM_HEADS, 1), lambda i, c: (0, 0)),
        pl.BlockSpec((None, SSM_GROUPS, SSM_STATE, GROUP_WIDTH), lambda i, c: (i, 0, 0, 0)),
    ]
    del xs_blocks
    args = [xbc, xbc, xbc, dt, dt_t, a.reshape(1, SSM_HEADS), a.reshape(SSM_HEADS, 1), h0]
    if combine:
        in_specs += [pl.BlockSpec((None, q, d_inner), lambda i, c: (i, cidx(c), 0)),
                     pl.BlockSpec((1, d_inner), lambda i, c: (0, 0))]
        args += [y_prev, jnp.repeat(d_skip, SSM_HEAD_DIM).reshape(1, d_inner)]
    out_shape = [jax.ShapeDtypeStruct((b, l, d_inner), F32),
                 jax.ShapeDtypeStruct((b, SSM_GROUPS, SSM_STATE, GROUP_WIDTH), F32)]
    out_specs = [pl.BlockSpec((None, q, d_inner), lambda i, c: (i, cidx(c), 0)),
                 pl.BlockSpec((None, SSM_GROUPS, SSM_STATE, GROUP_WIDTH), lambda i, c: (i, 0, 0, 0))]
    kern = functools.partial(_ssd_kernel, reverse=reverse, combine=combine, q=q)
    return pl.pallas_call(
        kern, grid=(b, nc), in_specs=in_specs, out_specs=out_specs, out_shape=out_shape,
        scratch_shapes=[pltpu.VMEM((SSM_GROUPS, SSM_STATE, GROUP_WIDTH), F32)], name=name,
        compiler_params=_compiler_params(("parallel", "arbitrary")))(*args)


def _rope(x, cos, sin_up, sin_dn):
    return (x * cos + pltpu.roll(x, ROPE_PAIRS, 1) * sin_up
            + pltpu.roll(x, LANES - ROPE_PAIRS, 1) * sin_dn)


LOG2E = 1.4426950408889634


def _attn_kernel(*refs, n_lat, n_ctx, tk, rope, out_scale):
    it = iter(refs)
    q_ref, qn_ref = next(it), next(it)
    kl_ref = vl_ref = None
    if n_lat:
        kl_ref, vl_ref = next(it), next(it)
    kc_ref, vc_ref = next(it), next(it)
    q_tabs = qn_tabs = None
    if rope:
        q_tabs = [next(it) for _ in range(3)]
        qn_tabs = [next(it) for _ in range(3)]
        ck_ref, suk_ref, sdk_ref = (next(it) for _ in range(3))
    lam_ref, g_ref, o_ref, k_s, vt_s, m_s, l_s, acc_s, s_s, wq_s = (next(it) for _ in range(10))

    qi = pl.program_id(2)
    tq = q_ref.shape[0]
    n_tiles = (n_lat + n_ctx) // tk

    def query_weights(qr, tabs):
        q = qr[...] * (DIFF_SCALE * LOG2E)
        if rope:
            q = _rope(q, tabs[0][...], tabs[1][...], tabs[2][...])
        qt = q.T
        row = lax.broadcasted_iota(jnp.int32, qt.shape, 0)
        return jnp.concatenate([jnp.where(row < DIFF_HEAD_DIM, qt, 0.0),
                                jnp.where(row >= DIFF_HEAD_DIM, qt, 0.0)], axis=1).astype(BF16)

    @pl.when(qi == 0)
    def _():
        if n_lat:
            k = kl_ref[...]
            if rope:
                k = _rope(k, ck_ref[...], suk_ref[...], sdk_ref[...])
            k_s[0:n_lat, :] = k.astype(BF16)
        k_s[n_lat:n_lat + n_ctx, :] = kc_ref[...].astype(BF16)
        for t in range(n_tiles):
            lo, hi = t * tk, (t + 1) * tk
            pieces = []
            if lo < n_lat:
                pieces.append(vl_ref[lo:min(hi, n_lat), :])
            if hi > n_lat:
                pieces.append(vc_ref[max(lo, n_lat) - n_lat:hi - n_lat, :])
            v = pieces[0] if len(pieces) == 1 else jnp.concatenate(pieces, axis=0)
            vt_s[t] = v.T.astype(BF16)
        w_first = query_weights(q_ref, q_tabs)
        wq_s[...] = w_first
        s_s[0] = _dot(k_s[0:tk, :], w_first)

    wq = wq_s[...]

    m_s[...] = jnp.full(m_s.shape, -1e30, F32)
    l_s[...] = jnp.zeros(l_s.shape, F32)
    acc_s[...] = jnp.zeros(acc_s.shape, F32)

    def scores(t):
        off = pl.multiple_of(t * tk, tk)
        return _dot(k_s[pl.ds(off, tk), :], wq)

    def absorb(buf, t):
        s = s_s[buf]
        m_old = m_s[...]
        m_new = jnp.maximum(m_old, jnp.max(s, axis=0, keepdims=True))
        alpha = jnp.exp2(m_old - m_new)
        p = jnp.exp2(s - m_new)
        l_s[...] = alpha * l_s[...] + jnp.sum(p, axis=0, keepdims=True)
        acc_s[...] = alpha * acc_s[...] + _dot(vt_s[t], p.astype(BF16))
        m_s[...] = m_new


    def body(i, carry):
        t0 = 2 * i
        s_s[1] = scores(t0 + 1)
        absorb(0, t0)
        s_s[0] = scores(t0 + 2)
        absorb(1, t0 + 1)
        return carry

    lax.fori_loop(0, n_tiles // 2 - 1, body, 0)
    s_s[1] = scores(n_tiles - 1)
    absorb(0, n_tiles - 2)
    w_next = query_weights(qn_ref, qn_tabs)
    wq_s[...] = w_next
    s_s[0] = _dot(k_s[0:tk, :], w_next)
    absorb(1, n_tiles - 1)

    o_t = acc_s[...] / l_s[...]
    o = (o_t[:, :tq] - lam_ref[0:1, 0:1] * o_t[:, tq:]).T
    ms = jnp.mean(o * o, axis=-1, keepdims=True)
    o_ref[...] = o * lax.rsqrt(ms + EPS) * (g_ref[...] * out_scale)


ATTN_TILES = dict(tq=512, tk=1408)
ATTN_TILES_CTX = dict(tq=256, tk=128)


def diff_attention(qkv_q, qkv_lat, qkv_ctx, lam, subln_g, out_scale, rope_tabs, *, tq, tk, name):
    b, lq, d3 = qkv_q.shape
    d = d3 // 3
    nh = d // DIFF_V_DIM
    n_lat = 0 if qkv_lat is None else qkv_lat.shape[1]
    n_ctx = qkv_ctx.shape[1]
    rope = rope_tabs is not None
    tq = min(tq, lq)
    tk = min(tk, n_lat + n_ctx)
    assert lq % tq == 0 and (n_lat + n_ctx) % (2 * tk) == 0

    nq = lq // tq
    nxt = lambda t: jnp.minimum(t + 1, nq - 1)
    args = [qkv_q, qkv_q]
    in_specs = [pl.BlockSpec((None, tq, DIFF_V_DIM), lambda i, h, t: (i, t, h)),
                pl.BlockSpec((None, tq, DIFF_V_DIM), lambda i, h, t: (i, nxt(t), h))]
    if n_lat:
        args += [qkv_lat, qkv_lat]
        in_specs += [pl.BlockSpec((None, n_lat, DIFF_V_DIM), lambda i, h, t: (i, 0, nh + h)),
                     pl.BlockSpec((None, n_lat, DIFF_V_DIM), lambda i, h, t: (i, 0, 2 * nh + h))]
    args += [qkv_ctx, qkv_ctx]
    in_specs += [pl.BlockSpec((None, n_ctx, DIFF_V_DIM), lambda i, h, t: (i, 0, nh + h)),
                 pl.BlockSpec((None, n_ctx, DIFF_V_DIM), lambda i, h, t: (i, 0, 2 * nh + h))]
    if rope:
        args += list(rope_tabs) * 3
        in_specs += [pl.BlockSpec((tq, LANES), lambda i, h, t: (t, 0))] * 3
        in_specs += [pl.BlockSpec((tq, LANES), lambda i, h, t: (nxt(t), 0))] * 3
        in_specs += [pl.BlockSpec((n_lat, LANES), lambda i, h, t: (0, 0), pipeline_mode=pl.Buffered(1))] * 3
    args += [jnp.broadcast_to(lam.astype(F32).reshape(1, 1), (8, LANES)), subln_g.reshape(1, DIFF_V_DIM)]
    in_specs += [pl.BlockSpec((8, LANES), lambda i, h, t: (0, 0)),
                 pl.BlockSpec((1, DIFF_V_DIM), lambda i, h, t: (0, 0))]

    kern = functools.partial(_attn_kernel, n_lat=n_lat, n_ctx=n_ctx, tk=tk, rope=rope, out_scale=out_scale)
    return pl.pallas_call(
        kern, grid=(b, nh, lq // tq), in_specs=in_specs,
        out_specs=pl.BlockSpec((None, tq, DIFF_V_DIM), lambda i, h, t: (i, t, h)),
        out_shape=jax.ShapeDtypeStruct((b, lq, d), F32),
        scratch_shapes=[pltpu.VMEM((n_lat + n_ctx, DIFF_V_DIM), BF16),
                        pltpu.VMEM(((n_lat + n_ctx) // tk, DIFF_V_DIM, tk), BF16),
                        pltpu.VMEM((1, 2 * tq), F32), pltpu.VMEM((1, 2 * tq), F32),
                        pltpu.VMEM((DIFF_V_DIM, 2 * tq), F32),
                        pltpu.VMEM((2, tk, 2 * tq), F32),
                        pltpu.VMEM((DIFF_V_DIM, 2 * tq), BF16)],
        name=name,
        compiler_params=_compiler_params(("parallel", "parallel", "arbitrary")))(*args)


def _rope_tables(l):
    rows = l // GRID_W
    row = jnp.repeat(jnp.arange(rows), GRID_W).astype(F32)
    col = jnp.tile(jnp.arange(GRID_W), rows).astype(F32)
    inv = ROPE_THETA ** (-jnp.arange(ROPE_PAIRS, dtype=F32) / ROPE_PAIRS)
    lane = jnp.arange(LANES)
    within = lane % DIFF_HEAD_DIM
    axis = within // (2 * ROPE_PAIRS)
    second = (within // ROPE_PAIRS) % 2
    pos = jnp.where(axis[None, :] == 0, row[:, None], col[:, None])
    ang = pos * inv[within % ROPE_PAIRS][None, :]
    cos, sin = jnp.cos(ang), jnp.sin(ang)
    sin_up = jnp.where(second[None, :] == 1, sin, 0.0)
    sin_dn = jnp.where(second[None, :] == 0, -sin, 0.0)
    return cos, sin_up, sin_dn


FFN_F_SPLIT = 1


def _ffn_kernel(x_ref, w1_ref, w3_ref, w2_ref, gate_ref, o_ref, w1_s, w3_s, w2_s):
    @pl.when(pl.program_id(1) == 0)
    def _():
        w1_s[...] = w1_ref[...].astype(BF16)
        w3_s[...] = w3_ref[...].astype(BF16)
        w2_s[...] = w2_ref[...].astype(BF16)

    x = x_ref[...]
    fs = w1_s.shape[1] // FFN_F_SPLIT
    acc = None
    for s in range(FFN_F_SPLIT):
        h1 = _dot(x, w1_s[:, s * fs:(s + 1) * fs])
        h3 = _dot(x, w3_s[:, s * fs:(s + 1) * fs])
        hid = (h1 * jax.nn.sigmoid(h1) * h3).astype(BF16)
        part = _dot(hid, w2_s[s * fs:(s + 1) * fs, :])
        acc = part if acc is None else acc + part
    o_ref[...] = acc * gate_ref[...]


def _ffn_gather_kernel(rows_ref, x_hbm, g_ref, sh_ref, sc_ref, w1_ref, w3_ref, w2_ref, gate_ref, o_ref,
                       xbuf, sem, w1_s, w3_s, w2_s):
    e, t = pl.program_id(0), pl.program_id(1)
    tiles = pl.num_programs(1)
    tm = xbuf.shape[1]
    step = e * tiles + t
    last = pl.num_programs(0) * tiles - 1
    slot = lax.rem(step, 2)

    def gather(step_id, slot_id):
        for i in range(tm):
            row = rows_ref[step_id * tm + i]
            pltpu.make_async_copy(x_hbm.at[pl.ds(row, 1), :], xbuf.at[slot_id, pl.ds(i, 1), :],
                                  sem.at[slot_id]).start(priority=i % 2)

    @pl.when(step == 0)
    def _():
        gather(0, 0)

    @pl.when(t == 0)
    def _():
        w1_s[...] = w1_ref[...].astype(BF16)
        w3_s[...] = w3_ref[...].astype(BF16)
        w2_s[...] = w2_ref[...].astype(BF16)

    pltpu.make_async_copy(x_hbm.at[pl.ds(0, tm), :], xbuf.at[slot], sem.at[slot]).wait()
    xr = xbuf[slot]
    ms = jnp.mean(xr * xr, axis=-1, keepdims=True)
    x = (xr * lax.rsqrt(ms + EPS) * g_ref[...] * (1.0 + sc_ref[...]) + sh_ref[...]).astype(BF16)

    gather(jnp.where(step < last, step + 1, 0), 1 - slot)

    fs = w1_s.shape[1] // FFN_F_SPLIT
    acc = None
    for s in range(FFN_F_SPLIT):
        h1 = _dot(x, w1_s[:, s * fs:(s + 1) * fs])
        h3 = _dot(x, w3_s[:, s * fs:(s + 1) * fs])
        hid = (h1 * jax.nn.sigmoid(h1) * h3).astype(BF16)
        part = _dot(hid, w2_s[s * fs:(s + 1) * fs, :])
        acc = part if acc is None else acc + part
    o_ref[...] = acc * gate_ref[...]

    @pl.when(step == last)
    def _():
        pltpu.make_async_copy(x_hbm.at[pl.ds(0, tm), :], xbuf.at[1 - slot], sem.at[1 - slot]).wait()


def _with_layer(w):
    return w if isinstance(w, tuple) else (w[None], 0)


def expert_ffn_gather(x2d, rows, g, shift, scale, w1, w3, w2, gate, *, rows_per_batch, tm=512,
                      name="expert_ffn_gather"):
    d = x2d.shape[1]
    (w1, l1), (w3, l3), (w2, l2) = _with_layer(w1), _with_layer(w3), _with_layer(w2)
    _, e, _, f = w1.shape
    r = rows.shape[0] // e
    tm = min(tm, r, rows_per_batch)
    assert r % tm == 0 and rows_per_batch % tm == 0 and f % FFN_F_SPLIT == 0
    tiles = r // tm
    tiles_per_batch = rows_per_batch // tm
    nb = shift.shape[0]
    once = pl.Buffered(1)
    bsel = lambda i, t, rows_ref: ((t // tiles_per_batch) % nb, 0, 0)
    grid_spec = pltpu.PrefetchScalarGridSpec(
        num_scalar_prefetch=1, grid=(e, tiles),
        in_specs=[pl.BlockSpec(memory_space=pl.ANY),
                  pl.BlockSpec((1, d), lambda i, t, rows_ref: (0, 0)),
                  pl.BlockSpec((None, 1, d), bsel),
                  pl.BlockSpec((None, 1, d), bsel),
                  pl.BlockSpec((None, None, d, f), lambda i, t, rows_ref: (l1, i, 0, 0), pipeline_mode=once),
                  pl.BlockSpec((None, None, d, f), lambda i, t, rows_ref: (l3, i, 0, 0), pipeline_mode=once),
                  pl.BlockSpec((None, None, f, d), lambda i, t, rows_ref: (l2, i, 0, 0), pipeline_mode=once),
                  pl.BlockSpec((None, tm, 1), lambda i, t, rows_ref: (i, t, 0))],
        out_specs=pl.BlockSpec((None, tm, d), lambda i, t, rows_ref: (i, t, 0)),
        scratch_shapes=[pltpu.VMEM((2, tm, d), F32), pltpu.SemaphoreType.DMA((2,)),
                        pltpu.VMEM((d, f), BF16), pltpu.VMEM((d, f), BF16), pltpu.VMEM((f, d), BF16)])
    return pl.pallas_call(
        _ffn_gather_kernel, grid_spec=grid_spec,
        out_shape=jax.ShapeDtypeStruct((e, r, d), F32), name=name,
        compiler_params=_compiler_params(("arbitrary", "arbitrary")))(
            rows, x2d, g.reshape(1, d), shift, scale, w1, w3, w2, gate)


def expert_ffn(xs, w1, w3, w2, gate, *, tm=512, name="expert_ffn"):
    e, r, d = xs.shape
    (w1, l1), (w3, l3), (w2, l2) = _with_layer(w1), _with_layer(w3), _with_layer(w2)
    f = w1.shape[3]
    tm = min(tm, r)
    assert r % tm == 0 and f % FFN_F_SPLIT == 0
    once = pl.Buffered(1)
    return pl.pallas_call(
        _ffn_kernel, grid=(e, r // tm),
        in_specs=[pl.BlockSpec((None, tm, d), lambda i, t: (i, t, 0)),
                  pl.BlockSpec((None, None, d, f), lambda i, t: (l1, i, 0, 0), pipeline_mode=once),
                  pl.BlockSpec((None, None, d, f), lambda i, t: (l3, i, 0, 0), pipeline_mode=once),
                  pl.BlockSpec((None, None, f, d), lambda i, t: (l2, i, 0, 0), pipeline_mode=once),
                  pl.BlockSpec((None, tm, 1), lambda i, t: (i, t, 0))],
        out_specs=pl.BlockSpec((None, tm, d), lambda i, t: (i, t, 0)),
        out_shape=jax.ShapeDtypeStruct((e, r, d), F32),
        scratch_shapes=[pltpu.VMEM((d, f), BF16), pltpu.VMEM((d, f), BF16), pltpu.VMEM((f, d), BF16)],
        name=name,
        compiler_params=_compiler_params(("arbitrary", "arbitrary")))(xs, w1, w3, w2, gate)


ROUTE_CHUNK = 1024


def _lane_prefix_sum(x):
    n = x.shape[-1]
    lane = lax.broadcasted_iota(jnp.int32, x.shape, x.ndim - 1)
    shift = 1
    while shift < n:
        x = x + jnp.where(lane >= shift, pltpu.roll(x, shift, x.ndim - 1), 0)
        shift *= 2
    return x


def _route_kernel(logit_ref, idx_ref, gate_ref, key_s, aff_s, cnt_s, acc_s, *, cap):
    n = logit_ref.shape[0]
    lt = logit_ref[...].T[:N_EXPERTS, :]
    ex = jnp.exp(lt - jnp.max(lt, axis=0, keepdims=True))
    aff = ex / jnp.sum(ex, axis=0, keepdims=True)
    bits = pltpu.bitcast(aff, jnp.int32)

    def bit_step(i, thr):
        trial = thr | jnp.left_shift(jnp.int32(1), 30 - i)
        cnt = jnp.sum((bits >= trial).astype(jnp.int32), axis=1, keepdims=True)
        return jnp.where(cnt >= cap, trial, thr)

    thr = lax.fori_loop(0, 31, bit_step, jnp.zeros((N_EXPERTS, 1), jnp.int32))
    gt = bits > thr
    eq = (bits == thr).astype(jnp.int32)
    need = cap - jnp.sum(gt.astype(jnp.int32), axis=1, keepdims=True)
    sel = jnp.where(gt, 1, jnp.where((_lane_prefix_sum(eq) - eq) < need, eq, 0))
    key = jnp.where(sel > 0, _lane_prefix_sum(sel) - sel, -1)

    chunk = key_s.shape[2]
    n_chunks = n // chunk
    st = min(ROUTE_SLOT_TILE, cap)
    count = _lane_prefix_sum(sel)
    for c in range(n_chunks):
        lanes = slice(c * chunk, (c + 1) * chunk)
        key_s[c] = key[:, lanes]
        aff_s[c] = aff[:, lanes]
        cnt_s[c] = count[:, (c + 1) * chunk - LANES:(c + 1) * chunk]
    acc_s[...] = jnp.zeros(acc_s.shape, F32)
    exact = lambda v: v.astype(BF16).astype(F32)
    slot0 = lax.broadcasted_iota(jnp.int32, (st, chunk), 0)
    lane = lax.broadcasted_iota(jnp.int32, (1, chunk), 1)

    def per_expert(e, carry):
        def per_chunk(c, lo):
            hi = cnt_s[c, pl.ds(e, 1), LANES - 1:LANES][0, 0]
            a = aff_s[c, pl.ds(e, 1), :]
            a_hi = exact(a)
            a_mid = exact(a - a_hi)
            a_lo = exact(a - a_hi - a_mid)
            tok = lane + c * chunk
            rows = jnp.concatenate([(tok // LANES).astype(F32), (tok % LANES).astype(F32), a_hi, a_mid, a_lo,
                                    jnp.zeros((11, chunk), F32)], axis=0).astype(BF16)
            k = key_s[c, pl.ds(e, 1), :]

            def per_tile(t, carry2):
                r0 = pl.multiple_of(t * st, st)
                onehot = jnp.where(k == slot0 + r0, 1.0, 0.0).astype(BF16)
                acc_s[e, pl.ds(r0, st), 0:16] += _dot_nt(onehot, rows)
                return carry2

            lax.fori_loop(lo // st, jnp.where(hi > lo, (hi - 1) // st + 1, lo // st), per_tile, 0)
            return hi

        lax.fori_loop(0, n_chunks, per_chunk, jnp.int32(0))
        acc_t = acc_s[e].T
        idx_ref[pl.ds(e, 1), :] = (acc_t[0:1] * LANES + acc_t[1:2]).astype(jnp.int32)
        gate_ref[pl.ds(e, 1), :] = acc_t[2:3] + acc_t[3:4] + acc_t[4:5]
        return carry

    lax.fori_loop(0, N_EXPERTS, per_expert, 0)


ROUTE_SLOT_TILE = 256


def route(logits, b, n, cap, name="moe_route"):
    kern = functools.partial(_route_kernel, cap=cap)
    chunk = min(ROUTE_CHUNK, n)
    assert n % chunk == 0 and cap % min(ROUTE_SLOT_TILE, cap) == 0
    return pl.pallas_call(
        kern, grid=(b,),
        in_specs=[pl.BlockSpec((n, LANES), lambda i: (i, 0))],
        out_specs=[pl.BlockSpec((None, N_EXPERTS, cap), lambda i: (i, 0, 0)),
                   pl.BlockSpec((None, N_EXPERTS, cap), lambda i: (i, 0, 0))],
        out_shape=[jax.ShapeDtypeStruct((b, N_EXPERTS, cap), jnp.int32),
                   jax.ShapeDtypeStruct((b, N_EXPERTS, cap), F32)],
        scratch_shapes=[pltpu.VMEM((n // chunk, N_EXPERTS, chunk), jnp.int32),
                        pltpu.VMEM((n // chunk, N_EXPERTS, chunk), F32),
                        pltpu.VMEM((n // chunk, N_EXPERTS, LANES), jnp.int32),
                        pltpu.VMEM((N_EXPERTS, cap, LANES), F32)],
        name=name, compiler_params=_compiler_params(("parallel",)))(logits)


COMBINE_ROWS = 256
COMBINE_GROUP = 8


def _combine_kernel(idx_ref, ys_ref, o_ref, *, cap):
    bb, e, j = pl.program_id(0), pl.program_id(1), pl.program_id(2)
    rows = ys_ref.shape[0]

    @pl.when((e == 0) & (j == 0))
    def _():
        o_ref[...] = jnp.zeros(o_ref.shape, F32)

    base = (bb * pl.num_programs(1) + e) * cap + j * rows

    def group(g, carry):
        r = g * COMBINE_GROUP
        ids = [idx_ref[base + r + u] for u in range(COMBINE_GROUP)]
        old = [o_ref[ids[u]] for u in range(COMBINE_GROUP)]
        ys = ys_ref[pl.ds(pl.multiple_of(r, COMBINE_GROUP), COMBINE_GROUP), :]
        ys = ys.reshape(COMBINE_GROUP, o_ref.shape[1], LANES)
        for u in range(COMBINE_GROUP):
            o_ref[ids[u]] = old[u] + ys[u]
        return carry

    lax.fori_loop(0, rows // COMBINE_GROUP, group, 0)


def combine(ys, idx, n, name="moe_combine"):
    e, r, d = ys.shape
    b, _, cap = idx.shape
    sub = d // LANES
    rows = min(COMBINE_ROWS, cap)
    assert cap % rows == 0 and rows % COMBINE_GROUP == 0
    tiles = cap // rows
    grid_spec = pltpu.PrefetchScalarGridSpec(
        num_scalar_prefetch=1, grid=(b, e, tiles),
        in_specs=[pl.BlockSpec((None, rows, d), lambda i, k, j, idx_ref: (k, i * tiles + j, 0))],
        out_specs=pl.BlockSpec((None, n, sub, LANES), lambda i, k, j, idx_ref: (i, 0, 0, 0),
                               pipeline_mode=pl.Buffered(1)))
    out = pl.pallas_call(
        functools.partial(_combine_kernel, cap=cap), grid_spec=grid_spec,
        out_shape=jax.ShapeDtypeStruct((b, n, sub, LANES), F32), name=name,
        compiler_params=_compiler_params(("arbitrary", "arbitrary", "arbitrary")))(
            idx.reshape(-1), ys)
    return out.reshape(b, n, d)


def _pad_cols(w, n):
    return jnp.pad(w, ((0, 0), (0, n - w.shape[1])))


def _ssd_project(x2d, g, shift, scale, rows_per_batch, w_main, w_dt, conv_w, conv_b, dt_bias, b, l):
    d_inner = SSM_HEADS * SSM_HEAD_DIM
    main = fused_matmul(x2d, w_main, prologue="norm_mod", g=g, shift=shift, scale=scale,
                        rows_per_batch=rows_per_batch, out_dtype=BF16, tm=1024, name="ssd_in_proj")
    dt_raw = fused_matmul(x2d, w_dt, prologue="norm_mod", g=g, shift=shift, scale=scale,
                          rows_per_batch=rows_per_batch, precise=True, name="ssd_dt_proj")
    xbc = conv_silu(main.reshape(b, l, -1), d_inner, conv_w, conv_b)
    dt = jax.nn.softplus(dt_raw[:, :2 * SSM_HEADS].reshape(b, l, 2, SSM_HEADS) + dt_bias)
    return main, xbc, dt


def _ssd_bidirectional(xbc, dt, a, d_skip, h0_f, h0_b):
    dt_f, dt_b = dt[:, :, 0], dt[:, :, 1]
    y_f, hf = ssd_scan(xbc, dt_f, jnp.swapaxes(dt_f, 1, 2), a[0], h0_f, reverse=False, name="ssd_scan_fwd")
    y, hb = ssd_scan(xbc, dt_b, jnp.swapaxes(dt_b, 1, 2), a[1], h0_b, reverse=True, y_prev=y_f, d_skip=d_skip,
                     name="ssd_scan_bwd")
    return y, hf, hb


def _moe(x2d, g, shift, scale, gate_mod, rows_per_batch, b, n, router_w, w1, w3, w2):
    d = x2d.shape[1]
    cap = EC_CAPACITY_FACTOR * n // N_EXPERTS
    in_pallas = n % ROUTE_CHUNK == 0
    router_w = _pad_cols(router_w, LANES)
    bi = jnp.arange(b)[:, None, None]
    if in_pallas:
        logits = fused_matmul(x2d, router_w, prologue="norm_mod", g=g, shift=shift, scale=scale,
                              rows_per_batch=rows_per_batch, precise=True, name="moe_router")
        idx, gate = route(logits, b, n, cap)
        rows = jnp.swapaxes(idx + bi * n, 0, 1).reshape(-1)
        gate_e = jnp.swapaxes(gate, 0, 1).reshape(N_EXPERTS, b * cap, 1)
        ys = expert_ffn_gather(x2d, rows, g, shift, scale, w1, w3, w2, gate_e, rows_per_batch=cap)
        moe = combine(ys, idx, n)
    else:
        logits, h = fused_matmul(x2d, router_w, prologue="norm_mod", g=g, shift=shift, scale=scale,
                                 rows_per_batch=rows_per_batch, precise=True, emit_h=BF16, name="moe_router_ctx")
        aff = jax.nn.softmax(logits[:, :N_EXPERTS].reshape(b, n, N_EXPERTS), axis=-1)
        gate, idx = lax.top_k(jnp.swapaxes(aff, 1, 2), cap)
        xs = jnp.swapaxes(h.reshape(b, n, d)[bi, idx], 0, 1).reshape(N_EXPERTS, b * cap, d)
        gate_e = jnp.swapaxes(gate, 0, 1).reshape(N_EXPERTS, b * cap, 1)
        ys = expert_ffn(xs, w1, w3, w2, gate_e)
        ys = jnp.swapaxes(ys.reshape(N_EXPERTS, b, cap, d), 0, 1)
        moe = jnp.zeros((b, n, d), F32).at[bi, idx].add(ys)
    x3 = x2d.reshape(b, n, d)
    gm = gate_mod if gate_mod.shape[0] == b else jnp.broadcast_to(gate_mod, (b, 1, d))
    return (x3 + gm * moe).reshape(b * n, d)


def _dft_tables(l, ch):
    def cs(nn):
        idx = jnp.arange(nn)
        ang = ((idx[:, None] * idx[None, :]) % nn).astype(F32) * (2.0 * math.pi / nn)
        s = 1.0 / math.sqrt(nn)
        return jnp.cos(ang) * s, jnp.sin(ang) * s
    cl, sl = cs(l)
    cc, sc = cs(ch)
    return cl, sl, cc, sc


def _fourier(h2d, b, l, out_w, out_b, gate, res, rows_per_batch):
    d = h2d.shape[1]
    ch = d // FNET_GROUPS
    cl, sl, cc, sc = _dft_tables(l, ch)
    eye = jnp.eye(FNET_GROUPS, dtype=F32)
    w_ch = jnp.concatenate([jnp.kron(eye, cc), jnp.kron(eye, sc)], axis=1).astype(BF16)
    pq = fused_matmul(h2d, w_ch, out_dtype=BF16, name="fnet_channel_dft")
    pq = pq.reshape(b, l, 2, d)
    rhs = jnp.transpose(pq, (2, 1, 0, 3)).reshape(2 * l, b * d)
    lhs = jnp.concatenate([cl, -sl], axis=1).astype(BF16)
    mixed = fused_matmul(lhs, rhs, out_dtype=BF16, tk=2048, name="fnet_position_dft")
    mixed = jnp.transpose(mixed.reshape(l, b, d), (1, 0, 2)).reshape(b * l, d)
    return fused_matmul(mixed, out_w.astype(BF16), bias=out_b, gate=gate, res=res,
                        rows_per_batch=rows_per_batch, name="fnet_out_proj")


FFT_N1 = 32
FFT_ROWS = 8
FFT_CH = 128
FFT_UNROLL = 4


def _fnet_kernel(x_ref, cc_ref, sc_ref, kc_ref, ks_ref, mc_ref, ms_ref, o_ref, zr_s, zq_s, ar_s, ai_s):
    n1, nj, rows, ch = zr_s.shape
    n2 = nj * rows
    x = x_ref[...]
    zr_s[...] = _dot(x, cc_ref[...]).reshape(zr_s.shape)
    zq_s[...] = _dot(x, sc_ref[...]).reshape(zq_s.shape)
    kc, ks = kc_ref[...], ks_ref[...]

    def stage1(j, carry):
        p = zr_s[:, j].reshape(n1 * rows, ch).astype(BF16)
        q = zq_s[:, j].reshape(n1 * rows, ch).astype(BF16)
        ar_s[:, j] = (_dot(kc, p) - _dot(ks, q)).reshape(n1, rows, ch)
        ai_s[:, j] = (_dot(kc, q) + _dot(ks, p)).reshape(n1, rows, ch)
        return carry

    lax.fori_loop(0, nj, stage1, 0, unroll=FFT_UNROLL)

    def stage2(k1, carry):
        ar = ar_s[k1].reshape(n2, ch).astype(BF16)
        ai = ai_s[k1].reshape(n2, ch).astype(BF16)
        o_ref[k1] = (_dot(mc_ref[k1], ar) - _dot(ms_ref[k1], ai)).astype(o_ref.dtype)
        return carry

    lax.fori_loop(0, n1, stage2, 0, unroll=FFT_UNROLL)


def _fnet_tables(l, ch):
    n1, n2 = FFT_N1, l // FFT_N1
    def cs(num, den, scale):
        ang = (num % den).astype(F32) * (2.0 * math.pi / den)
        return jnp.cos(ang) * scale, jnp.sin(ang) * scale
    ic = jnp.arange(ch)
    cc, sc = cs(ic[:, None] * ic[None, :], ch, 1.0 / math.sqrt(ch))
    i1 = jnp.arange(n1)
    c1, s1 = cs(i1[:, None] * i1[None, :], n1, 1.0)
    eye = jnp.eye(FFT_ROWS, dtype=F32)
    kc, ks = jnp.kron(c1, eye), jnp.kron(s1, eye)
    k = i1[:, None, None] + n1 * jnp.arange(n2)[None, :, None]
    mc, ms = cs(k * jnp.arange(n2)[None, None, :], l, 1.0 / math.sqrt(l))
    return [t.astype(BF16) for t in (cc, sc, kc, ks, mc, ms)]


def fourier_mix(h, name="fnet_mix"):
    b, l, d = h.shape
    ch = d // FNET_GROUPS
    n1, n2 = FFT_N1, l // FFT_N1
    halves = ch // FFT_CH
    cc, sc, kc, ks, mc, ms = _fnet_tables(l, ch)
    once = pl.Buffered(1)
    const2 = lambda i, g, s: (0, 0)
    return pl.pallas_call(
        _fnet_kernel, grid=(b, FNET_GROUPS, halves),
        in_specs=[pl.BlockSpec((None, l, ch), lambda i, g, s: (i, 0, g)),
                  pl.BlockSpec((ch, FFT_CH), lambda i, g, s: (0, s)),
                  pl.BlockSpec((ch, FFT_CH), lambda i, g, s: (0, s)),
                  pl.BlockSpec((n1 * FFT_ROWS, n1 * FFT_ROWS), const2, pipeline_mode=once),
                  pl.BlockSpec((n1 * FFT_ROWS, n1 * FFT_ROWS), const2, pipeline_mode=once),
                  pl.BlockSpec((n1, n2, n2), lambda i, g, s: (0, 0, 0), pipeline_mode=once),
                  pl.BlockSpec((n1, n2, n2), lambda i, g, s: (0, 0, 0), pipeline_mode=once)],
        out_specs=pl.BlockSpec((None, n1, n2, FFT_CH), lambda i, g, s: (i, 0, 0, g * halves + s)),
        out_shape=jax.ShapeDtypeStruct((b, n1, n2, d), BF16),
        scratch_shapes=[pltpu.VMEM((n1, n2 // FFT_ROWS, FFT_ROWS, FFT_CH), F32) for _ in range(4)],
        name=name,
        compiler_params=_compiler_params(("parallel", "arbitrary", "arbitrary")))(h, cc, sc, kc, ks, mc, ms)


def fnet_out_proj(mixed, w, bias, gate, res, name="fnet_out_proj"):
    b, n1, n2, d = mixed.shape
    n = w.shape[1]
    res3 = res.reshape(b, n2, n1 * n)
    kern = functools.partial(_mm_kernel, prologue=None, precise=False, has_bias=True, has_gate_res=True,
                             emit_h=False, nk=1)
    rows = lambda i, j, k: (i // n1, 0, i % n1)
    out = pl.pallas_call(
        kern, grid=(b * n1, 1, 1),
        in_specs=[pl.BlockSpec((None, None, n2, d), lambda i, j, k: (i // n1, i % n1, 0, 0)),
                  pl.BlockSpec((d, n), lambda i, j, k: (0, 0)),
                  pl.BlockSpec((1, n), lambda i, j, k: (0, 0)),
                  pl.BlockSpec((None, 1, n), lambda i, j, k: (i // n1, 0, 0)),
                  pl.BlockSpec((None, n2, n), rows)],
        out_specs=pl.BlockSpec((None, n2, n), rows),
        out_shape=jax.ShapeDtypeStruct((b, n2, n1 * n), F32), name=name,
        compiler_params=_compiler_params(("parallel", "arbitrary", "arbitrary")))(
            mixed, w, bias.reshape(1, n), gate, res3)
    return out.reshape(b * n2 * n1, n)


CONV_HALO = 16


CONV_BLOCK = 128


def _conv_kernel(prev_ref, x_ref, next_ref, w_ref, b_ref, o_ref, *, taps):
    i = pl.program_id(1)
    last = pl.num_programs(1) - 1
    tl = x_ref.shape[0]
    zero = jnp.zeros(prev_ref.shape, prev_ref.dtype)
    prev = jnp.where(i > 0, prev_ref[...], zero)
    nxt = jnp.where(i < last, next_ref[...], zero)
    ext = jnp.concatenate([prev, x_ref[...], nxt], axis=0)
    span = CONV_BLOCK + 2 * CONV_HALO
    r_out = lax.broadcasted_iota(jnp.int32, (CONV_BLOCK, span), 0)
    r_in = lax.broadcasted_iota(jnp.int32, (CONV_BLOCK, span), 1)
    shifts = [jnp.where(r_in == r_out + (CONV_HALO - taps // 2 + k), 1.0, 0.0).astype(BF16)
              for k in range(taps)]
    for blk in range(tl // CONV_BLOCK):
        src = ext[blk * CONV_BLOCK:blk * CONV_BLOCK + span, :]
        acc = b_ref[...]
        for k in range(taps):
            acc = acc + _dot(shifts[k], src) * w_ref[k:k + 1, :]
        o_ref[blk * CONV_BLOCK:(blk + 1) * CONV_BLOCK, :] = acc * jax.nn.sigmoid(acc)


def conv_silu(main, col0, conv_w, conv_b, *, tl=512, tc=1024, name="ssd_conv"):
    b, l, _ = main.shape
    taps, c = conv_w.shape
    tl = min(tl, l)
    assert l % tl == 0 and c % tc == 0 and col0 % tc == 0 and tl % CONV_BLOCK == 0 and main.dtype == BF16
    cb = col0 // tc
    hb = tl // CONV_HALO
    nh = l // CONV_HALO
    kern = functools.partial(_conv_kernel, taps=taps)
    return pl.pallas_call(
        kern, grid=(b, l // tl, c // tc),
        in_specs=[pl.BlockSpec((None, CONV_HALO, tc), lambda n, i, j: (n, jnp.maximum(i * hb - 1, 0), cb + j)),
                  pl.BlockSpec((None, tl, tc), lambda n, i, j: (n, i, cb + j)),
                  pl.BlockSpec((None, CONV_HALO, tc), lambda n, i, j: (n, jnp.minimum((i + 1) * hb, nh - 1), cb + j)),
                  pl.BlockSpec((taps, tc), lambda n, i, j: (0, j)),
                  pl.BlockSpec((1, tc), lambda n, i, j: (0, j))],
        out_specs=pl.BlockSpec((None, tl, tc), lambda n, i, j: (n, i, j)),
        out_shape=jax.ShapeDtypeStruct((b, l, c), F32), name=name,
        compiler_params=_compiler_params(("parallel", "parallel", "parallel")))(
            main, main, main, conv_w, conv_b.reshape(1, c))


def kernel(x, c, ctx, c_ctx, mod_w, mod_b, norm1_g, norm2_g, ssd_in_w, ssd_conv_w, ssd_conv_b, ssd_dt_bias,
           ssd_a_log, ssd_d, ssd_norm_g, ssd_out_w, fnet_out_w, fnet_out_b, diff_qkv_w, diff_out_w, diff_lambda,
           diff_subln_g, router_w, moe_w1, moe_w3, moe_w2, final_g):
    b, l, d = x.shape
    lc = ctx.shape[1]
    d_inner = SSM_HEADS * SSM_HEAD_DIM

    cond = jnp.concatenate([c, c_ctx[None], jnp.zeros(((8 - (b + 1) % 8) % 8, d), F32)], axis=0)
    cond = jax.nn.silu(cond)
    xl = x.reshape(b * l, d)
    xc = ctx.reshape(b * lc, d)

    for i in range(DEPTH):
        need_ctx = i < DEPTH - 1
        kind, j = i % N_MIXERS, i // N_MIXERS
        mod = fused_matmul(cond, mod_w[i], bias=mod_b[i], tn=1024, name="modulation")
        m_l = [mod[:b, t * d:(t + 1) * d].reshape(b, 1, d) for t in range(6)]
        m_c = [mod[b:b + 1, t * d:(t + 1) * d].reshape(1, 1, d) for t in range(6)]

        if kind == 0:
            w_main = ssd_in_w[j][:, :d_inner + d_inner + 2 * SSM_GROUPS * SSM_STATE].astype(BF16)
            w_dt = _pad_cols(ssd_in_w[j][:, w_main.shape[1]:], LANES)
            a = -jnp.exp(ssd_a_log[j].astype(F32))
            h0 = jnp.zeros((b, SSM_GROUPS, SSM_STATE, GROUP_WIDTH), F32)
            z_c, xbc_c, dt_c = _ssd_project(xc, norm1_g[i], m_c[0], m_c[1], None, w_main, w_dt,
                                            ssd_conv_w[j], ssd_conv_b[j], ssd_dt_bias[j], b, lc)
            y_c, hf, hb = _ssd_bidirectional(xbc_c, dt_c, a, ssd_d[j], h0, h0)
            z_l, xbc_l, dt_l = _ssd_project(xl, norm1_g[i], m_l[0], m_l[1], l, w_main, w_dt,
                                            ssd_conv_w[j], ssd_conv_b[j], ssd_dt_bias[j], b, l)
            y_l, _, _ = _ssd_bidirectional(xbc_l, dt_l, a, ssd_d[j], hf, hb)
            w_out = ssd_out_w[j].astype(BF16)
            xl = fused_matmul(y_l.reshape(b * l, d_inner), w_out, prologue="gated_norm",
                              z=z_l, g=ssd_norm_g[j], gate=m_l[2], res=xl,
                              rows_per_batch=l, name="ssd_out_proj")
            if need_ctx:
                xc = fused_matmul(y_c.reshape(b * lc, d_inner), w_out, prologue="gated_norm",
                                  z=z_c, g=ssd_norm_g[j], gate=m_c[2], res=xc,
                                  name="ssd_out_proj_ctx")
        elif kind == 1:
            h_l = norm_modulate(xl, norm1_g[i], m_l[0], m_l[1], rows_per_batch=l, out_dtype=BF16, name="fnet_norm")
            mixed = fourier_mix(h_l.reshape(b, l, d))
            xl = fnet_out_proj(mixed, fnet_out_w[j].astype(BF16), fnet_out_b[j], m_l[2], xl)
            if need_ctx:
                h_c = norm_modulate(xc, norm1_g[i], m_c[0], m_c[1], out_dtype=BF16, name="fnet_norm_ctx")
                xc = _fourier(h_c, b, lc, fnet_out_w[j], fnet_out_b[j], m_c[2], xc, None)
        else:
            lambda_init = 0.8 - 0.6 * math.exp(-0.3 * i)
            lp = diff_lambda[j].astype(F32)
            lam = jnp.exp(jnp.sum(lp[0] * lp[1])) - jnp.exp(jnp.sum(lp[2] * lp[3])) + lambda_init
            w_qkv = diff_qkv_w[j].astype(BF16)
            w_out = diff_out_w[j].astype(BF16)
            qkv_c = fused_matmul(xc, w_qkv, prologue="norm_mod", g=norm1_g[i], shift=m_c[0], scale=m_c[1],
                                 name="diff_qkv_ctx").reshape(b, lc, 3 * d)
            qkv_l = fused_matmul(xl, w_qkv, prologue="norm_mod", g=norm1_g[i], shift=m_l[0], scale=m_l[1],
                                 rows_per_batch=l, tm=1024, name="diff_qkv").reshape(b, l, 3 * d)
            o_l = diff_attention(qkv_l, qkv_l, qkv_c, lam, diff_subln_g[j], 1.0 - lambda_init, _rope_tables(l),
                                 name="diff_attention", **ATTN_TILES)
            xl = fused_matmul(o_l.reshape(b * l, d), w_out, gate=m_l[2], res=xl, rows_per_batch=l,
                              name="diff_out_proj")
            if need_ctx:
                o_c = diff_attention(qkv_c, None, qkv_c, lam, diff_subln_g[j], 1.0 - lambda_init, None,
                                     name="diff_attention_ctx", **ATTN_TILES_CTX)
                xc = fused_matmul(o_c.reshape(b * lc, d), w_out, gate=m_c[2], res=xc, name="diff_out_proj_ctx")

        w1, w3, w2 = (moe_w1, i), (moe_w3, i), (moe_w2, i)
        xl = _moe(xl, norm2_g[i], m_l[3], m_l[4], m_l[5], l, b, l, router_w[i], w1, w3, w2)
        if need_ctx:
            xc = _moe(xc, norm2_g[i], m_c[3], m_c[4], m_c[5], None, b, lc, router_w[i], w1, w3, w2)

    return norm_modulate(xl, final_g, name="final_norm").reshape(b, l, d)
```
